```python
import jax, jax.numpy as jnp
from jax import lax
import numpy as np

D_MODEL = 2048
BATCH = 2
SEQ = 8192
DEPTH = 1

N_META = 16
D_FF = 5632
FFN_RES = 0.5
EPS = 1e-6
DSA_HEADS = 8
DSA_HEAD_DIM = 128
DSA_WIDTH = DSA_HEADS * DSA_HEAD_DIM
IDX_HEADS = 16
IDX_DIM = 64
TOPK_MAX = 256
Q_BLOCK = 128
GLA_HEADS = 4
GLA_DK = 128
GLA_DV = 256
GLA_KEY_WIDTH = GLA_HEADS * GLA_DK
GLA_WIDTH = GLA_HEADS * GLA_DV
GLA_GATE_RANK = 16
GLA_TAU = 16.0
GLA_CHUNK = 64
MIX_WIDTH = DSA_WIDTH + GLA_WIDTH
IN_SPLITS = (DSA_WIDTH, DSA_WIDTH, DSA_WIDTH,
             IDX_HEADS * IDX_DIM, IDX_DIM, IDX_HEADS,
             GLA_KEY_WIDTH, GLA_KEY_WIDTH, GLA_WIDTH,
             GLA_GATE_RANK, GLA_WIDTH)
IN_WIDTH = sum(IN_SPLITS)

kernel_name = "hymba_dsa_gla_macaron_layer"


def rms_norm(x, g):
    x32 = x.astype(jnp.float32)
    y = x32 * lax.rsqrt(jnp.mean(x32 * x32, axis=-1, keepdims=True) + EPS)
    return (y * g.astype(jnp.float32)).astype(x.dtype)


def swiglu(x, w_gate, w_up, w_down):
    return (jax.nn.silu(x @ w_gate) * (x @ w_up)) @ w_down


def dsa_attention(q, k, v, q_idx, k_idx, w_idx, top_k):
    b, t, nh, dh = q.shape
    n_blk = -(-t // Q_BLOCK)
    t_pad = n_blk * Q_BLOCK

    def blocks(a):
        a = jnp.pad(a, [(0, 0), (0, t_pad - t)] + [(0, 0)] * (a.ndim - 2))
        a = a.reshape((b, n_blk, Q_BLOCK) + a.shape[2:])
        return jnp.moveaxis(a, 1, 0)

    pos_q = jnp.arange(t_pad, dtype=jnp.int32).reshape(n_blk, Q_BLOCK)
    key_pos = jnp.arange(t, dtype=jnp.int32)
    scale = DSA_HEAD_DIM ** -0.5
    gather = jax.vmap(lambda a, i: a[i])

    def one_block(args):
        qb, qib, wb, pq = args
        logits = jnp.einsum('bqhd,bsd->bqhs', qib, k_idx)
        score = jnp.einsum('bqhs,bqh->bqs', jax.nn.relu(logits), wb).astype(jnp.float32)
        causal = key_pos[None, :] <= pq[:, None]
        score = jnp.where(causal[None], score, -jnp.inf)
        _, sel = lax.top_k(score, top_k)
        valid = sel <= pq[None, :, None]
        ks = gather(k, sel)
        vs = gather(v, sel)
        s = jnp.einsum('bqhd,bqkhd->bhqk', qb, ks).astype(jnp.float32) * scale
        s = jnp.where(valid[:, None], s, -1e30)
        p = jax.nn.softmax(s, axis=-1).astype(v.dtype)
        return jnp.einsum('bhqk,bqkhd->bqhd', p, vs)

    out = lax.map(one_block, (blocks(q), blocks(q_idx), blocks(w_idx), pos_q))
    out = jnp.moveaxis(out, 0, 1).reshape(b, t_pad, nh, dh)
    return out[:, :t]


def gla_chunked(q, k, v, log_g):
    b, t, h, dk = q.shape
    dv = v.shape[-1]
    lead = GLA_CHUNK - N_META
    f32 = jnp.float32

    def prep(a):
        a = jnp.pad(a, [(0, 0), (lead, 0), (0, 0), (0, 0)])
        n = a.shape[1] // GLA_CHUNK
        a = a.reshape(b, n, GLA_CHUNK, h, a.shape[-1])
        return jnp.moveaxis(a, 1, 0)

    causal = jnp.tril(jnp.ones((GLA_CHUNK, GLA_CHUNK), dtype=bool))[None, :, :, None, None]

    def step(state, inp):
        qi, ki, vi, gi = inp
        qf, kf, vf = qi.astype(f32), ki.astype(f32), vi.astype(f32)
        bcum = jnp.cumsum(gi.astype(f32), axis=1)
        o_inter = jnp.einsum('bchk,bhkv->bchv', qf * jnp.exp(bcum), state)
        diff = bcum[:, :, None] - bcum[:, None, :]
        decay = jnp.exp(jnp.where(causal, diff, -jnp.inf))
        attn = jnp.einsum('bihk,bjhk,bijhk->bhij', qf, kf, decay)
        o_intra = jnp.einsum('bhij,bjhv->bihv', attn, vf)
        b_last = bcum[:, -1]
        k_dec = kf * jnp.exp(b_last[:, None] - bcum)
        state = state * jnp.exp(b_last)[..., None] + jnp.einsum('bchk,bchv->bhkv', k_dec, vf)
        return state, (o_inter + o_intra).astype(v.dtype)

    state0 = jnp.zeros((b, h, dk, dv), f32)
    _, out = lax.scan(step, state0, (prep(q), prep(k), prep(v), prep(log_g)))
    out = jnp.moveaxis(out, 0, 1).reshape(b, lead + t, h, dv)
    return out[:, lead:]


def hybrid_mixer(h, w_in, w_gate_up, b_gate, gla_norm_g, w_out, top_k):
    b, t, _ = h.shape
    proj = h @ w_in
    (dq, dk_, dv_, iq, ik, iw, gq, gk, gv, g_low, g_out) = jnp.split(
        proj, np.cumsum(IN_SPLITS)[:-1].tolist(), axis=-1)
    dq = dq.reshape(b, t, DSA_HEADS, DSA_HEAD_DIM)
    dk_ = dk_.reshape(b, t, DSA_HEADS, DSA_HEAD_DIM)
    dv_ = dv_.reshape(b, t, DSA_HEADS, DSA_HEAD_DIM)
    iq = iq.reshape(b, t, IDX_HEADS, IDX_DIM) * (IDX_DIM ** -0.5)
    iw = iw * (IDX_HEADS ** -0.5)
    o_dsa = dsa_attention(dq, dk_, dv_, iq, ik, iw, top_k).reshape(b, t, DSA_WIDTH)
    gq = gq.reshape(b, t, GLA_HEADS, GLA_DK) * (GLA_DK ** -0.5)
    gk = gk.reshape(b, t, GLA_HEADS, GLA_DK)
    gv = gv.reshape(b, t, GLA_HEADS, GLA_DV)
    gate_logit = (g_low @ w_gate_up + b_gate).astype(jnp.float32)
    log_g = (jax.nn.log_sigmoid(gate_logit) / GLA_TAU).reshape(b, t, GLA_HEADS, GLA_DK)
    o_gla = gla_chunked(gq, gk, gv, log_g)
    o_gla = rms_norm(o_gla, gla_norm_g) * jax.nn.silu(g_out.reshape(b, t, GLA_HEADS, GLA_DV))
    o_gla = o_gla.reshape(b, t, GLA_WIDTH)
    return jnp.concatenate([o_dsa, o_gla], axis=-1) @ w_out


def setup_inputs(seed: int = 0) -> dict:
    key = jax.random.key(seed)
    ks = jax.random.split(key, 20)
    f32 = jnp.float32

    def w(k, shape, fan_in):
        return jax.random.normal(k, shape, f32) * (fan_in ** -0.5)

    def gain(k, shape):
        return 1.0 + 0.02 * jax.random.normal(k, shape, f32)

    L = DEPTH
    return {
        "x": jax.random.normal(ks[0], (BATCH, SEQ, D_MODEL), f32),
        "meta_tokens": jax.random.normal(ks[1], (N_META, D_MODEL), f32),
        "ffn1_pre_g": gain(ks[2], (L, D_MODEL)),
        "ffn1_w_gate": w(ks[3], (L, D_MODEL, D_FF), D_MODEL),
        "ffn1_w_up": w(ks[4], (L, D_MODEL, D_FF), D_MODEL),
        "ffn1_w_down": w(ks[5], (L, D_FF, D_MODEL), D_FF),
        "ffn1_post_g": gain(ks[6], (L, D_MODEL)),
        "mix_pre_g": gain(ks[7], (L, D_MODEL)),
        "w_in": w(ks[8], (L, D_MODEL, IN_WIDTH), D_MODEL),
        "w_gate_up": w(ks[9], (L, GLA_GATE_RANK, GLA_KEY_WIDTH), GLA_GATE_RANK),
        "b_gate": 0.01 * jax.random.normal(ks[10], (L, GLA_KEY_WIDTH), f32),
        "gla_norm_g": gain(ks[11], (L, GLA_DV)),
        "w_out": w(ks[12], (L, MIX_WIDTH, D_MODEL), MIX_WIDTH),
        "mix_post_g": gain(ks[13], (L, D_MODEL)),
        "ffn2_pre_g": gain(ks[14], (L, D_MODEL)),
        "ffn2_w_gate": w(ks[15], (L, D_MODEL, D_FF), D_MODEL),
        "ffn2_w_up": w(ks[16], (L, D_MODEL, D_FF), D_MODEL),
        "ffn2_w_down": w(ks[17], (L, D_FF, D_MODEL), D_FF),
        "ffn2_post_g": gain(ks[18], (L, D_MODEL)),
    }


def reference(x, meta_tokens, ffn1_pre_g, ffn1_w_gate, ffn1_w_up, ffn1_w_down, ffn1_post_g,
              mix_pre_g, w_in, w_gate_up, b_gate, gla_norm_g, w_out, mix_post_g,
              ffn2_pre_g, ffn2_w_gate, ffn2_w_up, ffn2_w_down, ffn2_post_g):
    b = x.shape[0]
    top_k = min(TOPK_MAX, SEQ // 4)
    meta = jnp.broadcast_to(meta_tokens[None].astype(x.dtype), (b, N_META, x.shape[-1]))
    hs = jnp.concatenate([meta, x], axis=1)
    for l in range(DEPTH):
        f = swiglu(rms_norm(hs, ffn1_pre_g[l]), ffn1_w_gate[l], ffn1_w_up[l], ffn1_w_down[l])
        hs = hs + FFN_RES * rms_norm(f, ffn1_post_g[l])
        m = hybrid_mixer(rms_norm(hs, mix_pre_g[l]), w_in[l], w_gate_up[l], b_gate[l],
                         gla_norm_g[l], w_out[l], top_k)
        hs = hs + rms_norm(m, mix_post_g[l])
        f = swiglu(rms_norm(hs, ffn2_pre_g[l]), ffn2_w_gate[l], ffn2_w_up[l], ffn2_w_down[l])
        hs = hs + FFN_RES * rms_norm(f, ffn2_post_g[l])
    return hs[:, N_META:]
```

```python
import functools

import numpy as np
import jax
import jax.numpy as jnp
from jax import lax
from jax.experimental import pallas as pl
from jax.experimental.pallas import tpu as pltpu

F32, BF16, I32 = jnp.float32, jnp.bfloat16, jnp.int32

N_META = 16
FFN_RES = 0.5
EPS = 1e-6
DSA_HEADS = 8
DSA_HEAD_DIM = 128
DSA_WIDTH = DSA_HEADS * DSA_HEAD_DIM
IDX_HEADS = 16
IDX_DIM = 64
TOPK_MAX = 256
GLA_HEADS = 4
GLA_DK = 128
GLA_DV = 256
GLA_KEY_WIDTH = GLA_HEADS * GLA_DK
GLA_WIDTH = GLA_HEADS * GLA_DV
GLA_GATE_RANK = 16
GLA_TAU = 16.0
GLA_CHUNK = 64
IN_SPLITS = (DSA_WIDTH, DSA_WIDTH, DSA_WIDTH, IDX_HEADS * IDX_DIM, IDX_DIM, IDX_HEADS,
             GLA_KEY_WIDTH, GLA_KEY_WIDTH, GLA_WIDTH, GLA_GATE_RANK, GLA_WIDTH)

LANES = 128
SLAB = 256
VMEM_LIMIT_BYTES = 56 * 1024 * 1024

P16_DQ, P16_DK, P16_DV, P16_IQ, P16_GV, P16_IK2_SLAB, P16_SLABS = 0, 1, 2, 3, 4, 20, 21
P32_GQ_SLAB, P32_GK_SLAB, P32_GOUT_SLAB, P32_SMALL_SLAB, P32_SLABS = 0, 2, 4, 8, 9
SLABS_PER_STEP = 3

NEG_BIAS = -1e30
INT_MIN = -(2 ** 31)
KEY_NEG_INF = int(np.array(0xFF800000 ^ 0x7FFFFFFF, dtype=np.uint32).view(np.int32))


def _params(sem):
    return pltpu.CompilerParams(dimension_semantics=sem, vmem_limit_bytes=VMEM_LIMIT_BYTES)


def _dot(a, b):
    return jnp.dot(a, b, preferred_element_type=F32)


def _dot_nt(a, b):
    return lax.dot_general(a, b, (((1,), (1,)), ((), ())), preferred_element_type=F32)


def _dot_tn(a, b):
    return lax.dot_general(a, b, (((0,), (0,)), ((), ())), preferred_element_type=F32)


def _split3(x):
    hi = x.astype(BF16)
    r = x - hi.astype(F32)
    mid = r.astype(BF16)
    lo = (r - mid.astype(F32)).astype(BF16)
    return hi, mid, lo


def _dot_exact_lhs01(l01, x):
    hi, mid, lo = _split3(x)
    return _dot(l01, hi) + _dot(l01, mid) + _dot(l01, lo)


def _dot_f32(a, b):
    ah, am, al = _split3(a)
    bh, bm, bl = _split3(b)
    return (_dot(ah, bh) + (_dot(ah, bm) + _dot(am, bh))
            + (_dot(ah, bl) + _dot(am, bm) + _dot(al, bh)))


def _rms(x, g):
    return x * lax.rsqrt(jnp.mean(x * x, axis=-1, keepdims=True) + EPS) * g


def _ffn_kernel(x_ref, preg_ref, wg_ref, wu_ref, wd_ref, postg_ref, o_ref, xn_ref, acc_ref, *, nf):
    f = pl.program_id(1)

    @pl.when(f == 0)
    def _():
        xn_ref[...] = _rms(x_ref[...], preg_ref[...]).astype(BF16)
        acc_ref[...] = jnp.zeros_like(acc_ref)

    xn = xn_ref[...]
    g = _dot(xn, wg_ref[...])
    u = _dot(xn, wu_ref[...])
    a = (g * jax.nn.sigmoid(g)) * u
    acc_ref[...] += _dot(a.astype(BF16), wd_ref[...])

    @pl.when(f == nf - 1)
    def _():
        o_ref[...] = x_ref[...] + FFN_RES * _rms(acc_ref[...], postg_ref[...])


def _ffn(x, pre_g, wg, wu, wd, post_g, tm, tf):
    r, d = x.shape
    nf = wg.shape[1] // tf
    return pl.pallas_call(
        functools.partial(_ffn_kernel, nf=nf),
        out_shape=jax.ShapeDtypeStruct((r, d), F32),
        grid=(r // tm, nf),
        in_specs=[
            pl.BlockSpec((tm, d), lambda i, f: (i, 0)),
            pl.BlockSpec((1, d), lambda i, f: (0, 0)),
            pl.BlockSpec((d, tf), lambda i, f: (0, f)),
            pl.BlockSpec((d, tf), lambda i, f: (0, f)),
            pl.BlockSpec((tf, d), lambda i, f: (f, 0)),
            pl.BlockSpec((1, d), lambda i, f: (0, 0)),
        ],
        out_specs=pl.BlockSpec((tm, d), lambda i, f: (i, 0)),
        scratch_shapes=[pltpu.VMEM((tm, d), BF16), pltpu.VMEM((tm, d), F32)],
        compiler_params=_params(("parallel", "arbitrary")),
        name="ffn",
    )(x, pre_g, wg, wu, wd, post_g)


def _proj_kernel(x_ref, g_ref, w_ref, o_ref, xn_ref):
    @pl.when(pl.program_id(1) == 0)
    def _():
        xn_ref[...] = _rms(x_ref[...], g_ref[...]).astype(BF16)

    res = _dot(xn_ref[...], w_ref[...])
    for s in range(SLABS_PER_STEP):
        o_ref[s] = res[:, s * SLAB:(s + 1) * SLAB].astype(o_ref.dtype)


def _proj(x, g, w, out_dtype, tm):
    r, d = x.shape
    n_slabs = w.shape[1] // SLAB
    tn = SLABS_PER_STEP * SLAB
    return pl.pallas_call(
        _proj_kernel,
        out_shape=jax.ShapeDtypeStruct((n_slabs, r, SLAB), out_dtype),
        grid=(r // tm, n_slabs // SLABS_PER_STEP),
        in_specs=[
            pl.BlockSpec((tm, d), lambda i, j: (i, 0)),
            pl.BlockSpec((1, d), lambda i, j: (0, 0)),
            pl.BlockSpec((d, tn), lambda i, j: (0, j)),
        ],
        out_specs=pl.BlockSpec((SLABS_PER_STEP, tm, SLAB), lambda i, j: (j, i, 0)),
        scratch_shapes=[pltpu.VMEM((tm, d), BF16)],
        compiler_params=_params(("parallel", "arbitrary")),
        name="proj",
    )(x, g, w)


IDX_TQ = 128
IDX_CW = 256


def _sortable(x):
    i = lax.bitcast_convert_type(x, I32)
    return jnp.where(i < 0, i ^ 0x7FFFFFFF, i)


def _idx_kernel(iq_ref, sm_ref, ik_ref, ikm_ref, bias_ref, keys_ref, wb_ref, *, seq, topk):
    qi = pl.program_id(1)
    nblk = seq // IDX_CW
    n_x = qi // (IDX_CW // IDX_TQ) + 1

    iw = sm_ref[0][:, 0:LANES]
    for h in range(IDX_HEADS):
        wb_ref[h] = jnp.broadcast_to(iw[:, h:h + 1], (IDX_TQ, IDX_CW))

    row = qi * IDX_TQ + lax.broadcasted_iota(I32, (IDX_TQ, IDX_CW), 0)
    col = lax.broadcasted_iota(I32, (IDX_TQ, IDX_CW), 1)

    def score_tile(k2):
        k_even, k_odd = k2[:, 0:LANES], k2[:, LANES:2 * LANES]
        acc = jnp.zeros((IDX_TQ, IDX_CW), F32)
        for p in range(IDX_HEADS // 2):
            lhs = iq_ref[p // 2][:, (p % 2) * LANES:(p % 2 + 1) * LANES]
            acc = acc + jnp.maximum(_dot_nt(lhs, k_even), 0.0) * wb_ref[2 * p]
            acc = acc + jnp.maximum(_dot_nt(lhs, k_odd), 0.0) * wb_ref[2 * p + 1]
        return acc

    def x_tile(c, carry):
        k2 = ik_ref[0, pl.ds(pl.multiple_of(c * IDX_CW, IDX_CW), IDX_CW), :]
        acc = jnp.where(c * IDX_CW + col <= row, score_tile(k2), -jnp.inf)
        keys_ref[c] = _sortable(acc)
        return carry

    lax.fori_loop(0, n_x, x_tile, 0)
    keys_ref[nblk] = _sortable(jnp.where(col < N_META, score_tile(ikm_ref[...]), -jnp.inf))

    def count_ge(cand):
        cand_w = jnp.concatenate([cand, cand], axis=1)

        def body(c, cnt):
            return cnt + jnp.where(keys_ref[c] >= cand_w, 1.0, 0.0)

        cnt = lax.fori_loop(0, n_x, body, jnp.zeros((IDX_TQ, IDX_CW), F32))
        cnt = cnt + jnp.where(keys_ref[nblk] >= cand_w, 1.0, 0.0)
        return jnp.sum(cnt, axis=1, keepdims=True)

    def bisect(it, thr):
        cand = thr ^ lax.shift_left(jnp.int32(1), 31 - it)
        return jnp.where(count_ge(cand) >= float(topk), cand, thr)

    thr = lax.fori_loop(0, 32, bisect, jnp.full((IDX_TQ, LANES), INT_MIN, I32))
    thr = jnp.maximum(thr, KEY_NEG_INF + 1)
    thr_w = jnp.concatenate([thr, thr], axis=1)

    for c in range(nblk):
        @pl.when(c < n_x)
        def _(c=c):
            bias_ref[0, :, c * IDX_CW:(c + 1) * IDX_CW] = jnp.where(
                keys_ref[c] >= thr_w, 0.0, NEG_BIAS).astype(BF16)

        @pl.when(c >= n_x)
        def _(c=c):
            bias_ref[0, :, c * IDX_CW:(c + 1) * IDX_CW] = jnp.full((IDX_TQ, IDX_CW), NEG_BIAS, BF16)

    bias_ref[0, :, seq:seq + LANES] = jnp.where(
        keys_ref[nblk][:, 0:LANES] >= thr, 0.0, NEG_BIAS).astype(BF16)


def _idx(p16, p32, ik2_meta, batch, seq, topk):
    nq = seq // IDX_TQ
    return pl.pallas_call(
        functools.partial(_idx_kernel, seq=seq, topk=topk),
        out_shape=jax.ShapeDtypeStruct((batch, seq, seq + LANES), BF16),
        grid=(batch, nq),
        in_specs=[
            pl.BlockSpec((4, IDX_TQ, SLAB), lambda b, q: (P16_IQ, b * nq + q, 0)),
            pl.BlockSpec((1, IDX_TQ, SLAB), lambda b, q: (P32_SMALL_SLAB, b * nq + q, 0)),
            pl.BlockSpec((1, seq, SLAB), lambda b, q: (P16_IK2_SLAB, b, 0)),
            pl.BlockSpec((IDX_CW, SLAB), lambda b, q: (0, 0)),
        ],
        out_specs=pl.BlockSpec((1, IDX_TQ, seq + LANES), lambda b, q: (b, q, 0)),
        scratch_shapes=[
            pltpu.VMEM((seq // IDX_CW + 1, IDX_TQ, IDX_CW), I32),
            pltpu.VMEM((IDX_HEADS, IDX_TQ, IDX_CW), F32),
        ],
        compiler_params=_params(("parallel", "arbitrary")),
        name="idx",
    )(p16, p32, p16, ik2_meta)


def _attn_kernel(qt_ref, kt_ref, q_ref, k_ref, v_ref, b_ref, bm_ref, km_ref, vm_ref, o_ref,
                 m_ref, l_ref, acc_ref, *, tq, tk):
    p = pl.program_id(1)
    qi, ki = qt_ref[p], kt_ref[p]
    scale = DSA_HEAD_DIM ** -0.5

    def head(ref, h):
        return ref[h // 2][:, (h % 2) * LANES:(h % 2 + 1) * LANES]

    def accumulate(kref, vref, bias):
        for h in range(DSA_HEADS):
            s = _dot_nt(head(q_ref, h), head(kref, h)) * scale + bias
            m_prev = m_ref[h]
            m_new = jnp.maximum(m_prev, jnp.max(s, axis=1, keepdims=True))
            alpha = jnp.exp(m_prev - m_new)
            pr = jnp.exp(s - m_new)
            l_ref[h] = alpha * l_ref[h] + jnp.sum(pr, axis=1, keepdims=True)
            hs = slice(h * DSA_HEAD_DIM, (h + 1) * DSA_HEAD_DIM)
            acc_ref[:, hs] = alpha * acc_ref[:, hs] + _dot(pr.astype(BF16), head(vref, h))
            m_ref[h] = m_new

    @pl.when(ki == 0)
    def _():
        m_ref[...] = jnp.full(m_ref.shape, -jnp.inf, F32)
        l_ref[...] = jnp.zeros_like(l_ref)
        acc_ref[...] = jnp.zeros_like(acc_ref)
        accumulate(km_ref, vm_ref, bm_ref[0].astype(F32))

    accumulate(k_ref, v_ref, b_ref[0].astype(F32))

    @pl.when(ki == (qi * tq + tq - 1) // tk)
    def _():
        for h in range(DSA_HEADS):
            hs = slice(h * DSA_HEAD_DIM, (h + 1) * DSA_HEAD_DIM)
            o_ref[:, hs] = (acc_ref[:, hs] / l_ref[h]).astype(o_ref.dtype)


def _attn(p16, bias, p16_meta, batch, seq, tq, tk):
    nq, nk = seq // tq, seq // tk
    pairs = [(q, k) for q in range(nq) for k in range((q * tq + tq - 1) // tk + 1)]
    q_tab = jnp.asarray([q for q, _ in pairs], I32)
    k_tab = jnp.asarray([k for _, k in pairs], I32)
    grid_spec = pltpu.PrefetchScalarGridSpec(
        num_scalar_prefetch=2,
        grid=(batch, len(pairs)),
        in_specs=[
            pl.BlockSpec((4, tq, SLAB), lambda b, p, qt, kt: (P16_DQ, b * nq + qt[p], 0)),
            pl.BlockSpec((4, tk, SLAB), lambda b, p, qt, kt: (P16_DK, b * nk + kt[p], 0)),
            pl.BlockSpec((4, tk, SLAB), lambda b, p, qt, kt: (P16_DV, b * nk + kt[p], 0)),
            pl.BlockSpec((1, tq, tk), lambda b, p, qt, kt: (b, qt[p], kt[p])),
            pl.BlockSpec((1, tq, LANES), lambda b, p, qt, kt: (b, qt[p], seq // LANES)),
            pl.BlockSpec((4, LANES, SLAB), lambda b, p, qt, kt: (P16_DK, 0, 0)),
            pl.BlockSpec((4, LANES, SLAB), lambda b, p, qt, kt: (P16_DV, 0, 0)),
        ],
        out_specs=pl.BlockSpec((tq, DSA_WIDTH), lambda b, p, qt, kt: (b * nq + qt[p], 0)),
        scratch_shapes=[
            pltpu.VMEM((DSA_HEADS, tq, 1), F32),
            pltpu.VMEM((DSA_HEADS, tq, 1), F32),
            pltpu.VMEM((tq, DSA_WIDTH), F32),
        ],
    )
    return pl.pallas_call(
        functools.partial(_attn_kernel, tq=tq, tk=tk),
        out_shape=jax.ShapeDtypeStruct((batch * seq, DSA_WIDTH), BF16),
        grid_spec=grid_spec,
        compiler_params=_params(("parallel", "arbitrary")),
        name="attn",
    )(q_tab, k_tab, p16, p16, p16, bias, bias, p16_meta, p16_meta)


def _log_sigmoid(x):
    return jnp.minimum(x, 0.0) - jnp.log1p(jnp.exp(-jnp.abs(x)))


def _gla_kernel(gq_ref, gk_ref, go_ref, sm_ref, gv_ref, mk_ref, mv_ref, msm_ref, wgate_ref, bgate_ref,
                ng_ref, o_ref, st_ref, bc_ref, k_ref):
    c = GLA_CHUNK

    def log_decay(small):
        return _log_sigmoid(_dot_f32(small, wgate_ref[...]) + bgate_ref[...]) / GLA_TAU

    def lower_ones(n):
        r = lax.broadcasted_iota(I32, (n, n), 0)
        cc = lax.broadcasted_iota(I32, (n, n), 1)
        return r >= cc

    @pl.when(pl.program_id(1) == 0)
    def _():
        bcm = _dot_exact_lhs01(jnp.where(lower_ones(N_META), 1.0, 0.0).astype(BF16),
                               log_decay(msm_ref[0]))
        for h in range(GLA_HEADS):
            ks = slice(h * GLA_DK, (h + 1) * GLA_DK)
            kd = mk_ref[h // 2][:, (h % 2) * GLA_DK:(h % 2 + 1) * GLA_DK] * jnp.exp(
                bcm[N_META - 1:N_META, ks] - bcm[:, ks])
            st_ref[h] = _dot_tn(mv_ref[h], kd.astype(BF16))

    causal = lower_ones(c)
    bc_ref[...] = _dot_exact_lhs01(jnp.where(causal, 1.0, 0.0).astype(BF16), log_decay(sm_ref[0]))
    lane = lax.broadcasted_iota(I32, (c, c), 1)

    for h in range(GLA_HEADS):
        ks = slice(h * GLA_DK, (h + 1) * GLA_DK)
        hsl = slice((h % 2) * GLA_DK, (h % 2 + 1) * GLA_DK)
        q = gq_ref[h // 2][:, hsl] * (GLA_DK ** -0.5)
        k = gk_ref[h // 2][:, hsl]
        v = gv_ref[h]
        bc = bc_ref[:, ks]
        k_ref[...] = k

        attn = jnp.zeros((c, c), F32)
        for j in range(c):
            r0 = (j // 8) * 8
            e = jnp.exp(jnp.minimum(bc[r0:] - bc_ref[j:j + 1, ks], 0.0))
            colv = jnp.sum(q[r0:] * k_ref[j:j + 1, :] * e, axis=1, keepdims=True)
            if r0:
                colv = jnp.concatenate([jnp.zeros((r0, 1), F32), colv], axis=0)
            attn = jnp.where(lane == j, colv, attn)
        attn = jnp.where(causal, attn, 0.0)

        state = st_ref[h]
        b_last = bc[c - 1:c, :]
        o = _dot_nt((q * jnp.exp(bc)).astype(BF16), state.astype(BF16))
        o = o + _dot(attn.astype(BF16), v)
        kd = k * jnp.exp(b_last - bc)
        st_ref[h] = state * jnp.exp(b_last) + _dot_tn(v, kd.astype(BF16))

        g = go_ref[h]
        y = _rms(o, ng_ref[...]) * (g * jax.nn.sigmoid(g))
        o_ref[:, h * GLA_DV:(h + 1) * GLA_DV] = y.astype(o_ref.dtype)


def _gla(p32, p16, p32_meta, p16_meta, w_gate_pad, b_gate, norm_g, batch, seq):
    c = GLA_CHUNK
    ns = seq // c
    row = lambda b, i: b * ns + i
    return pl.pallas_call(
        _gla_kernel,
        out_shape=jax.ShapeDtypeStruct((batch * seq, GLA_WIDTH), BF16),
        grid=(batch, ns),
        in_specs=[
            pl.BlockSpec((2, c, SLAB), lambda b, i: (P32_GQ_SLAB // 2, row(b, i), 0)),
            pl.BlockSpec((2, c, SLAB), lambda b, i: (P32_GK_SLAB // 2, row(b, i), 0)),
            pl.BlockSpec((4, c, SLAB), lambda b, i: (P32_GOUT_SLAB // 4, row(b, i), 0)),
            pl.BlockSpec((1, c, SLAB), lambda b, i: (P32_SMALL_SLAB, row(b, i), 0)),
            pl.BlockSpec((4, c, SLAB), lambda b, i: (P16_GV, row(b, i), 0)),
            pl.BlockSpec((2, N_META, SLAB), lambda b, i: (P32_GK_SLAB // 2, 0, 0)),
            pl.BlockSpec((4, N_META, SLAB), lambda b, i: (P16_GV, 0, 0)),
            pl.BlockSpec((1, N_META, SLAB), lambda b, i: (P32_SMALL_SLAB, 0, 0)),
            pl.BlockSpec((SLAB, GLA_KEY_WIDTH), lambda b, i: (0, 0)),
            pl.BlockSpec((1, GLA_KEY_WIDTH), lambda b, i: (0, 0)),
            pl.BlockSpec((1, GLA_DV), lambda b, i: (0, 0)),
        ],
        out_specs=pl.BlockSpec((c, GLA_WIDTH), lambda b, i: (row(b, i), 0)),
        scratch_shapes=[
            pltpu.VMEM((GLA_HEADS, GLA_DV, GLA_DK), F32),
            pltpu.VMEM((c, GLA_KEY_WIDTH), F32),
            pltpu.VMEM((c, GLA_DK), F32),
        ],
        compiler_params=_params(("parallel", "arbitrary")),
        name="gla",
    )(p32, p32, p32, p32, p16, p32_meta, p16_meta, p32_meta, w_gate_pad, b_gate, norm_g)


def _out_kernel(h_ref, a_ref, g_ref, wa_ref, wg_ref, pg_ref, o_ref):
    m = _dot(a_ref[...], wa_ref[...]) + _dot(g_ref[...], wg_ref[...])
    o_ref[...] = h_ref[...] + _rms(m, pg_ref[...])


def _out(hs, o_dsa, o_gla, w_dsa, w_gla, post_g, tm):
    r, d = hs.shape
    return pl.pallas_call(
        _out_kernel,
        out_shape=jax.ShapeDtypeStruct((r, d), F32),
        grid=(r // tm,),
        in_specs=[
            pl.BlockSpec((tm, d), lambda i: (i, 0)),
            pl.BlockSpec((tm, DSA_WIDTH), lambda i: (i, 0)),
            pl.BlockSpec((tm, GLA_WIDTH), lambda i: (i, 0)),
            pl.BlockSpec((DSA_WIDTH, d), lambda i: (0, 0)),
            pl.BlockSpec((GLA_WIDTH, d), lambda i: (0, 0)),
            pl.BlockSpec((1, d), lambda i: (0, 0)),
        ],
        out_specs=pl.BlockSpec((tm, d), lambda i: (i, 0)),
        compiler_params=_params(("parallel",)),
        name="outproj",
    )(hs, o_dsa, o_gla, w_dsa, w_gla, post_g)


def _tile(n, pref):
    t = min(n, pref)
    while n % t:
        t //= 2
    return t


def _proj_weights(w_in):
    d = w_in.shape[0]
    offs = np.cumsum((0,) + IN_SPLITS)
    dq, dk, dv, iq, ik, iw, gq, gk, gv, glow, gout = (w_in[:, offs[i]:offs[i + 1]] for i in range(11))
    z = lambda n: jnp.zeros((d, n), w_in.dtype)
    ik2 = jnp.concatenate([ik, z(IDX_DIM), z(IDX_DIM), ik], axis=1)
    w16 = jnp.concatenate([dq, dk, dv, iq * (IDX_DIM ** -0.5), gv, ik2], axis=1)
    small = jnp.concatenate([iw * (IDX_HEADS ** -0.5), glow, z(SLAB - IDX_HEADS - GLA_GATE_RANK)], axis=1)
    w32 = jnp.concatenate([gq, gk, gout, small], axis=1)
    return w16.astype(BF16), w32.astype(BF16)


def kernel(x, meta_tokens, ffn1_pre_g, ffn1_w_gate, ffn1_w_up, ffn1_w_down, ffn1_post_g, mix_pre_g, w_in, w_gate_up, b_gate, gla_norm_g, w_out, mix_post_g, ffn2_pre_g, ffn2_w_gate, ffn2_w_up, ffn2_w_down, ffn2_post_g):
    batch, seq, d = x.shape
    depth = w_in.shape[0]
    assert depth == 1, "the meta rows skip the mixer, which is only valid for the last layer"
    d_ff = ffn1_w_gate.shape[-1]
    topk = min(TOPK_MAX, seq // 4)
    rows = batch * seq
    tm = _tile(rows, 512)
    tf = _tile(d_ff, 512)
    tq = _tile(seq, 256)
    tk = _tile(seq, 512)

    hs = x.reshape(rows, d)
    hm = meta_tokens.astype(x.dtype)
    row2 = lambda v: v.reshape(1, -1)
    for l in range(depth):
        ffn1 = (row2(ffn1_pre_g[l]), ffn1_w_gate[l].astype(BF16), ffn1_w_up[l].astype(BF16),
                ffn1_w_down[l].astype(BF16), row2(ffn1_post_g[l]))
        hs = _ffn(hs, *ffn1, tm=tm, tf=tf)
        hm = _ffn(hm, *ffn1, tm=N_META, tf=tf)

        w16, w32 = _proj_weights(w_in[l])
        pre = row2(mix_pre_g[l])
        p16, p32 = _proj(hs, pre, w16, BF16, tm), _proj(hs, pre, w32, F32, tm)
        p16m, p32m = _proj(hm, pre, w16, BF16, N_META), _proj(hm, pre, w32, F32, N_META)

        pad_rows = lambda a, n: jnp.pad(a, ((0, 0), (0, n - a.shape[1]), (0, 0)))
        bias = _idx(p16, p32, pad_rows(p16m, IDX_CW)[P16_IK2_SLAB], batch, seq, topk)
        o_dsa = _attn(p16, bias, pad_rows(p16m, LANES), batch, seq, tq, tk)

        w_gate_pad = jnp.zeros((SLAB, GLA_KEY_WIDTH), F32).at[
            IDX_HEADS:IDX_HEADS + GLA_GATE_RANK].set(w_gate_up[l])
        o_gla = _gla(p32, p16, p32m, p16m, w_gate_pad, row2(b_gate[l]), row2(gla_norm_g[l]), batch, seq)

        wo = w_out[l].astype(BF16)
        hs = _out(hs, o_dsa, o_gla, wo[:DSA_WIDTH], wo[DSA_WIDTH:], row2(mix_post_g[l]), tm)
        ffn2 = (row2(ffn2_pre_g[l]), ffn2_w_gate[l].astype(BF16), ffn2_w_up[l].astype(BF16),
                ffn2_w_down[l].astype(BF16), row2(ffn2_post_g[l]))
        hs = _ffn(hs, *ffn2, tm=tm, tf=tf)
    return hs.reshape(batch, seq, d)
```

```python
import functools

import numpy as np
import jax
import jax.numpy as jnp
from jax import lax
from jax.experimental import pallas as pl
from jax.experimental.pallas import tpu as pltpu

F32, BF16, I32 = jnp.float32, jnp.bfloat16, jnp.int32

N_META = 16
FFN_RES = 0.5
EPS = 1e-6
DSA_HEADS = 8
DSA_HEAD_DIM = 128
DSA_WIDTH = DSA_HEADS * DSA_HEAD_DIM
IDX_HEADS = 16
IDX_DIM = 64
TOPK_MAX = 256
GLA_HEADS = 4
GLA_DK = 128
GLA_DV = 256
GLA_KEY_WIDTH = GLA_HEADS * GLA_DK
GLA_WIDTH = GLA_HEADS * GLA_DV
GLA_GATE_RANK = 16
GLA_TAU = 16.0
GLA_CHUNK = 64
IN_SPLITS = (DSA_WIDTH, DSA_WIDTH, DSA_WIDTH, IDX_HEADS * IDX_DIM, IDX_DIM, IDX_HEADS,
             GLA_KEY_WIDTH, GLA_KEY_WIDTH, GLA_WIDTH, GLA_GATE_RANK, GLA_WIDTH)

LANES = 128
SLAB = 256
VMEM_LIMIT_BYTES = 56 * 1024 * 1024

P16_DQ, P16_DK, P16_DV, P16_IQ, P16_GV, P16_IK2_SLAB, P16_SLABS = 0, 1, 2, 3, 4, 20, 21
P32_GQ_SLAB, P32_GK_SLAB, P32_GOUT_SLAB, P32_SMALL_SLAB, P32_SLABS = 0, 2, 4, 8, 9
SLABS_PER_STEP = 3

NEG_BIAS = -1e30
INT_MIN = -(2 ** 31)
KEY_NEG_INF = int(np.array(0xFF800000 ^ 0x7FFFFFFF, dtype=np.uint32).view(np.int32))


def _params(sem):
    return pltpu.CompilerParams(dimension_semantics=sem, vmem_limit_bytes=VMEM_LIMIT_BYTES)


def _dot(a, b):
    return jnp.dot(a, b, preferred_element_type=F32)


def _dot_nt(a, b):
    return lax.dot_general(a, b, (((1,), (1,)), ((), ())), preferred_element_type=F32)


def _dot_tn(a, b):
    return lax.dot_general(a, b, (((0,), (0,)), ((), ())), preferred_element_type=F32)


def _split3(x):
    hi = x.astype(BF16)
    r = x - hi.astype(F32)
    mid = r.astype(BF16)
    lo = (r - mid.astype(F32)).astype(BF16)
    return hi, mid, lo


def _dot_exact_lhs01(l01, x):
    hi, mid, lo = _split3(x)
    return _dot(l01, hi) + _dot(l01, mid) + _dot(l01, lo)


def _dot_f32(a, b):
    ah, am, al = _split3(a)
    bh, bm, bl = _split3(b)
    return (_dot(ah, bh) + (_dot(ah, bm) + _dot(am, bh))
            + (_dot(ah, bl) + _dot(am, bm) + _dot(al, bh)))


def _rms(x, g):
    return x * lax.rsqrt(jnp.mean(x * x, axis=-1, keepdims=True) + EPS) * g


def _ffn_kernel(x_ref, preg_ref, wg_ref, wu_ref, wd_ref, postg_ref, o_ref, xn_ref, acc_ref, *, nf):
    f = pl.program_id(1)

    @pl.when(f == 0)
    def _():
        xn_ref[...] = _rms(x_ref[...], preg_ref[...]).astype(BF16)
        acc_ref[...] = jnp.zeros_like(acc_ref)

    xn = xn_ref[...]
    g = _dot(xn, wg_ref[...])
    u = _dot(xn, wu_ref[...])
    a = (g * jax.nn.sigmoid(g)) * u
    acc_ref[...] += _dot(a.astype(BF16), wd_ref[...])

    @pl.when(f == nf - 1)
    def _():
        o_ref[...] = x_ref[...] + FFN_RES * _rms(acc_ref[...], postg_ref[...])


def _ffn(x, pre_g, wg, wu, wd, post_g, tm, tf):
    r, d = x.shape
    nf = wg.shape[1] // tf
    return pl.pallas_call(
        functools.partial(_ffn_kernel, nf=nf),
        out_shape=jax.ShapeDtypeStruct((r, d), F32),
        grid=(r // tm, nf),
        in_specs=[
            pl.BlockSpec((tm, d), lambda i, f: (i, 0)),
            pl.BlockSpec((1, d), lambda i, f: (0, 0)),
            pl.BlockSpec((d, tf), lambda i, f: (0, f)),
            pl.BlockSpec((d, tf), lambda i, f: (0, f)),
            pl.BlockSpec((tf, d), lambda i, f: (f, 0)),
            pl.BlockSpec((1, d), lambda i, f: (0, 0)),
        ],
        out_specs=pl.BlockSpec((tm, d), lambda i, f: (i, 0)),
        scratch_shapes=[pltpu.VMEM((tm, d), BF16), pltpu.VMEM((tm, d), F32)],
        compiler_params=_params(("parallel", "arbitrary")),
        name="ffn",
    )(x, pre_g, wg, wu, wd, post_g)


def _proj_kernel(x_ref, g_ref, w_ref, o_ref, xn_ref):
    @pl.when(pl.program_id(1) == 0)
    def _():
        xn_ref[...] = _rms(x_ref[...], g_ref[...]).astype(BF16)

    res = _dot(xn_ref[...], w_ref[...])
    for s in range(SLABS_PER_STEP):
        o_ref[s] = res[:, s * SLAB:(s + 1) * SLAB].astype(o_ref.dtype)


def _proj(x, g, w, out_dtype, tm):
    r, d = x.shape
    n_slabs = w.shape[1] // SLAB
    tn = SLABS_PER_STEP * SLAB
    return pl.pallas_call(
        _proj_kernel,
        out_shape=jax.ShapeDtypeStruct((n_slabs, r, SLAB), out_dtype),
        grid=(r // tm, n_slabs // SLABS_PER_STEP),
        in_specs=[
            pl.BlockSpec((tm, d), lambda i, j: (i, 0)),
            pl.BlockSpec((1, d), lambda i, j: (0, 0)),
            pl.BlockSpec((d, tn), lambda i, j: (0, j)),
        ],
        out_specs=pl.BlockSpec((SLABS_PER_STEP, tm, SLAB), lambda i, j: (j, i, 0)),
        scratch_shapes=[pltpu.VMEM((tm, d), BF16)],
        compiler_params=_params(("parallel", "arbitrary")),
        name="proj",
    )(x, g, w)


IDX_TQ = 256
IDX_CW = 256
BF16_KEY_MIN, BF16_KEY_MAX = -(2 ** 15), 2 ** 15 - 1
STAGE2_BITS = 18


def _key32(pattern):
    return jnp.where(pattern < 0, pattern ^ 0x7FFFFFFF, pattern)


def _pattern_of_key16(k16):
    return lax.shift_left(jnp.where(k16 < 0, k16 ^ 0x7FFF, k16), 16)


def _idx_kernel(iq_ref, sm_ref, ik_ref, ikm_ref, bias_ref, sc_ref, sc16_ref, w_ref, *, seq, topk):
    qi = pl.program_id(1)
    tq, cw = IDX_TQ, IDX_CW
    nt = seq // cw
    n_x = qi + 1
    kf = float(topk)

    w_ref[...] = sm_ref[0][:, 0:LANES].T
    key_pos = lax.broadcasted_iota(I32, (cw, tq), 0)
    q_pos = qi * tq + lax.broadcasted_iota(I32, (cw, tq), 1)

    def score_tile(k2):
        lhs = jnp.concatenate([k2[:, 0:LANES], k2[:, LANES:2 * LANES]], axis=0)
        acc = jnp.zeros((cw, tq), F32)
        for p in range(IDX_HEADS // 2):
            rhs = iq_ref[p // 2][:, (p % 2) * LANES:(p % 2 + 1) * LANES]
            l2 = _dot_nt(lhs, rhs)
            acc = acc + jnp.maximum(l2[0:cw], 0.0) * w_ref[2 * p:2 * p + 1, :]
            acc = acc + jnp.maximum(l2[cw:2 * cw], 0.0) * w_ref[2 * p + 1:2 * p + 2, :]
        return acc

    def x_tile(c, carry):
        k2 = ik_ref[0, pl.ds(pl.multiple_of(c * cw, cw), cw), :]
        acc = jnp.where(c * cw + key_pos <= q_pos, score_tile(k2), -jnp.inf)
        sc_ref[c] = acc
        sc16_ref[c] = acc.astype(BF16)
        return carry

    lax.fori_loop(0, n_x, x_tile, 0)
    acc_m = jnp.where(key_pos < N_META, score_tile(ikm_ref[...]), -jnp.inf)
    sc_ref[nt] = acc_m
    sc16_ref[nt] = acc_m.astype(BF16)

    one16, zero16 = jnp.ones((cw, tq), BF16), jnp.zeros((cw, tq), BF16)

    def count16(c16):
        def tile_count(c):
            hit = jnp.where(sc16_ref[c] >= c16, one16, zero16)
            return jnp.sum(hit.reshape(cw // 16, 16, tq), axis=0).astype(F32)

        cnt = lax.fori_loop(0, n_x, lambda c, a: a + tile_count(c), jnp.zeros((16, tq), F32))
        return jnp.sum(cnt + tile_count(nt), axis=0, keepdims=True)

    def stage1(it, k16):
        cand = k16 + lax.shift_left(jnp.int32(1), 15 - it)
        c32 = lax.bitcast_convert_type(_pattern_of_key16(cand), F32)
        c16 = jnp.broadcast_to(c32, (cw, tq)).astype(BF16)
        return jnp.where(count16(c16) >= kf, cand, k16)

    k16 = lax.fori_loop(0, 16, stage1, jnp.full((1, tq), BF16_KEY_MIN, I32))

    lo = _key32(_pattern_of_key16(jnp.maximum(k16 - 2, BF16_KEY_MIN)))
    hi = _key32(_pattern_of_key16(jnp.minimum(k16 + 2, BF16_KEY_MAX)))

    def count32(cf):
        def tile_count(c):
            hit = jnp.where(sc_ref[c] >= cf, 1.0, 0.0)
            return jnp.sum(hit.reshape(cw // 8, 8, tq), axis=0)

        cnt = lax.fori_loop(0, n_x, lambda c, a: a + tile_count(c), jnp.zeros((8, tq), F32))
        return jnp.sum(cnt + tile_count(nt), axis=0, keepdims=True)

    short = (qi * tq + lax.broadcasted_iota(I32, (1, tq), 1) + 1 + N_META) <= topk

    def unsettled(cnt_t):
        return (jnp.max(jnp.where(short | (cnt_t == kf), 0.0, 1.0)) > 0.5).astype(I32)

    def stage2_cond(st):
        it, _, _, go = st
        return jnp.logical_and(it < STAGE2_BITS, go > 0)

    def stage2(st):
        it, t, cnt_t, _ = st
        cand = t + lax.shift_left(jnp.int32(1), STAGE2_BITS - 1 - it)
        cnt = count32(lax.bitcast_convert_type(jnp.where(cand < 0, cand ^ 0x7FFFFFFF, cand), F32))
        ok = jnp.logical_and(cand < hi, cnt >= kf)
        cnt_t = jnp.where(ok, cnt, cnt_t)
        return it + 1, jnp.where(ok, cand, t), cnt_t, unsettled(cnt_t)

    _, t, _, _ = lax.while_loop(stage2_cond, stage2,
                                (jnp.int32(0), lo, jnp.full((1, tq), -1.0, F32), jnp.int32(1)))
    thr = lax.bitcast_convert_type(jnp.where(t < 0, t ^ 0x7FFFFFFF, t), F32)
    thr = jnp.where(short, jnp.finfo(F32).min, thr)

    for c in range(nt):
        @pl.when(c < n_x)
        def _(c=c):
            bias_ref[0, c * cw:(c + 1) * cw, :] = jnp.where(sc_ref[c] >= thr, 0.0, NEG_BIAS).astype(BF16)

        @pl.when(c >= n_x)
        def _(c=c):
            bias_ref[0, c * cw:(c + 1) * cw, :] = jnp.full((cw, tq), NEG_BIAS, BF16)

    bias_ref[0, seq:seq + LANES, :] = jnp.where(sc_ref[nt][0:LANES] >= thr, 0.0, NEG_BIAS).astype(BF16)


def _idx(p16, p32, ik2_meta, batch, seq, topk):
    nq = seq // IDX_TQ
    nt = seq // IDX_CW
    return pl.pallas_call(
        functools.partial(_idx_kernel, seq=seq, topk=topk),
        out_shape=jax.ShapeDtypeStruct((batch, seq + LANES, seq), BF16),
        grid=(batch, nq),
        in_specs=[
            pl.BlockSpec((4, IDX_TQ, SLAB), lambda b, q: (P16_IQ, b * nq + q, 0)),
            pl.BlockSpec((1, IDX_TQ, SLAB), lambda b, q: (P32_SMALL_SLAB, b * nq + q, 0)),
            pl.BlockSpec((1, seq, SLAB), lambda b, q: (P16_IK2_SLAB, b, 0)),
            pl.BlockSpec((IDX_CW, SLAB), lambda b, q: (0, 0)),
        ],
        out_specs=pl.BlockSpec((1, seq + LANES, IDX_TQ), lambda b, q: (b, 0, q)),
        scratch_shapes=[
            pltpu.VMEM((nt + 1, IDX_CW, IDX_TQ), F32),
            pltpu.VMEM((nt + 1, IDX_CW, IDX_TQ), BF16),
            pltpu.VMEM((LANES, IDX_TQ), F32),
        ],
        compiler_params=_params(("parallel", "arbitrary")),
        name="idx",
    )(p16, p32, p16, ik2_meta)


def _attn_kernel(qt_ref, kt_ref, q_ref, k_ref, vt_ref, b_ref, bm_ref, km_ref, vtm_ref, o_ref,
                 m_ref, l_ref, acc_ref, *, tq, tk):
    p = pl.program_id(1)
    qi, ki = qt_ref[p], kt_ref[p]
    scale2 = DSA_HEAD_DIM ** -0.5 * float(np.log2(np.e))

    def head(ref, h):
        return ref[h // 2][:, (h % 2) * LANES:(h % 2 + 1) * LANES]

    def accumulate(kref, vtref, bias):
        def scores(h):
            return _dot_nt(head(kref, h), head(q_ref, h)) * scale2 + bias

        ahead = 2
        pending = [scores(h) for h in range(ahead)]
        for h in range(DSA_HEADS):
            s = pending.pop(0)
            if h + ahead < DSA_HEADS:
                pending.append(scores(h + ahead))
            m_prev = m_ref[h]
            m_new = jnp.maximum(m_prev, jnp.max(s, axis=0, keepdims=True))
            alpha = jnp.exp2(m_prev - m_new)
            pr = jnp.exp2(s - m_new)
            l_ref[h] = alpha * l_ref[h] + jnp.sum(pr, axis=0, keepdims=True)
            vt = vtref[h * DSA_HEAD_DIM:(h + 1) * DSA_HEAD_DIM, :]
            acc_ref[h] = alpha * acc_ref[h] + _dot(vt, pr.astype(BF16))
            m_ref[h] = m_new

    @pl.when(ki == 0)
    def _():
        m_ref[...] = jnp.full(m_ref.shape, -jnp.inf, F32)
        l_ref[...] = jnp.zeros_like(l_ref)
        acc_ref[...] = jnp.zeros_like(acc_ref)
        accumulate(km_ref, vtm_ref, bm_ref[0].astype(F32))

    accumulate(k_ref, vt_ref, b_ref[0].astype(F32))

    @pl.when(ki == (qi * tq + tq - 1) // tk)
    def _():
        for h in range(DSA_HEADS):
            hs = slice(h * DSA_HEAD_DIM, (h + 1) * DSA_HEAD_DIM)
            o_ref[:, hs] = (acc_ref[h] / l_ref[h]).T.astype(o_ref.dtype)


def _attn(p16, vt, bias, p16_meta, vt_meta, batch, seq, tq, tk):
    nq, nk = seq // tq, seq // tk
    pairs = [(q, k) for q in range(nq) for k in range((q * tq + tq - 1) // tk + 1)]
    q_tab = jnp.asarray([q for q, _ in pairs], I32)
    k_tab = jnp.asarray([k for _, k in pairs], I32)
    grid_spec = pltpu.PrefetchScalarGridSpec(
        num_scalar_prefetch=2,
        grid=(batch, len(pairs)),
        in_specs=[
            pl.BlockSpec((4, tq, SLAB), lambda b, p, qt, kt: (P16_DQ, b * nq + qt[p], 0)),
            pl.BlockSpec((4, tk, SLAB), lambda b, p, qt, kt: (P16_DK, b * nk + kt[p], 0)),
            pl.BlockSpec((DSA_WIDTH, tk), lambda b, p, qt, kt: (0, b * nk + kt[p])),
            pl.BlockSpec((1, tk, tq), lambda b, p, qt, kt: (b, kt[p], qt[p])),
            pl.BlockSpec((1, LANES, tq), lambda b, p, qt, kt: (b, seq // LANES, qt[p])),
            pl.BlockSpec((4, LANES, SLAB), lambda b, p, qt, kt: (P16_DK, 0, 0)),
            pl.BlockSpec((DSA_WIDTH, LANES), lambda b, p, qt, kt: (0, 0)),
        ],
        out_specs=pl.BlockSpec((tq, DSA_WIDTH), lambda b, p, qt, kt: (b * nq + qt[p], 0)),
        scratch_shapes=[
            pltpu.VMEM((DSA_HEADS, 1, tq), F32),
            pltpu.VMEM((DSA_HEADS, 1, tq), F32),
            pltpu.VMEM((DSA_HEADS, DSA_HEAD_DIM, tq), F32),
        ],
    )
    return pl.pallas_call(
        functools.partial(_attn_kernel, tq=tq, tk=tk),
        out_shape=jax.ShapeDtypeStruct((batch * seq, DSA_WIDTH), BF16),
        grid_spec=grid_spec,
        compiler_params=_params(("parallel", "arbitrary")),
        name="attn",
    )(q_tab, k_tab, p16, p16, vt, bias, bias, p16_meta, vt_meta)


def _log_sigmoid(x):
    return jnp.minimum(x, 0.0) - jnp.log1p(jnp.exp(-jnp.abs(x)))


def _gla_kernel(gq_ref, gk_ref, go_ref, sm_ref, gv_ref, mk_ref, mv_ref, msm_ref, wgate_ref, bgate_ref,
                ng_ref, o_ref, st_ref, bc_ref, k_ref):
    c = GLA_CHUNK

    def log_decay(small):
        return _log_sigmoid(_dot_f32(small, wgate_ref[...]) + bgate_ref[...]) / GLA_TAU

    def lower_ones(n):
        r = lax.broadcasted_iota(I32, (n, n), 0)
        cc = lax.broadcasted_iota(I32, (n, n), 1)
        return r >= cc

    @pl.when(pl.program_id(1) == 0)
    def _():
        bcm = _dot_exact_lhs01(jnp.where(lower_ones(N_META), 1.0, 0.0).astype(BF16),
                               log_decay(msm_ref[0]))
        for h in range(GLA_HEADS):
            ks = slice(h * GLA_DK, (h + 1) * GLA_DK)
            kd = mk_ref[h // 2][:, (h % 2) * GLA_DK:(h % 2 + 1) * GLA_DK] * jnp.exp(
                bcm[N_META - 1:N_META, ks] - bcm[:, ks])
            st_ref[h] = _dot_tn(mv_ref[h], kd.astype(BF16))

    causal = lower_ones(c)
    bc_ref[...] = _dot_exact_lhs01(jnp.where(causal, 1.0, 0.0).astype(BF16), log_decay(sm_ref[0]))
    lane = lax.broadcasted_iota(I32, (c, c), 1)

    for h in range(GLA_HEADS):
        ks = slice(h * GLA_DK, (h + 1) * GLA_DK)
        hsl = slice((h % 2) * GLA_DK, (h % 2 + 1) * GLA_DK)
        q = gq_ref[h // 2][:, hsl] * (GLA_DK ** -0.5)
        k = gk_ref[h // 2][:, hsl]
        v = gv_ref[h]
        bc = bc_ref[:, ks]
        k_ref[...] = k

        attn = jnp.zeros((c, c), F32)
        for j in range(c):
            r0 = (j // 8) * 8
            e = jnp.exp(jnp.minimum(bc[r0:] - bc_ref[j:j + 1, ks], 0.0))
            colv = jnp.sum(q[r0:] * k_ref[j:j + 1, :] * e, axis=1, keepdims=True)
            if r0:
                colv = jnp.concatenate([jnp.zeros((r0, 1), F32), colv], axis=0)
            attn = jnp.where(lane == j, colv, attn)
        attn = jnp.where(causal, attn, 0.0)

        state = st_ref[h]
        b_last = bc[c - 1:c, :]
        o = _dot_nt((q * jnp.exp(bc)).astype(BF16), state.astype(BF16))
        o = o + _dot(attn.astype(BF16), v)
        kd = k * jnp.exp(b_last - bc)
        st_ref[h] = state * jnp.exp(b_last) + _dot_tn(v, kd.astype(BF16))

        g = go_ref[h]
        y = _rms(o, ng_ref[...]) * (g * jax.nn.sigmoid(g))
        o_ref[:, h * GLA_DV:(h + 1) * GLA_DV] = y.astype(o_ref.dtype)


def _gla(p32, p16, p32_meta, p16_meta, w_gate_pad, b_gate, norm_g, batch, seq):
    c = GLA_CHUNK
    ns = seq // c
    row = lambda b, i: b * ns + i
    return pl.pallas_call(
        _gla_kernel,
        out_shape=jax.ShapeDtypeStruct((batch * seq, GLA_WIDTH), BF16),
        grid=(batch, ns),
        in_specs=[
            pl.BlockSpec((2, c, SLAB), lambda b, i: (P32_GQ_SLAB // 2, row(b, i), 0)),
            pl.BlockSpec((2, c, SLAB), lambda b, i: (P32_GK_SLAB // 2, row(b, i), 0)),
            pl.BlockSpec((4, c, SLAB), lambda b, i: (P32_GOUT_SLAB // 4, row(b, i), 0)),
            pl.BlockSpec((1, c, SLAB), lambda b, i: (P32_SMALL_SLAB, row(b, i), 0)),
            pl.BlockSpec((4, c, SLAB), lambda b, i: (P16_GV, row(b, i), 0)),
            pl.BlockSpec((2, N_META, SLAB), lambda b, i: (P32_GK_SLAB // 2, 0, 0)),
            pl.BlockSpec((4, N_META, SLAB), lambda b, i: (P16_GV, 0, 0)),
            pl.BlockSpec((1, N_META, SLAB), lambda b, i: (P32_SMALL_SLAB, 0, 0)),
            pl.BlockSpec((SLAB, GLA_KEY_WIDTH), lambda b, i: (0, 0)),
            pl.BlockSpec((1, GLA_KEY_WIDTH), lambda b, i: (0, 0)),
            pl.BlockSpec((1, GLA_DV), lambda b, i: (0, 0)),
        ],
        out_specs=pl.BlockSpec((c, GLA_WIDTH), lambda b, i: (row(b, i), 0)),
        scratch_shapes=[
            pltpu.VMEM((GLA_HEADS, GLA_DV, GLA_DK), F32),
            pltpu.VMEM((c, GLA_KEY_WIDTH), F32),
            pltpu.VMEM((c, GLA_DK), F32),
        ],
        compiler_params=_params(("parallel", "arbitrary")),
        name="gla",
    )(p32, p32, p32, p32, p16, p32_meta, p16_meta, p32_meta, w_gate_pad, b_gate, norm_g)


def _out_kernel(h_ref, a_ref, g_ref, wa_ref, wg_ref, pg_ref, o_ref):
    m = _dot(a_ref[...], wa_ref[...]) + _dot(g_ref[...], wg_ref[...])
    o_ref[...] = h_ref[...] + _rms(m, pg_ref[...])


def _out(hs, o_dsa, o_gla, w_dsa, w_gla, post_g, tm):
    r, d = hs.shape
    return pl.pallas_call(
        _out_kernel,
        out_shape=jax.ShapeDtypeStruct((r, d), F32),
        grid=(r // tm,),
        in_specs=[
            pl.BlockSpec((tm, d), lambda i: (i, 0)),
            pl.BlockSpec((tm, DSA_WIDTH), lambda i: (i, 0)),
            pl.BlockSpec((tm, GLA_WIDTH), lambda i: (i, 0)),
            pl.BlockSpec((DSA_WIDTH, d), lambda i: (0, 0)),
            pl.BlockSpec((GLA_WIDTH, d), lambda i: (0, 0)),
            pl.BlockSpec((1, d), lambda i: (0, 0)),
        ],
        out_specs=pl.BlockSpec((tm, d), lambda i: (i, 0)),
        compiler_params=_params(("parallel",)),
        name="outproj",
    )(hs, o_dsa, o_gla, w_dsa, w_gla, post_g)


def _tile(n, pref):
    t = min(n, pref)
    while n % t:
        t //= 2
    return t


def _proj_weights(w_in):
    d = w_in.shape[0]
    offs = np.cumsum((0,) + IN_SPLITS)
    dq, dk, dv, iq, ik, iw, gq, gk, gv, glow, gout = (w_in[:, offs[i]:offs[i + 1]] for i in range(11))
    z = lambda n: jnp.zeros((d, n), w_in.dtype)
    ik2 = jnp.concatenate([ik, z(IDX_DIM), z(IDX_DIM), ik], axis=1)
    w16 = jnp.concatenate([dq, dk, dv, iq * (IDX_DIM ** -0.5), gv, ik2], axis=1)
    small = jnp.concatenate([iw * (IDX_HEADS ** -0.5), glow, z(SLAB - IDX_HEADS - GLA_GATE_RANK)], axis=1)
    w32 = jnp.concatenate([gq, gk, gout, small], axis=1)
    return w16.astype(BF16), w32.astype(BF16)


def kernel(x, meta_tokens, ffn1_pre_g, ffn1_w_gate, ffn1_w_up, ffn1_w_down, ffn1_post_g, mix_pre_g, w_in, w_gate_up, b_gate, gla_norm_g, w_out, mix_post_g, ffn2_pre_g, ffn2_w_gate, ffn2_w_up, ffn2_w_down, ffn2_post_g):
    batch, seq, d = x.shape
    depth = w_in.shape[0]
    assert depth == 1, "the meta rows skip the mixer, which is only valid for the last layer"
    d_ff = ffn1_w_gate.shape[-1]
    topk = min(TOPK_MAX, seq // 4)
    rows = batch * seq
    tm = _tile(rows, 512)
    tf = _tile(d_ff, 512)
    tq = _tile(seq, 256)
    tk = _tile(seq, 1024)

    hs = x.reshape(rows, d)
    hm = meta_tokens.astype(x.dtype)
    row2 = lambda v: v.reshape(1, -1)
    for l in range(depth):
        ffn1 = (row2(ffn1_pre_g[l]), ffn1_w_gate[l].astype(BF16), ffn1_w_up[l].astype(BF16),
                ffn1_w_down[l].astype(BF16), row2(ffn1_post_g[l]))
        hs = _ffn(hs, *ffn1, tm=tm, tf=tf)
        hm = _ffn(hm, *ffn1, tm=N_META, tf=tf)

        w16, w32 = _proj_weights(w_in[l])
        pre = row2(mix_pre_g[l])
        p16, p32 = _proj(hs, pre, w16, BF16, tm), _proj(hs, pre, w32, F32, tm)
        p16m, p32m = _proj(hm, pre, w16, BF16, N_META), _proj(hm, pre, w32, F32, N_META)

        pad_rows = lambda a, n: jnp.pad(a, ((0, 0), (0, n - a.shape[1]), (0, 0)))
        bias = _idx(p16, p32, pad_rows(p16m, IDX_CW)[P16_IK2_SLAB], batch, seq, topk)
        p16m_pad = pad_rows(p16m, LANES)
        v_t = lambda a: jnp.transpose(a[4 * P16_DV:4 * P16_DV + 4], (0, 2, 1)).reshape(DSA_WIDTH, -1)
        o_dsa = _attn(p16, v_t(p16), bias, p16m_pad, v_t(p16m_pad), batch, seq, tq, tk)

        w_gate_pad = jnp.zeros((SLAB, GLA_KEY_WIDTH), F32).at[
            IDX_HEADS:IDX_HEADS + GLA_GATE_RANK].set(w_gate_up[l])
        o_gla = _gla(p32, p16, p32m, p16m, w_gate_pad, row2(b_gate[l]), row2(gla_norm_g[l]), batch, seq)

        wo = w_out[l].astype(BF16)
        hs = _out(hs, o_dsa, o_gla, wo[:DSA_WIDTH], wo[DSA_WIDTH:], row2(mix_post_g[l]), tm)
        ffn2 = (row2(ffn2_pre_g[l]), ffn2_w_gate[l].astype(BF16), ffn2_w_up[l].astype(BF16),
                ffn2_w_down[l].astype(BF16), row2(ffn2_post_g[l]))
        hs = _ffn(hs, *ffn2, tm=tm, tf=tf)
    return hs.reshape(batch, seq, d)
```

```python
import functools

import numpy as np
import jax
import jax.numpy as jnp
from jax import lax
from jax.experimental import pallas as pl
from jax.experimental.pallas import tpu as pltpu

F32, BF16, I32 = jnp.float32, jnp.bfloat16, jnp.int32

N_META = 16
FFN_RES = 0.5
EPS = 1e-6
DSA_HEADS = 8
DSA_HEAD_DIM = 128
DSA_WIDTH = DSA_HEADS * DSA_HEAD_DIM
IDX_HEADS = 16
IDX_DIM = 64
TOPK_MAX = 256
GLA_HEADS = 4
GLA_DK = 128
GLA_DV = 256
GLA_KEY_WIDTH = GLA_HEADS * GLA_DK
GLA_WIDTH = GLA_HEADS * GLA_DV
GLA_GATE_RANK = 16
GLA_TAU = 16.0
GLA_CHUNK = 64
GLA_SUB = 16
GLA_ROWS = 128
IN_SPLITS = (DSA_WIDTH, DSA_WIDTH, DSA_WIDTH, IDX_HEADS * IDX_DIM, IDX_DIM, IDX_HEADS,
             GLA_KEY_WIDTH, GLA_KEY_WIDTH, GLA_WIDTH, GLA_GATE_RANK, GLA_WIDTH)

LANES = 128
SLAB = 256
VMEM_LIMIT_BYTES = 56 * 1024 * 1024

P16_DQ, P16_DK, P16_DV, P16_IQ, P16_GV, P16_IK2_SLAB, P16_SLABS = 0, 1, 2, 3, 4, 20, 21
P32_GQ_SLAB, P32_GK_SLAB, P32_GOUT_SLAB, P32_SMALL_SLAB, P32_SLABS = 0, 2, 4, 8, 9
SLABS_PER_STEP = 3

NEG_BIAS = -1e30
INT_MIN = -(2 ** 31)
KEY_NEG_INF = int(np.array(0xFF800000 ^ 0x7FFFFFFF, dtype=np.uint32).view(np.int32))


def _params(sem):
    return pltpu.CompilerParams(dimension_semantics=sem, vmem_limit_bytes=VMEM_LIMIT_BYTES)


def _dot(a, b):
    return jnp.dot(a, b, preferred_element_type=F32)


def _dot_nt(a, b):
    return lax.dot_general(a, b, (((1,), (1,)), ((), ())), preferred_element_type=F32)


def _dot_tn(a, b):
    return lax.dot_general(a, b, (((0,), (0,)), ((), ())), preferred_element_type=F32)


def _split3(x):
    hi = x.astype(BF16)
    r = x - hi.astype(F32)
    mid = r.astype(BF16)
    lo = (r - mid.astype(F32)).astype(BF16)
    return hi, mid, lo


def _dot_exact_lhs01(l01, x):
    hi, mid, lo = _split3(x)
    return _dot(l01, hi) + _dot(l01, mid) + _dot(l01, lo)


def _dot_f32(a, b):
    ah, am, al = _split3(a)
    bh, bm, bl = _split3(b)
    return (_dot(ah, bh) + (_dot(ah, bm) + _dot(am, bh))
            + (_dot(ah, bl) + _dot(am, bm) + _dot(al, bh)))


def _rms(x, g):
    return x * lax.rsqrt(jnp.mean(x * x, axis=-1, keepdims=True) + EPS) * g


def _ffn_kernel(x_ref, preg_ref, wg_ref, wu_ref, wd_ref, postg_ref, o_ref, xn_ref, acc_ref, *, nf):
    f = pl.program_id(1)

    @pl.when(f == 0)
    def _():
        xn_ref[...] = _rms(x_ref[...], preg_ref[...]).astype(BF16)
        acc_ref[...] = jnp.zeros_like(acc_ref)

    xn = xn_ref[...]
    g = _dot(xn, wg_ref[...])
    u = _dot(xn, wu_ref[...])
    a = (g * jax.nn.sigmoid(g)) * u
    acc_ref[...] += _dot(a.astype(BF16), wd_ref[...])

    @pl.when(f == nf - 1)
    def _():
        o_ref[...] = x_ref[...] + FFN_RES * _rms(acc_ref[...], postg_ref[...])


def _ffn(x, pre_g, wg, wu, wd, post_g, tm, tf):
    r, d = x.shape
    nf = wg.shape[1] // tf
    return pl.pallas_call(
        functools.partial(_ffn_kernel, nf=nf),
        out_shape=jax.ShapeDtypeStruct((r, d), F32),
        grid=(r // tm, nf),
        in_specs=[
            pl.BlockSpec((tm, d), lambda i, f: (i, 0)),
            pl.BlockSpec((1, d), lambda i, f: (0, 0)),
            pl.BlockSpec((d, tf), lambda i, f: (0, f)),
            pl.BlockSpec((d, tf), lambda i, f: (0, f)),
            pl.BlockSpec((tf, d), lambda i, f: (f, 0)),
            pl.BlockSpec((1, d), lambda i, f: (0, 0)),
        ],
        out_specs=pl.BlockSpec((tm, d), lambda i, f: (i, 0)),
        scratch_shapes=[pltpu.VMEM((tm, d), BF16), pltpu.VMEM((tm, d), F32)],
        compiler_params=_params(("parallel", "arbitrary")),
        name="ffn",
    )(x, pre_g, wg, wu, wd, post_g)


def _proj_kernel(x_ref, g_ref, w_ref, o_ref, xn_ref):
    @pl.when(pl.program_id(1) == 0)
    def _():
        xn_ref[...] = _rms(x_ref[...], g_ref[...]).astype(BF16)

    res = _dot(xn_ref[...], w_ref[...])
    for s in range(SLABS_PER_STEP):
        o_ref[s] = res[:, s * SLAB:(s + 1) * SLAB].astype(o_ref.dtype)


def _proj(x, g, w, out_dtype, tm):
    r, d = x.shape
    n_slabs = w.shape[1] // SLAB
    tn = SLABS_PER_STEP * SLAB
    return pl.pallas_call(
        _proj_kernel,
        out_shape=jax.ShapeDtypeStruct((n_slabs, r, SLAB), out_dtype),
        grid=(r // tm, n_slabs // SLABS_PER_STEP),
        in_specs=[
            pl.BlockSpec((tm, d), lambda i, j: (i, 0)),
            pl.BlockSpec((1, d), lambda i, j: (0, 0)),
            pl.BlockSpec((d, tn), lambda i, j: (0, j)),
        ],
        out_specs=pl.BlockSpec((SLABS_PER_STEP, tm, SLAB), lambda i, j: (j, i, 0)),
        scratch_shapes=[pltpu.VMEM((tm, d), BF16)],
        compiler_params=_params(("parallel", "arbitrary")),
        name="proj",
    )(x, g, w)


IDX_TQ = 256
IDX_CW = 256
BF16_KEY_MIN, BF16_KEY_MAX = -(2 ** 15), 2 ** 15 - 1
STAGE2_BITS = 18


def _key32(pattern):
    return jnp.where(pattern < 0, pattern ^ 0x7FFFFFFF, pattern)


def _pattern_of_key16(k16):
    return lax.shift_left(jnp.where(k16 < 0, k16 ^ 0x7FFF, k16), 16)


def _idx_kernel(iq_ref, sm_ref, ik_ref, ikm_ref, bias_ref, sc_ref, sc16_ref, w_ref, *, seq, topk):
    qi = pl.program_id(1)
    tq, cw = IDX_TQ, IDX_CW
    nt = seq // cw
    n_x = qi + 1
    kf = float(topk)

    w_ref[...] = sm_ref[0][:, 0:LANES].T
    key_pos = lax.broadcasted_iota(I32, (cw, tq), 0)
    q_pos = qi * tq + lax.broadcasted_iota(I32, (cw, tq), 1)

    def score_tile(k2):
        lhs = jnp.concatenate([k2[:, 0:LANES], k2[:, LANES:2 * LANES]], axis=0)
        acc = jnp.zeros((cw, tq), F32)
        for p in range(IDX_HEADS // 2):
            rhs = iq_ref[p // 2][:, (p % 2) * LANES:(p % 2 + 1) * LANES]
            l2 = _dot_nt(lhs, rhs)
            acc = acc + jnp.maximum(l2[0:cw], 0.0) * w_ref[2 * p:2 * p + 1, :]
            acc = acc + jnp.maximum(l2[cw:2 * cw], 0.0) * w_ref[2 * p + 1:2 * p + 2, :]
        return acc

    def x_tile(c, carry):
        k2 = ik_ref[0, pl.ds(pl.multiple_of(c * cw, cw), cw), :]
        acc = jnp.where(c * cw + key_pos <= q_pos, score_tile(k2), -jnp.inf)
        sc_ref[c + 1] = acc
        sc16_ref[c + 1] = acc.astype(BF16)
        return carry

    lax.fori_loop(0, n_x, x_tile, 0)
    acc_m = jnp.where(key_pos < N_META, score_tile(ikm_ref[...]), -jnp.inf)
    sc_ref[0] = acc_m
    sc16_ref[0] = acc_m.astype(BF16)
    sc_ref[n_x + 1] = jnp.full((cw, tq), -jnp.inf, F32)
    sc16_ref[n_x + 1] = jnp.full((cw, tq), -jnp.inf, BF16)
    n_pairs = (n_x + 2) // 2

    one16, zero16 = jnp.ones((cw, tq), BF16), jnp.zeros((cw, tq), BF16)

    def count16(c16):
        def tile_count(c):
            hit = jnp.where(sc16_ref[c] >= c16, one16, zero16).reshape(cw // 16, 16, tq)
            parts = [hit[i] for i in range(cw // 16)]
            while len(parts) > 1:
                parts = [parts[i] + parts[i + 1] for i in range(0, len(parts), 2)]
            return parts[0].astype(F32)

        cnt = lax.fori_loop(0, n_pairs, lambda p, a: a + (tile_count(2 * p) + tile_count(2 * p + 1)),
                            jnp.zeros((16, tq), F32))
        return jnp.sum(cnt, axis=0, keepdims=True)

    def stage1(it, k16):
        cand = k16 + lax.shift_left(jnp.int32(1), 15 - it)
        c32 = lax.bitcast_convert_type(_pattern_of_key16(cand), F32)
        c16 = jnp.broadcast_to(c32, (cw, tq)).astype(BF16)
        return jnp.where(count16(c16) >= kf, cand, k16)

    k16 = lax.fori_loop(0, 16, stage1, jnp.full((1, tq), BF16_KEY_MIN, I32))

    lo = _key32(_pattern_of_key16(jnp.maximum(k16 - 2, BF16_KEY_MIN)))
    hi = _key32(_pattern_of_key16(jnp.minimum(k16 + 2, BF16_KEY_MAX)))

    def count32(cf):
        def tile_count(c):
            hit = jnp.where(sc_ref[c] >= cf, 1.0, 0.0)
            return jnp.sum(hit.reshape(cw // 8, 8, tq), axis=0)

        cnt = lax.fori_loop(0, n_pairs, lambda p, a: a + (tile_count(2 * p) + tile_count(2 * p + 1)),
                            jnp.zeros((8, tq), F32))
        return jnp.sum(cnt, axis=0, keepdims=True)

    short = (qi * tq + lax.broadcasted_iota(I32, (1, tq), 1) + 1 + N_META) <= topk

    def unsettled(cnt_t):
        return (jnp.max(jnp.where(short | (cnt_t == kf), 0.0, 1.0)) > 0.5).astype(I32)

    def stage2_cond(st):
        it, _, _, go = st
        return jnp.logical_and(it < STAGE2_BITS, go > 0)

    def stage2(st):
        it, t, cnt_t, _ = st
        cand = t + lax.shift_left(jnp.int32(1), STAGE2_BITS - 1 - it)
        cnt = count32(lax.bitcast_convert_type(jnp.where(cand < 0, cand ^ 0x7FFFFFFF, cand), F32))
        ok = jnp.logical_and(cand < hi, cnt >= kf)
        cnt_t = jnp.where(ok, cnt, cnt_t)
        return it + 1, jnp.where(ok, cand, t), cnt_t, unsettled(cnt_t)

    _, t, _, _ = lax.while_loop(stage2_cond, stage2,
                                (jnp.int32(0), lo, jnp.full((1, tq), -1.0, F32), jnp.int32(1)))
    thr = lax.bitcast_convert_type(jnp.where(t < 0, t ^ 0x7FFFFFFF, t), F32)
    thr = jnp.where(short, jnp.finfo(F32).min, thr)

    for c in range(nt):
        @pl.when(c < n_x)
        def _(c=c):
            bias_ref[0, c * cw:(c + 1) * cw, :] = jnp.where(sc_ref[c + 1] >= thr, 0.0, NEG_BIAS).astype(BF16)

        @pl.when(c >= n_x)
        def _(c=c):
            bias_ref[0, c * cw:(c + 1) * cw, :] = jnp.full((cw, tq), NEG_BIAS, BF16)

    bias_ref[0, seq:seq + LANES, :] = jnp.where(sc_ref[0][0:LANES] >= thr, 0.0, NEG_BIAS).astype(BF16)


def _idx(p16, p32, ik2_meta, batch, seq, topk):
    nq = seq // IDX_TQ
    nt = seq // IDX_CW
    return pl.pallas_call(
        functools.partial(_idx_kernel, seq=seq, topk=topk),
        out_shape=jax.ShapeDtypeStruct((batch, seq + LANES, seq), BF16),
        grid=(batch, nq),
        in_specs=[
            pl.BlockSpec((4, IDX_TQ, SLAB), lambda b, q: (P16_IQ, b * nq + q, 0)),
            pl.BlockSpec((1, IDX_TQ, SLAB), lambda b, q: (P32_SMALL_SLAB, b * nq + q, 0)),
            pl.BlockSpec((1, seq, SLAB), lambda b, q: (P16_IK2_SLAB, b, 0)),
            pl.BlockSpec((IDX_CW, SLAB), lambda b, q: (0, 0)),
        ],
        out_specs=pl.BlockSpec((1, seq + LANES, IDX_TQ), lambda b, q: (b, 0, q)),
        scratch_shapes=[
            pltpu.VMEM((nt + 2, IDX_CW, IDX_TQ), F32),
            pltpu.VMEM((nt + 2, IDX_CW, IDX_TQ), BF16),
            pltpu.VMEM((LANES, IDX_TQ), F32),
        ],
        compiler_params=_params(("parallel", "arbitrary")),
        name="idx",
    )(p16, p32, p16, ik2_meta)


def _attn_kernel(qt_ref, kt_ref, q_ref, k_ref, vt_ref, b_ref, bm_ref, km_ref, vtm_ref, o_ref,
                 m_ref, l_ref, acc_ref, *, tq, tk):
    p = pl.program_id(1)
    qi, ki = qt_ref[p], kt_ref[p]
    scale2 = DSA_HEAD_DIM ** -0.5 * float(np.log2(np.e))

    def head(ref, h):
        return ref[h // 2][:, (h % 2) * LANES:(h % 2 + 1) * LANES]

    def accumulate(kref, vtref, bias):
        def scores(h):
            return _dot_nt(head(kref, h), head(q_ref, h)) * scale2 + bias

        ahead = 2
        pending = [scores(h) for h in range(ahead)]
        for h in range(DSA_HEADS):
            s = pending.pop(0)
            if h + ahead < DSA_HEADS:
                pending.append(scores(h + ahead))
            m_prev = m_ref[h]
            m_new = jnp.maximum(m_prev, jnp.max(s, axis=0, keepdims=True))
            alpha = jnp.exp2(m_prev - m_new)
            pr = jnp.exp2(s - m_new)
            l_ref[h] = alpha * l_ref[h] + jnp.sum(pr, axis=0, keepdims=True)
            vt = vtref[h * DSA_HEAD_DIM:(h + 1) * DSA_HEAD_DIM, :]
            acc_ref[h] = alpha * acc_ref[h] + _dot(vt, pr.astype(BF16))
            m_ref[h] = m_new

    @pl.when(ki == 0)
    def _():
        m_ref[...] = jnp.full(m_ref.shape, -jnp.inf, F32)
        l_ref[...] = jnp.zeros_like(l_ref)
        acc_ref[...] = jnp.zeros_like(acc_ref)
        accumulate(km_ref, vtm_ref, bm_ref[0].astype(F32))

    accumulate(k_ref, vt_ref, b_ref[0].astype(F32))

    @pl.when(ki == (qi * tq + tq - 1) // tk)
    def _():
        for h in range(DSA_HEADS):
            hs = slice(h * DSA_HEAD_DIM, (h + 1) * DSA_HEAD_DIM)
            o_ref[:, hs] = (acc_ref[h] / l_ref[h]).T.astype(o_ref.dtype)


def _attn(p16, vt, bias, p16_meta, vt_meta, batch, seq, tq, tk):
    nq, nk = seq // tq, seq // tk
    pairs = [(q, k) for q in range(nq) for k in range((q * tq + tq - 1) // tk + 1)]
    q_tab = jnp.asarray([q for q, _ in pairs], I32)
    k_tab = jnp.asarray([k for _, k in pairs], I32)
    grid_spec = pltpu.PrefetchScalarGridSpec(
        num_scalar_prefetch=2,
        grid=(batch, len(pairs)),
        in_specs=[
            pl.BlockSpec((4, tq, SLAB), lambda b, p, qt, kt: (P16_DQ, b * nq + qt[p], 0)),
            pl.BlockSpec((4, tk, SLAB), lambda b, p, qt, kt: (P16_DK, b * nk + kt[p], 0)),
            pl.BlockSpec((DSA_WIDTH, tk), lambda b, p, qt, kt: (0, b * nk + kt[p])),
            pl.BlockSpec((1, tk, tq), lambda b, p, qt, kt: (b, kt[p], qt[p])),
            pl.BlockSpec((1, LANES, tq), lambda b, p, qt, kt: (b, seq // LANES, qt[p])),
            pl.BlockSpec((4, LANES, SLAB), lambda b, p, qt, kt: (P16_DK, 0, 0)),
            pl.BlockSpec((DSA_WIDTH, LANES), lambda b, p, qt, kt: (0, 0)),
        ],
        out_specs=pl.BlockSpec((tq, DSA_WIDTH), lambda b, p, qt, kt: (b * nq + qt[p], 0)),
        scratch_shapes=[
            pltpu.VMEM((DSA_HEADS, 1, tq), F32),
            pltpu.VMEM((DSA_HEADS, 1, tq), F32),
            pltpu.VMEM((DSA_HEADS, DSA_HEAD_DIM, tq), F32),
        ],
    )
    return pl.pallas_call(
        functools.partial(_attn_kernel, tq=tq, tk=tk),
        out_shape=jax.ShapeDtypeStruct((batch * seq, DSA_WIDTH), BF16),
        grid_spec=grid_spec,
        compiler_params=_params(("parallel", "arbitrary")),
        name="attn",
    )(q_tab, k_tab, p16, p16, vt, bias, bias, p16_meta, vt_meta)


def _log_sigmoid(x):
    return jnp.minimum(x, 0.0) - jnp.log1p(jnp.exp(-jnp.abs(x)))


def _gla_kernel(gq_ref, gk_ref, go_ref, sm_ref, gv_ref, mk_ref, mv_ref, msm_ref, wgate_ref, bgate_ref,
                ng_ref, o_ref, st_ref, bc_ref):
    c = GLA_CHUNK

    def log_decay(small):
        return _log_sigmoid(_dot_f32(small, wgate_ref[...]) + bgate_ref[...]) / GLA_TAU

    def lower_ones(n):
        r = lax.broadcasted_iota(I32, (n, n), 0)
        cc = lax.broadcasted_iota(I32, (n, n), 1)
        return r >= cc

    @pl.when(pl.program_id(1) == 0)
    def _():
        bcm = _dot_exact_lhs01(jnp.where(lower_ones(N_META), 1.0, 0.0).astype(BF16),
                               log_decay(msm_ref[0]))
        for h in range(GLA_HEADS):
            ks = slice(h * GLA_DK, (h + 1) * GLA_DK)
            kd = mk_ref[h // 2][:, (h % 2) * GLA_DK:(h % 2 + 1) * GLA_DK] * jnp.exp(
                bcm[N_META - 1:N_META, ks] - bcm[:, ks])
            st_ref[h] = _dot_tn(mv_ref[h], kd.astype(BF16))

    rows = bc_ref.shape[0]
    r_i = lax.broadcasted_iota(I32, (rows, rows), 0)
    c_i = lax.broadcasted_iota(I32, (rows, rows), 1)
    chunk_tri = jnp.logical_and(r_i >= c_i, r_i // c == c_i // c)
    bc_ref[...] = _dot_exact_lhs01(jnp.where(chunk_tri, 1.0, 0.0).astype(BF16), log_decay(sm_ref[0]))
    causal = lower_ones(c)
    sub = GLA_SUB
    lane = lax.broadcasted_iota(I32, (sub, c), 1)

    for ch in range(rows // c):
        base = ch * c
        for h in range(GLA_HEADS):
            ks = slice(h * GLA_DK, (h + 1) * GLA_DK)
            hsl = slice((h % 2) * GLA_DK, (h % 2 + 1) * GLA_DK)
            q = gq_ref[h // 2][base:base + c, hsl] * (GLA_DK ** -0.5)
            k = gk_ref[h // 2][base:base + c, hsl]
            v = gv_ref[h][base:base + c]
            bc = bc_ref[base:base + c, ks]

            blocks = []
            for s0 in range(0, c, sub):
                q_s, bc_s = q[s0:s0 + sub], bc[s0:s0 + sub]
                diag = jnp.zeros((sub, c), F32)
                for j in range(s0, s0 + sub):
                    e = jnp.exp(jnp.minimum(bc_s - bc_ref[base + j:base + j + 1, ks], 0.0))
                    k_j = gk_ref[h // 2][base + j:base + j + 1, hsl]
                    diag = jnp.where(lane == j, jnp.sum(q_s * k_j * e, axis=1, keepdims=True), diag)
                if s0:
                    b_edge = bc_ref[base + s0 - 1:base + s0, ks]
                    q_t = q_s * jnp.exp(bc_s - b_edge)
                    k_t = k * jnp.exp(jnp.minimum(b_edge - bc, 0.0))
                    diag = jnp.where(lane < s0, _dot_nt(q_t.astype(BF16), k_t.astype(BF16)), diag)
                blocks.append(diag)
            attn = jnp.where(causal, jnp.concatenate(blocks, axis=0), 0.0)

            state = st_ref[h]
            b_last = bc[c - 1:c, :]
            o = _dot_nt((q * jnp.exp(bc)).astype(BF16), state.astype(BF16))
            o = o + _dot(attn.astype(BF16), v)
            kd = k * jnp.exp(b_last - bc)
            st_ref[h] = state * jnp.exp(b_last) + _dot_tn(v, kd.astype(BF16))

            g = go_ref[h][base:base + c]
            y = _rms(o, ng_ref[...]) * (g * jax.nn.sigmoid(g))
            o_ref[base:base + c, h * GLA_DV:(h + 1) * GLA_DV] = y.astype(o_ref.dtype)


def _gla(p32, p16, p32_meta, p16_meta, w_gate_pad, b_gate, norm_g, batch, seq):
    c = _tile(seq, GLA_ROWS)
    ns = seq // c
    row = lambda b, i: b * ns + i
    return pl.pallas_call(
        _gla_kernel,
        out_shape=jax.ShapeDtypeStruct((batch * seq, GLA_WIDTH), BF16),
        grid=(batch, ns),
        in_specs=[
            pl.BlockSpec((2, c, SLAB), lambda b, i: (P32_GQ_SLAB // 2, row(b, i), 0)),
            pl.BlockSpec((2, c, SLAB), lambda b, i: (P32_GK_SLAB // 2, row(b, i), 0)),
            pl.BlockSpec((4, c, SLAB), lambda b, i: (P32_GOUT_SLAB // 4, row(b, i), 0)),
            pl.BlockSpec((1, c, SLAB), lambda b, i: (P32_SMALL_SLAB, row(b, i), 0)),
            pl.BlockSpec((4, c, SLAB), lambda b, i: (P16_GV, row(b, i), 0)),
            pl.BlockSpec((2, N_META, SLAB), lambda b, i: (P32_GK_SLAB // 2, 0, 0)),
            pl.BlockSpec((4, N_META, SLAB), lambda b, i: (P16_GV, 0, 0)),
            pl.BlockSpec((1, N_META, SLAB), lambda b, i: (P32_SMALL_SLAB, 0, 0)),
            pl.BlockSpec((SLAB, GLA_KEY_WIDTH), lambda b, i: (0, 0)),
            pl.BlockSpec((1, GLA_KEY_WIDTH), lambda b, i: (0, 0)),
            pl.BlockSpec((1, GLA_DV), lambda b, i: (0, 0)),
        ],
        out_specs=pl.BlockSpec((c, GLA_WIDTH), lambda b, i: (row(b, i), 0)),
        scratch_shapes=[
            pltpu.VMEM((GLA_HEADS, GLA_DV, GLA_DK), F32),
            pltpu.VMEM((c, GLA_KEY_WIDTH), F32),
        ],
        compiler_params=_params(("parallel", "arbitrary")),
        name="gla",
    )(p32, p32, p32, p32, p16, p32_meta, p16_meta, p32_meta, w_gate_pad, b_gate, norm_g)


def _out_kernel(h_ref, a_ref, g_ref, wa_ref, wg_ref, pg_ref, o_ref):
    m = _dot(a_ref[...], wa_ref[...]) + _dot(g_ref[...], wg_ref[...])
    o_ref[...] = h_ref[...] + _rms(m, pg_ref[...])


def _out(hs, o_dsa, o_gla, w_dsa, w_gla, post_g, tm):
    r, d = hs.shape
    return pl.pallas_call(
        _out_kernel,
        out_shape=jax.ShapeDtypeStruct((r, d), F32),
        grid=(r // tm,),
        in_specs=[
            pl.BlockSpec((tm, d), lambda i: (i, 0)),
            pl.BlockSpec((tm, DSA_WIDTH), lambda i: (i, 0)),
            pl.BlockSpec((tm, GLA_WIDTH), lambda i: (i, 0)),
            pl.BlockSpec((DSA_WIDTH, d), lambda i: (0, 0)),
            pl.BlockSpec((GLA_WIDTH, d), lambda i: (0, 0)),
            pl.BlockSpec((1, d), lambda i: (0, 0)),
        ],
        out_specs=pl.BlockSpec((tm, d), lambda i: (i, 0)),
        compiler_params=_params(("parallel",)),
        name="outproj",
    )(hs, o_dsa, o_gla, w_dsa, w_gla, post_g)


def _tile(n, pref):
    t = min(n, pref)
    while n % t:
        t //= 2
    return t


def _proj_weights(w_in):
    d = w_in.shape[0]
    offs = np.cumsum((0,) + IN_SPLITS)
    dq, dk, dv, iq, ik, iw, gq, gk, gv, glow, gout = (w_in[:, offs[i]:offs[i + 1]] for i in range(11))
    z = lambda n: jnp.zeros((d, n), w_in.dtype)
    ik2 = jnp.concatenate([ik, z(IDX_DIM), z(IDX_DIM), ik], axis=1)
    w16 = jnp.concatenate([dq, dk, dv, iq * (IDX_DIM ** -0.5), gv, ik2], axis=1)
    small = jnp.concatenate([iw * (IDX_HEADS ** -0.5), glow, z(SLAB - IDX_HEADS - GLA_GATE_RANK)], axis=1)
    w32 = jnp.concatenate([gq, gk, gout, small], axis=1)
    return w16.astype(BF16), w32.astype(BF16)


def kernel(x, meta_tokens, ffn1_pre_g, ffn1_w_gate, ffn1_w_up, ffn1_w_down, ffn1_post_g, mix_pre_g, w_in, w_gate_up, b_gate, gla_norm_g, w_out, mix_post_g, ffn2_pre_g, ffn2_w_gate, ffn2_w_up, ffn2_w_down, ffn2_post_g):
    batch, seq, d = x.shape
    depth = w_in.shape[0]
    assert depth == 1, "the meta rows skip the mixer, which is only valid for the last layer"
    d_ff = ffn1_w_gate.shape[-1]
    topk = min(TOPK_MAX, seq // 4)
    rows = batch * seq
    tm = _tile(rows, 512)
    tf = _tile(d_ff, 512)
    tq = _tile(seq, 256)
    tk = _tile(seq, 1024)

    hs = x.reshape(rows, d)
    hm = meta_tokens.astype(x.dtype)
    row2 = lambda v: v.reshape(1, -1)
    for l in range(depth):
        ffn1 = (row2(ffn1_pre_g[l]), ffn1_w_gate[l].astype(BF16), ffn1_w_up[l].astype(BF16),
                ffn1_w_down[l].astype(BF16), row2(ffn1_post_g[l]))
        hs = _ffn(hs, *ffn1, tm=tm, tf=tf)
        hm = _ffn(hm, *ffn1, tm=N_META, tf=tf)

        w16, w32 = _proj_weights(w_in[l])
        pre = row2(mix_pre_g[l])
        p16, p32 = _proj(hs, pre, w16, BF16, tm), _proj(hs, pre, w32, F32, tm)
        p16m, p32m = _proj(hm, pre, w16, BF16, N_META), _proj(hm, pre, w32, F32, N_META)

        pad_rows = lambda a, n: jnp.pad(a, ((0, 0), (0, n - a.shape[1]), (0, 0)))
        bias = _idx(p16, p32, pad_rows(p16m, IDX_CW)[P16_IK2_SLAB], batch, seq, topk)
        p16m_pad = pad_rows(p16m, LANES)
        v_t = lambda a: jnp.transpose(a[4 * P16_DV:4 * P16_DV + 4], (0, 2, 1)).reshape(DSA_WIDTH, -1)
        o_dsa = _attn(p16, v_t(p16), bias, p16m_pad, v_t(p16m_pad), batch, seq, tq, tk)

        w_gate_pad = jnp.zeros((SLAB, GLA_KEY_WIDTH), F32).at[
            IDX_HEADS:IDX_HEADS + GLA_GATE_RANK].set(w_gate_up[l])
        o_gla = _gla(p32, p16, p32m, p16m, w_gate_pad, row2(b_gate[l]), row2(gla_norm_g[l]), batch, seq)

        wo = w_out[l].astype(BF16)
        hs = _out(hs, o_dsa, o_gla, wo[:DSA_WIDTH], wo[DSA_WIDTH:], row2(mix_post_g[l]), tm)
        ffn2 = (row2(ffn2_pre_g[l]), ffn2_w_gate[l].astype(BF16), ffn2_w_up[l].astype(BF16),
                ffn2_w_down[l].astype(BF16), row2(ffn2_post_g[l]))
        hs = _ffn(hs, *ffn2, tm=tm, tf=tf)
    return hs.reshape(batch, seq, d)
```

```python
import functools

import numpy as np
import jax
import jax.numpy as jnp
from jax import lax
from jax.experimental import pallas as pl
from jax.experimental.pallas import tpu as pltpu

F32, BF16, I32 = jnp.float32, jnp.bfloat16, jnp.int32

N_META = 16
FFN_RES = 0.5
EPS = 1e-6
DSA_HEADS = 8
DSA_HEAD_DIM = 128
DSA_WIDTH = DSA_HEADS * DSA_HEAD_DIM
IDX_HEADS = 16
IDX_DIM = 64
TOPK_MAX = 256
GLA_HEADS = 4
GLA_DK = 128
GLA_DV = 256
GLA_KEY_WIDTH = GLA_HEADS * GLA_DK
GLA_WIDTH = GLA_HEADS * GLA_DV
GLA_GATE_RANK = 16
GLA_TAU = 16.0
GLA_CHUNK = 64
GLA_SUB = 16
GLA_ROWS = 128
IN_SPLITS = (DSA_WIDTH, DSA_WIDTH, DSA_WIDTH, IDX_HEADS * IDX_DIM, IDX_DIM, IDX_HEADS,
             GLA_KEY_WIDTH, GLA_KEY_WIDTH, GLA_WIDTH, GLA_GATE_RANK, GLA_WIDTH)

LANES = 128
SLAB = 256
VMEM_LIMIT_BYTES = 56 * 1024 * 1024

P16_DQ, P16_DK, P16_DV, P16_IQ, P16_GV, P16_IK2_SLAB, P16_SLABS = 0, 1, 2, 3, 4, 20, 21
P32_GQ_SLAB, P32_GK_SLAB, P32_GOUT_SLAB, P32_SMALL_SLAB, P32_SLABS = 0, 2, 4, 8, 9
P16_SLABS_PER_STEP, P32_SLABS_PER_STEP = 7, 3

NEG_BIAS = -1e30
INT_MIN = -(2 ** 31)
KEY_NEG_INF = int(np.array(0xFF800000 ^ 0x7FFFFFFF, dtype=np.uint32).view(np.int32))


def _params(sem, **flags):
    return pltpu.CompilerParams(dimension_semantics=sem, vmem_limit_bytes=VMEM_LIMIT_BYTES,
                                flags=flags or None)


def _dot(a, b):
    return jnp.dot(a, b, preferred_element_type=F32)


def _dot_nt(a, b):
    return lax.dot_general(a, b, (((1,), (1,)), ((), ())), preferred_element_type=F32)


def _dot_tn(a, b):
    return lax.dot_general(a, b, (((0,), (0,)), ((), ())), preferred_element_type=F32)


def _split3(x):
    hi = x.astype(BF16)
    r = x - hi.astype(F32)
    mid = r.astype(BF16)
    lo = (r - mid.astype(F32)).astype(BF16)
    return hi, mid, lo


def _dot_exact_lhs01(l01, x):
    hi, mid, lo = _split3(x)
    return _dot(l01, hi) + _dot(l01, mid) + _dot(l01, lo)


def _dot_f32(a, b):
    ah, am, al = _split3(a)
    bh, bm, bl = _split3(b)
    return (_dot(ah, bh) + (_dot(ah, bm) + _dot(am, bh))
            + (_dot(ah, bl) + _dot(am, bm) + _dot(al, bh)))


def _rms(x, g):
    return x * lax.rsqrt(jnp.mean(x * x, axis=-1, keepdims=True) + EPS) * g


def _ffn_kernel(x_ref, preg_ref, wg_ref, wu_ref, wd_ref, postg_ref, o_ref, xn_ref, *, nf):
    f = pl.program_id(1)

    @pl.when(f == 0)
    def _():
        xn_ref[...] = _rms(x_ref[...], preg_ref[...]).astype(BF16)
        o_ref[...] = jnp.zeros_like(o_ref)

    xn = xn_ref[...]
    g = _dot(xn, wg_ref[...])
    u = _dot(xn, wu_ref[...])
    a = (g * jax.nn.sigmoid(g)) * u
    o_ref[...] += _dot(a.astype(BF16), wd_ref[...])

    @pl.when(f == nf - 1)
    def _():
        o_ref[...] = x_ref[...] + FFN_RES * _rms(o_ref[...], postg_ref[...])


def _ffn(x, pre_g, wg, wu, wd, post_g, tm, tf):
    r, d = x.shape
    nf = wg.shape[1] // tf
    return pl.pallas_call(
        functools.partial(_ffn_kernel, nf=nf),
        out_shape=jax.ShapeDtypeStruct((r, d), F32),
        grid=(r // tm, nf),
        in_specs=[
            pl.BlockSpec((tm, d), lambda i, f: (i, 0)),
            pl.BlockSpec((1, d), lambda i, f: (0, 0)),
            pl.BlockSpec((d, tf), lambda i, f: (0, f)),
            pl.BlockSpec((d, tf), lambda i, f: (0, f)),
            pl.BlockSpec((tf, d), lambda i, f: (f, 0)),
            pl.BlockSpec((1, d), lambda i, f: (0, 0)),
        ],
        out_specs=pl.BlockSpec((tm, d), lambda i, f: (i, 0)),
        scratch_shapes=[pltpu.VMEM((tm, d), BF16)],
        compiler_params=_params(("parallel", "arbitrary")),
        name="ffn",
    )(x, pre_g, wg, wu, wd, post_g)


def _proj_kernel(x_ref, g_ref, w_ref, o_ref, xn_ref):
    @pl.when(pl.program_id(1) == 0)
    def _():
        xn_ref[...] = _rms(x_ref[...], g_ref[...]).astype(BF16)

    res = _dot(xn_ref[...], w_ref[...])
    for s in range(o_ref.shape[0]):
        o_ref[s] = res[:, s * SLAB:(s + 1) * SLAB].astype(o_ref.dtype)


def _proj(x, g, w, out_dtype, tm, slabs_per_step):
    r, d = x.shape
    n_slabs = w.shape[1] // SLAB
    tn = slabs_per_step * SLAB
    return pl.pallas_call(
        _proj_kernel,
        out_shape=jax.ShapeDtypeStruct((n_slabs, r, SLAB), out_dtype),
        grid=(r // tm, n_slabs // slabs_per_step),
        in_specs=[
            pl.BlockSpec((tm, d), lambda i, j: (i, 0)),
            pl.BlockSpec((1, d), lambda i, j: (0, 0)),
            pl.BlockSpec((d, tn), lambda i, j: (0, j)),
        ],
        out_specs=pl.BlockSpec((slabs_per_step, tm, SLAB), lambda i, j: (j, i, 0)),
        scratch_shapes=[pltpu.VMEM((tm, d), BF16)],
        compiler_params=_params(("parallel", "arbitrary")),
        name="proj",
    )(x, g, w)


IDX_TQ = 256
IDX_CW = 256
BF16_KEY_MIN, BF16_KEY_MAX = -(2 ** 15), 2 ** 15 - 1
STAGE2_BITS = 18


def _key32(pattern):
    return jnp.where(pattern < 0, pattern ^ 0x7FFFFFFF, pattern)


def _pattern_of_key16(k16):
    return lax.shift_left(jnp.where(k16 < 0, k16 ^ 0x7FFF, k16), 16)


def _idx_kernel(iq_ref, sm_ref, ik_ref, ikm_ref, bias_ref, sc_ref, sc16_ref, w_ref, *, seq, topk):
    qi = pl.program_id(1)
    tq, cw = IDX_TQ, IDX_CW
    nt = seq // cw
    n_x = qi + 1
    kf = float(topk)

    w_ref[...] = sm_ref[0][:, 0:LANES].T
    key_pos = lax.broadcasted_iota(I32, (cw, tq), 0)
    q_pos = qi * tq + lax.broadcasted_iota(I32, (cw, tq), 1)

    def score_tile(k2):
        lhs = jnp.concatenate([k2[:, 0:LANES], k2[:, LANES:2 * LANES]], axis=0)
        acc = jnp.zeros((cw, tq), F32)
        for p in range(IDX_HEADS // 2):
            rhs = iq_ref[p // 2][:, (p % 2) * LANES:(p % 2 + 1) * LANES]
            l2 = _dot_nt(lhs, rhs)
            acc = acc + jnp.maximum(l2[0:cw], 0.0) * w_ref[2 * p:2 * p + 1, :]
            acc = acc + jnp.maximum(l2[cw:2 * cw], 0.0) * w_ref[2 * p + 1:2 * p + 2, :]
        return acc

    def x_tile(c):
        k2 = ik_ref[0, pl.ds(pl.multiple_of(c * cw, cw), cw), :]
        acc = jnp.where(c * cw + key_pos <= q_pos, score_tile(k2), -jnp.inf)
        sc_ref[c + 1] = acc
        sc16_ref[c + 1] = acc.astype(BF16)

    def x_tile_pair(p, carry):
        x_tile(2 * p)
        x_tile(2 * p + 1)
        return carry

    lax.fori_loop(0, (n_x + 1) // 2, x_tile_pair, 0)
    acc_m = jnp.where(key_pos < N_META, score_tile(ikm_ref[...]), -jnp.inf)
    sc_ref[0] = acc_m
    sc16_ref[0] = acc_m.astype(BF16)
    sc_ref[n_x + 1] = jnp.full((cw, tq), -jnp.inf, F32)
    sc16_ref[n_x + 1] = jnp.full((cw, tq), -jnp.inf, BF16)
    n_pairs = (n_x + 2) // 2

    one16, zero16 = jnp.ones((cw, tq), BF16), jnp.zeros((cw, tq), BF16)

    def count16(c16):
        def tile_count(c):
            hit = jnp.where(sc16_ref[c] >= c16, one16, zero16).reshape(cw // 16, 16, tq)
            parts = [hit[i] for i in range(cw // 16)]
            while len(parts) > 1:
                parts = [parts[i] + parts[i + 1] for i in range(0, len(parts), 2)]
            return parts[0].astype(F32)

        cnt = lax.fori_loop(0, n_pairs, lambda p, a: a + (tile_count(2 * p) + tile_count(2 * p + 1)),
                            jnp.zeros((16, tq), F32))
        return jnp.sum(cnt, axis=0, keepdims=True)

    def stage1(it, k16):
        cand = k16 + lax.shift_left(jnp.int32(1), 15 - it)
        c32 = lax.bitcast_convert_type(_pattern_of_key16(cand), F32)
        c16 = jnp.broadcast_to(c32, (cw, tq)).astype(BF16)
        return jnp.where(count16(c16) >= kf, cand, k16)

    k16 = lax.fori_loop(0, 16, stage1, jnp.full((1, tq), BF16_KEY_MIN, I32))

    lo = _key32(_pattern_of_key16(jnp.maximum(k16 - 2, BF16_KEY_MIN)))
    hi = _key32(_pattern_of_key16(jnp.minimum(k16 + 2, BF16_KEY_MAX)))

    def count32(cf):
        def tile_count(c):
            hit = jnp.where(sc_ref[c] >= cf, 1.0, 0.0)
            return jnp.sum(hit.reshape(cw // 8, 8, tq), axis=0)

        cnt = lax.fori_loop(0, n_pairs, lambda p, a: a + (tile_count(2 * p) + tile_count(2 * p + 1)),
                            jnp.zeros((8, tq), F32))
        return jnp.sum(cnt, axis=0, keepdims=True)

    short = (qi * tq + lax.broadcasted_iota(I32, (1, tq), 1) + 1 + N_META) <= topk

    def unsettled(cnt_t):
        return (jnp.max(jnp.where(short | (cnt_t == kf), 0.0, 1.0)) > 0.5).astype(I32)

    def stage2_cond(st):
        it, _, _, go = st
        return jnp.logical_and(it < STAGE2_BITS, go > 0)

    def stage2(st):
        it, t, cnt_t, _ = st
        cand = t + lax.shift_left(jnp.int32(1), STAGE2_BITS - 1 - it)
        cnt = count32(lax.bitcast_convert_type(jnp.where(cand < 0, cand ^ 0x7FFFFFFF, cand), F32))
        ok = jnp.logical_and(cand < hi, cnt >= kf)
        cnt_t = jnp.where(ok, cnt, cnt_t)
        return it + 1, jnp.where(ok, cand, t), cnt_t, unsettled(cnt_t)

    _, t, _, _ = lax.while_loop(stage2_cond, stage2,
                                (jnp.int32(0), lo, jnp.full((1, tq), -1.0, F32), jnp.int32(1)))
    thr = lax.bitcast_convert_type(jnp.where(t < 0, t ^ 0x7FFFFFFF, t), F32)
    thr = jnp.where(short, jnp.finfo(F32).min, thr)

    for c in range(nt):
        @pl.when(c < n_x)
        def _(c=c):
            bias_ref[0, c * cw:(c + 1) * cw, :] = jnp.where(sc_ref[c + 1] >= thr, 0.0, NEG_BIAS).astype(BF16)

        @pl.when(c >= n_x)
        def _(c=c):
            bias_ref[0, c * cw:(c + 1) * cw, :] = jnp.full((cw, tq), NEG_BIAS, BF16)

    bias_ref[0, seq:seq + LANES, :] = jnp.where(sc_ref[0][0:LANES] >= thr, 0.0, NEG_BIAS).astype(BF16)


def _idx(p16, p32, ik2_meta, batch, seq, topk):
    nq = seq // IDX_TQ
    nt = seq // IDX_CW
    return pl.pallas_call(
        functools.partial(_idx_kernel, seq=seq, topk=topk),
        out_shape=jax.ShapeDtypeStruct((batch, seq + LANES, seq), BF16),
        grid=(batch, nq),
        in_specs=[
            pl.BlockSpec((4, IDX_TQ, SLAB), lambda b, q: (P16_IQ, b * nq + q, 0)),
            pl.BlockSpec((1, IDX_TQ, SLAB), lambda b, q: (P32_SMALL_SLAB, b * nq + q, 0)),
            pl.BlockSpec((1, seq, SLAB), lambda b, q: (P16_IK2_SLAB, b, 0)),
            pl.BlockSpec((IDX_CW, SLAB), lambda b, q: (0, 0)),
        ],
        out_specs=pl.BlockSpec((1, seq + LANES, IDX_TQ), lambda b, q: (b, 0, q)),
        scratch_shapes=[
            pltpu.VMEM((nt + 2, IDX_CW, IDX_TQ), F32),
            pltpu.VMEM((nt + 2, IDX_CW, IDX_TQ), BF16),
            pltpu.VMEM((LANES, IDX_TQ), F32),
        ],
        compiler_params=_params(("parallel", "arbitrary")),
        name="idx",
    )(p16, p32, p16, ik2_meta)


ATTN_AHEAD = 2


def _attn_kernel(qt_ref, kt_ref, q_ref, k_ref, vt_ref, b_ref, bm_ref, km_ref, vtm_ref, o_ref,
                 m_ref, l_ref, acc_ref, *, tq, tk):
    p = pl.program_id(1)
    qi, ki = qt_ref[p], kt_ref[p]

    def head(ref, h):
        return ref[h // 2][:, (h % 2) * LANES:(h % 2 + 1) * LANES]

    def accumulate(kref, vtref, bias):
        def scores(h):
            return _dot_nt(head(kref, h), head(q_ref, h)) + bias

        pending = [scores(h) for h in range(ATTN_AHEAD)]
        for h in range(DSA_HEADS):
            s = pending.pop(0)
            if h + ATTN_AHEAD < DSA_HEADS:
                pending.append(scores(h + ATTN_AHEAD))
            m_prev = m_ref[h]
            m_new = jnp.maximum(m_prev, jnp.max(s, axis=0, keepdims=True))
            alpha = jnp.exp2(m_prev - m_new)
            pr = jnp.exp2(s - m_new)
            l_ref[h] = alpha * l_ref[h] + jnp.sum(pr, axis=0, keepdims=True)
            vt = vtref[h * DSA_HEAD_DIM:(h + 1) * DSA_HEAD_DIM, :]
            acc_ref[h] = alpha * acc_ref[h] + _dot(vt, pr.astype(BF16))
            m_ref[h] = m_new

    @pl.when(ki == 0)
    def _():
        m_ref[...] = jnp.full(m_ref.shape, -jnp.inf, F32)
        l_ref[...] = jnp.zeros_like(l_ref)
        acc_ref[...] = jnp.zeros_like(acc_ref)
        accumulate(km_ref, vtm_ref, bm_ref[0].astype(F32))

    accumulate(k_ref, vt_ref, b_ref[0].astype(F32))

    @pl.when(ki == (qi * tq + tq - 1) // tk)
    def _():
        for h in range(DSA_HEADS):
            hs = slice(h * DSA_HEAD_DIM, (h + 1) * DSA_HEAD_DIM)
            o_ref[:, hs] = (acc_ref[h] / l_ref[h]).T.astype(o_ref.dtype)


def _attn(p16, vt, bias, p16_meta, vt_meta, batch, seq, tq, tk):
    nq, nk = seq // tq, seq // tk
    pairs = [(q, k) for q in range(nq) for k in range((q * tq + tq - 1) // tk + 1)]
    q_tab = jnp.asarray([q for q, _ in pairs], I32)
    k_tab = jnp.asarray([k for _, k in pairs], I32)
    grid_spec = pltpu.PrefetchScalarGridSpec(
        num_scalar_prefetch=2,
        grid=(batch, len(pairs)),
        in_specs=[
            pl.BlockSpec((4, tq, SLAB), lambda b, p, qt, kt: (P16_DQ, b * nq + qt[p], 0)),
            pl.BlockSpec((4, tk, SLAB), lambda b, p, qt, kt: (P16_DK, b * nk + kt[p], 0)),
            pl.BlockSpec((DSA_WIDTH, tk), lambda b, p, qt, kt: (0, b * nk + kt[p])),
            pl.BlockSpec((1, tk, tq), lambda b, p, qt, kt: (b, kt[p], qt[p])),
            pl.BlockSpec((1, LANES, tq), lambda b, p, qt, kt: (b, seq // LANES, qt[p])),
            pl.BlockSpec((4, LANES, SLAB), lambda b, p, qt, kt: (P16_DK, 0, 0)),
            pl.BlockSpec((DSA_WIDTH, LANES), lambda b, p, qt, kt: (0, 0)),
        ],
        out_specs=pl.BlockSpec((tq, DSA_WIDTH), lambda b, p, qt, kt: (b * nq + qt[p], 0)),
        scratch_shapes=[
            pltpu.VMEM((DSA_HEADS, 1, tq), F32),
            pltpu.VMEM((DSA_HEADS, 1, tq), F32),
            pltpu.VMEM((DSA_HEADS, DSA_HEAD_DIM, tq), F32),
        ],
    )
    return pl.pallas_call(
        functools.partial(_attn_kernel, tq=tq, tk=tk),
        out_shape=jax.ShapeDtypeStruct((batch * seq, DSA_WIDTH), BF16),
        grid_spec=grid_spec,
        compiler_params=_params(("parallel", "arbitrary")),
        name="attn",
    )(q_tab, k_tab, p16, p16, vt, bias, bias, p16_meta, vt_meta)


def _log_sigmoid(x):
    return jnp.minimum(x, 0.0) - jnp.log1p(jnp.exp(-jnp.abs(x)))


def _gla_kernel(gq_ref, gk_ref, go_ref, sm_ref, gv_ref, mk_ref, mv_ref, msm_ref, wgate_ref, bgate_ref,
                ng_ref, o_ref, st_ref, bc_ref):
    c = GLA_CHUNK

    def log_decay(small):
        return _log_sigmoid(_dot_f32(small, wgate_ref[...]) + bgate_ref[...]) / GLA_TAU

    def lower_ones(n):
        r = lax.broadcasted_iota(I32, (n, n), 0)
        cc = lax.broadcasted_iota(I32, (n, n), 1)
        return r >= cc

    @pl.when(pl.program_id(1) == 0)
    def _():
        bcm = _dot_exact_lhs01(jnp.where(lower_ones(N_META), 1.0, 0.0).astype(BF16),
                               log_decay(msm_ref[0]))
        for h in range(GLA_HEADS):
            ks = slice(h * GLA_DK, (h + 1) * GLA_DK)
            kd = mk_ref[h // 2][:, (h % 2) * GLA_DK:(h % 2 + 1) * GLA_DK] * jnp.exp(
                bcm[N_META - 1:N_META, ks] - bcm[:, ks])
            st_ref[h] = _dot_tn(mv_ref[h], kd.astype(BF16))

    rows = bc_ref.shape[0]
    r_i = lax.broadcasted_iota(I32, (rows, rows), 0)
    c_i = lax.broadcasted_iota(I32, (rows, rows), 1)
    chunk_tri = jnp.logical_and(r_i >= c_i, r_i // c == c_i // c)
    bc_ref[...] = _dot_exact_lhs01(jnp.where(chunk_tri, 1.0, 0.0).astype(BF16), log_decay(sm_ref[0]))
    causal = lower_ones(c)
    sub = GLA_SUB
    lane = lax.broadcasted_iota(I32, (sub, c), 1)

    for ch in range(rows // c):
        base = ch * c
        for h in range(GLA_HEADS):
            ks = slice(h * GLA_DK, (h + 1) * GLA_DK)
            hsl = slice((h % 2) * GLA_DK, (h % 2 + 1) * GLA_DK)
            q = gq_ref[h // 2][base:base + c, hsl] * (GLA_DK ** -0.5)
            k = gk_ref[h // 2][base:base + c, hsl]
            v = gv_ref[h][base:base + c]
            bc = bc_ref[base:base + c, ks]

            blocks = []
            for s0 in range(0, c, sub):
                q_s, bc_s = q[s0:s0 + sub], bc[s0:s0 + sub]
                diag = jnp.zeros((sub, c), F32)
                for j in range(s0, s0 + sub):
                    e = jnp.exp(jnp.minimum(bc_s - bc_ref[base + j:base + j + 1, ks], 0.0))
                    k_j = gk_ref[h // 2][base + j:base + j + 1, hsl]
                    diag = jnp.where(lane == j, jnp.sum(q_s * k_j * e, axis=1, keepdims=True), diag)
                if s0:
                    b_edge = bc_ref[base + s0 - 1:base + s0, ks]
                    q_t = q_s * jnp.exp(bc_s - b_edge)
                    k_t = k * jnp.exp(jnp.minimum(b_edge - bc, 0.0))
                    diag = jnp.where(lane < s0, _dot_nt(q_t.astype(BF16), k_t.astype(BF16)), diag)
                blocks.append(diag)
            attn = jnp.where(causal, jnp.concatenate(blocks, axis=0), 0.0)

            state = st_ref[h]
            b_last = bc[c - 1:c, :]
            o = _dot_nt((q * jnp.exp(bc)).astype(BF16), state.astype(BF16))
            o = o + _dot(attn.astype(BF16), v)
            kd = k * jnp.exp(b_last - bc)
            st_ref[h] = state * jnp.exp(b_last) + _dot_tn(v, kd.astype(BF16))

            g = go_ref[h][base:base + c]
            y = _rms(o, ng_ref[...]) * (g * jax.nn.sigmoid(g))
            o_ref[base:base + c, h * GLA_DV:(h + 1) * GLA_DV] = y.astype(o_ref.dtype)


def _gla(p32, p16, p32_meta, p16_meta, w_gate_pad, b_gate, norm_g, batch, seq):
    c = _tile(seq, GLA_ROWS)
    ns = seq // c
    row = lambda b, i: b * ns + i
    return pl.pallas_call(
        _gla_kernel,
        out_shape=jax.ShapeDtypeStruct((batch * seq, GLA_WIDTH), BF16),
        grid=(batch, ns),
        in_specs=[
            pl.BlockSpec((2, c, SLAB), lambda b, i: (P32_GQ_SLAB // 2, row(b, i), 0)),
            pl.BlockSpec((2, c, SLAB), lambda b, i: (P32_GK_SLAB // 2, row(b, i), 0)),
            pl.BlockSpec((4, c, SLAB), lambda b, i: (P32_GOUT_SLAB // 4, row(b, i), 0)),
            pl.BlockSpec((1, c, SLAB), lambda b, i: (P32_SMALL_SLAB, row(b, i), 0)),
            pl.BlockSpec((4, c, SLAB), lambda b, i: (P16_GV, row(b, i), 0)),
            pl.BlockSpec((2, N_META, SLAB), lambda b, i: (P32_GK_SLAB // 2, 0, 0)),
            pl.BlockSpec((4, N_META, SLAB), lambda b, i: (P16_GV, 0, 0)),
            pl.BlockSpec((1, N_META, SLAB), lambda b, i: (P32_SMALL_SLAB, 0, 0)),
            pl.BlockSpec((SLAB, GLA_KEY_WIDTH), lambda b, i: (0, 0)),
            pl.BlockSpec((1, GLA_KEY_WIDTH), lambda b, i: (0, 0)),
            pl.BlockSpec((1, GLA_DV), lambda b, i: (0, 0)),
        ],
        out_specs=pl.BlockSpec((c, GLA_WIDTH), lambda b, i: (row(b, i), 0)),
        scratch_shapes=[
            pltpu.VMEM((GLA_HEADS, GLA_DV, GLA_DK), F32),
            pltpu.VMEM((c, GLA_KEY_WIDTH), F32),
        ],
        compiler_params=_params(("parallel", "arbitrary")),
        name="gla",
    )(p32, p32, p32, p32, p16, p32_meta, p16_meta, p32_meta, w_gate_pad, b_gate, norm_g)


def _out_kernel(h_ref, a_ref, g_ref, wa_ref, wg_ref, pg_ref, o_ref):
    m = _dot(a_ref[...], wa_ref[...]) + _dot(g_ref[...], wg_ref[...])
    o_ref[...] = h_ref[...] + _rms(m, pg_ref[...])


def _out(hs, o_dsa, o_gla, w_dsa, w_gla, post_g, tm):
    r, d = hs.shape
    return pl.pallas_call(
        _out_kernel,
        out_shape=jax.ShapeDtypeStruct((r, d), F32),
        grid=(r // tm,),
        in_specs=[
            pl.BlockSpec((tm, d), lambda i: (i, 0)),
            pl.BlockSpec((tm, DSA_WIDTH), lambda i: (i, 0)),
            pl.BlockSpec((tm, GLA_WIDTH), lambda i: (i, 0)),
            pl.BlockSpec((DSA_WIDTH, d), lambda i: (0, 0)),
            pl.BlockSpec((GLA_WIDTH, d), lambda i: (0, 0)),
            pl.BlockSpec((1, d), lambda i: (0, 0)),
        ],
        out_specs=pl.BlockSpec((tm, d), lambda i: (i, 0)),
        compiler_params=_params(("parallel",)),
        name="outproj",
    )(hs, o_dsa, o_gla, w_dsa, w_gla, post_g)


def _tile(n, pref):
    t = min(n, pref)
    while n % t:
        t //= 2
    return t


def _proj_weights(w_in):
    d = w_in.shape[0]
    offs = np.cumsum((0,) + IN_SPLITS)
    dq, dk, dv, iq, ik, iw, gq, gk, gv, glow, gout = (w_in[:, offs[i]:offs[i + 1]] for i in range(11))
    z = lambda n: jnp.zeros((d, n), w_in.dtype)
    ik2 = jnp.concatenate([ik, z(IDX_DIM), z(IDX_DIM), ik], axis=1)
    q_scale = DSA_HEAD_DIM ** -0.5 * float(np.log2(np.e))
    w16 = jnp.concatenate([dq * q_scale, dk, dv, iq * (IDX_DIM ** -0.5), gv, ik2], axis=1)
    small = jnp.concatenate([iw * (IDX_HEADS ** -0.5), glow, z(SLAB - IDX_HEADS - GLA_GATE_RANK)], axis=1)
    w32 = jnp.concatenate([gq, gk, gout, small], axis=1)
    return w16.astype(BF16), w32.astype(BF16)


def kernel(x, meta_tokens, ffn1_pre_g, ffn1_w_gate, ffn1_w_up, ffn1_w_down, ffn1_post_g, mix_pre_g, w_in, w_gate_up, b_gate, gla_norm_g, w_out, mix_post_g, ffn2_pre_g, ffn2_w_gate, ffn2_w_up, ffn2_w_down, ffn2_post_g):
    batch, seq, d = x.shape
    depth = w_in.shape[0]
    assert depth == 1, "the meta rows skip the mixer, which is only valid for the last layer"
    d_ff = ffn1_w_gate.shape[-1]
    topk = min(TOPK_MAX, seq // 4)
    rows = batch * seq
    tm = _tile(rows, 512)
    tm_ffn = _tile(rows, 512)
    tf = _tile(d_ff, 512)
    tq = _tile(seq, 256)
    tk = _tile(seq, 1024)

    hs = x.reshape(rows, d)
    hm = meta_tokens.astype(x.dtype)
    row2 = lambda v: v.reshape(1, -1)
    for l in range(depth):
        ffn1 = (row2(ffn1_pre_g[l]), ffn1_w_gate[l].astype(BF16), ffn1_w_up[l].astype(BF16),
                ffn1_w_down[l].astype(BF16), row2(ffn1_post_g[l]))
        hs = _ffn(hs, *ffn1, tm=tm_ffn, tf=tf)
        hm = _ffn(hm, *ffn1, tm=N_META, tf=tf)

        w16, w32 = _proj_weights(w_in[l])
        pre = row2(mix_pre_g[l])
        proj16 = functools.partial(_proj, g=pre, w=w16, out_dtype=BF16, slabs_per_step=P16_SLABS_PER_STEP)
        proj32 = functools.partial(_proj, g=pre, w=w32, out_dtype=F32, slabs_per_step=P32_SLABS_PER_STEP)
        p16, p32 = proj16(hs, tm=tm), proj32(hs, tm=tm)
        p16m, p32m = proj16(hm, tm=N_META), proj32(hm, tm=N_META)

        pad_rows = lambda a, n: jnp.pad(a, ((0, 0), (0, n - a.shape[1]), (0, 0)))
        bias = _idx(p16, p32, pad_rows(p16m, IDX_CW)[P16_IK2_SLAB], batch, seq, topk)
        p16m_pad = pad_rows(p16m, LANES)
        v_t = lambda a: jnp.transpose(a[4 * P16_DV:4 * P16_DV + 4], (0, 2, 1)).reshape(DSA_WIDTH, -1)
        o_dsa = _attn(p16, v_t(p16), bias, p16m_pad, v_t(p16m_pad), batch, seq, tq, tk)

        w_gate_pad = jnp.zeros((SLAB, GLA_KEY_WIDTH), F32).at[
            IDX_HEADS:IDX_HEADS + GLA_GATE_RANK].set(w_gate_up[l])
        o_gla = _gla(p32, p16, p32m, p16m, w_gate_pad, row2(b_gate[l]), row2(gla_norm_g[l]), batch, seq)

        wo = w_out[l].astype(BF16)
        hs = _out(hs, o_dsa, o_gla, wo[:DSA_WIDTH], wo[DSA_WIDTH:], row2(mix_post_g[l]), tm)
        ffn2 = (row2(ffn2_pre_g[l]), ffn2_w_gate[l].astype(BF16), ffn2_w_up[l].astype(BF16),
                ffn2_w_down[l].astype(BF16), row2(ffn2_post_g[l]))
        hs = _ffn(hs, *ffn2, tm=tm_ffn, tf=tf)
    return hs.reshape(batch, seq, d)
```

```python
import functools

import numpy as np
import jax
import jax.numpy as jnp
from jax import lax
from jax.experimental import pallas as pl
from jax.experimental.pallas import tpu as pltpu

F32, BF16, I32 = jnp.float32, jnp.bfloat16, jnp.int32

N_META = 16
FFN_RES = 0.5
EPS = 1e-6
DSA_HEADS = 8
DSA_HEAD_DIM = 128
DSA_WIDTH = DSA_HEADS * DSA_HEAD_DIM
IDX_HEADS = 16
IDX_DIM = 64
TOPK_MAX = 256
GLA_HEADS = 4
GLA_DK = 128
GLA_DV = 256
GLA_KEY_WIDTH = GLA_HEADS * GLA_DK
GLA_WIDTH = GLA_HEADS * GLA_DV
GLA_GATE_RANK = 16
GLA_TAU = 16.0
GLA_CHUNK = 64
GLA_SUB = 16
GLA_ROWS = 128
IN_SPLITS = (DSA_WIDTH, DSA_WIDTH, DSA_WIDTH, IDX_HEADS * IDX_DIM, IDX_DIM, IDX_HEADS,
             GLA_KEY_WIDTH, GLA_KEY_WIDTH, GLA_WIDTH, GLA_GATE_RANK, GLA_WIDTH)

LANES = 128
SLAB = 256
VMEM_LIMIT_BYTES = 56 * 1024 * 1024

P16_DQ, P16_DK, P16_DV, P16_IQ, P16_GV, P16_IK2_SLAB, P16_SLABS = 0, 1, 2, 3, 4, 20, 21
P32_GQ_SLAB, P32_GK_SLAB, P32_GOUT_SLAB, P32_SMALL_SLAB, P32_SLABS = 0, 2, 4, 8, 9
P16_SLABS_PER_STEP, P32_SLABS_PER_STEP = 7, 3

NEG_BIAS = -1e30
INT_MIN = -(2 ** 31)
KEY_NEG_INF = int(np.array(0xFF800000 ^ 0x7FFFFFFF, dtype=np.uint32).view(np.int32))


def _params(sem, **flags):
    return pltpu.CompilerParams(dimension_semantics=sem, vmem_limit_bytes=VMEM_LIMIT_BYTES,
                                flags=flags or None)


def _dot(a, b):
    return jnp.dot(a, b, preferred_element_type=F32)


def _dot_nt(a, b):
    return lax.dot_general(a, b, (((1,), (1,)), ((), ())), preferred_element_type=F32)


def _dot_tn(a, b):
    return lax.dot_general(a, b, (((0,), (0,)), ((), ())), preferred_element_type=F32)


def _split3(x):
    hi = x.astype(BF16)
    r = x - hi.astype(F32)
    mid = r.astype(BF16)
    lo = (r - mid.astype(F32)).astype(BF16)
    return hi, mid, lo


def _dot_exact_lhs01(l01, x):
    hi, mid, lo = _split3(x)
    return _dot(l01, hi) + _dot(l01, mid) + _dot(l01, lo)


def _dot_f32(a, b):
    ah, am, al = _split3(a)
    bh, bm, bl = _split3(b)
    return (_dot(ah, bh) + (_dot(ah, bm) + _dot(am, bh))
            + (_dot(ah, bl) + _dot(am, bm) + _dot(al, bh)))


def _rms(x, g):
    return x * lax.rsqrt(jnp.mean(x * x, axis=-1, keepdims=True) + EPS) * g


def _ffn_kernel(x_ref, preg_ref, wg_ref, wu_ref, wd_ref, postg_ref, o_ref, xn_ref, *, nf):
    f = pl.program_id(1)

    @pl.when(f == 0)
    def _():
        xn_ref[...] = _rms(x_ref[...], preg_ref[...]).astype(BF16)
        o_ref[...] = jnp.zeros_like(o_ref)

    xn = xn_ref[...]
    g = _dot(xn, wg_ref[...])
    u = _dot(xn, wu_ref[...])
    a = (g * jax.nn.sigmoid(g)) * u
    o_ref[...] += _dot(a.astype(BF16), wd_ref[...])

    @pl.when(f == nf - 1)
    def _():
        o_ref[...] = x_ref[...] + FFN_RES * _rms(o_ref[...], postg_ref[...])


def _ffn(x, pre_g, wg, wu, wd, post_g, tm, tf):
    r, d = x.shape
    nf = wg.shape[1] // tf
    return pl.pallas_call(
        functools.partial(_ffn_kernel, nf=nf),
        out_shape=jax.ShapeDtypeStruct((r, d), F32),
        grid=(r // tm, nf),
        in_specs=[
            pl.BlockSpec((tm, d), lambda i, f: (i, 0)),
            pl.BlockSpec((1, d), lambda i, f: (0, 0)),
            pl.BlockSpec((d, tf), lambda i, f: (0, f)),
            pl.BlockSpec((d, tf), lambda i, f: (0, f)),
            pl.BlockSpec((tf, d), lambda i, f: (f, 0)),
            pl.BlockSpec((1, d), lambda i, f: (0, 0)),
        ],
        out_specs=pl.BlockSpec((tm, d), lambda i, f: (i, 0)),
        scratch_shapes=[pltpu.VMEM((tm, d), BF16)],
        compiler_params=_params(("parallel", "arbitrary")),
        name="ffn",
    )(x, pre_g, wg, wu, wd, post_g)


def _proj_kernel(x_ref, g_ref, w_ref, o_ref, xn_ref):
    @pl.when(pl.program_id(1) == 0)
    def _():
        xn_ref[...] = _rms(x_ref[...], g_ref[...]).astype(BF16)

    res = _dot(xn_ref[...], w_ref[...])
    for s in range(o_ref.shape[0]):
        o_ref[s] = res[:, s * SLAB:(s + 1) * SLAB].astype(o_ref.dtype)


def _proj(x, g, w, out_dtype, tm, slabs_per_step):
    r, d = x.shape
    n_slabs = w.shape[1] // SLAB
    tn = slabs_per_step * SLAB
    return pl.pallas_call(
        _proj_kernel,
        out_shape=jax.ShapeDtypeStruct((n_slabs, r, SLAB), out_dtype),
        grid=(r // tm, n_slabs // slabs_per_step),
        in_specs=[
            pl.BlockSpec((tm, d), lambda i, j: (i, 0)),
            pl.BlockSpec((1, d), lambda i, j: (0, 0)),
            pl.BlockSpec((d, tn), lambda i, j: (0, j)),
        ],
        out_specs=pl.BlockSpec((slabs_per_step, tm, SLAB), lambda i, j: (j, i, 0)),
        scratch_shapes=[pltpu.VMEM((tm, d), BF16)],
        compiler_params=_params(("parallel", "arbitrary")),
        name="proj",
    )(x, g, w)


IDX_TQ = 256
IDX_CW = 256
BF16_KEY_MIN, BF16_KEY_MAX = -(2 ** 15), 2 ** 15 - 1
STAGE2_BITS = 18


def _key32(pattern):
    return jnp.where(pattern < 0, pattern ^ 0x7FFFFFFF, pattern)


def _pattern_of_key16(k16):
    return lax.shift_left(jnp.where(k16 < 0, k16 ^ 0x7FFF, k16), 16)


def _idx_kernel(iq_ref, sm_ref, ik_ref, ikm_ref, bias_ref, sc_ref, sc16_ref, w_ref, *, seq, topk):
    qi = pl.program_id(1)
    tq, cw = IDX_TQ, IDX_CW
    nt = seq // cw
    n_x = qi + 1
    kf = float(topk)

    w_ref[...] = sm_ref[0][:, 0:LANES].T
    key_pos = lax.broadcasted_iota(I32, (cw, tq), 0)
    q_pos = qi * tq + lax.broadcasted_iota(I32, (cw, tq), 1)

    def score_tile(k2):
        lhs = jnp.concatenate([k2[:, 0:LANES], k2[:, LANES:2 * LANES]], axis=0)
        acc = jnp.zeros((cw, tq), F32)
        for p in range(IDX_HEADS // 2):
            rhs = iq_ref[p // 2][:, (p % 2) * LANES:(p % 2 + 1) * LANES]
            l2 = _dot_nt(lhs, rhs)
            acc = acc + jnp.maximum(l2[0:cw], 0.0) * w_ref[2 * p:2 * p + 1, :]
            acc = acc + jnp.maximum(l2[cw:2 * cw], 0.0) * w_ref[2 * p + 1:2 * p + 2, :]
        return acc

    def x_tile(c):
        k2 = ik_ref[0, pl.ds(pl.multiple_of(c * cw, cw), cw), :]
        acc = jnp.where(c * cw + key_pos <= q_pos, score_tile(k2), -jnp.inf)
        sc_ref[c + 1] = acc
        sc16_ref[c + 1] = acc.astype(BF16)

    def x_tile_pair(p, carry):
        x_tile(2 * p)
        x_tile(2 * p + 1)
        return carry

    lax.fori_loop(0, (n_x + 1) // 2, x_tile_pair, 0)
    acc_m = jnp.where(key_pos < N_META, score_tile(ikm_ref[...]), -jnp.inf)
    sc_ref[0] = acc_m
    sc16_ref[0] = acc_m.astype(BF16)
    sc_ref[n_x + 1] = jnp.full((cw, tq), -jnp.inf, F32)
    sc16_ref[n_x + 1] = jnp.full((cw, tq), -jnp.inf, BF16)
    n_pairs = (n_x + 2) // 2

    one16, zero16 = jnp.ones((cw, tq), BF16), jnp.zeros((cw, tq), BF16)

    def count16(c16):
        def tile_count(c):
            hit = jnp.where(sc16_ref[c] >= c16, one16, zero16).reshape(cw // 16, 16, tq)
            parts = [hit[i] for i in range(cw // 16)]
            while len(parts) > 1:
                parts = [parts[i] + parts[i + 1] for i in range(0, len(parts), 2)]
            return parts[0].astype(F32)

        cnt = lax.fori_loop(0, n_pairs, lambda p, a: a + (tile_count(2 * p) + tile_count(2 * p + 1)),
                            jnp.zeros((16, tq), F32))
        return jnp.sum(cnt, axis=0, keepdims=True)

    def stage1(it, k16):
        cand = k16 + lax.shift_left(jnp.int32(1), 15 - it)
        c32 = lax.bitcast_convert_type(_pattern_of_key16(cand), F32)
        c16 = jnp.broadcast_to(c32, (cw, tq)).astype(BF16)
        return jnp.where(count16(c16) >= kf, cand, k16)

    k16 = lax.fori_loop(0, 16, stage1, jnp.full((1, tq), BF16_KEY_MIN, I32))

    lo = _key32(_pattern_of_key16(jnp.maximum(k16 - 2, BF16_KEY_MIN)))
    hi = _key32(_pattern_of_key16(jnp.minimum(k16 + 2, BF16_KEY_MAX)))

    def count32(cf):
        def tile_count(c):
            hit = jnp.where(sc_ref[c] >= cf, 1.0, 0.0)
            return jnp.sum(hit.reshape(cw // 8, 8, tq), axis=0)

        cnt = lax.fori_loop(0, n_pairs, lambda p, a: a + (tile_count(2 * p) + tile_count(2 * p + 1)),
                            jnp.zeros((8, tq), F32))
        return jnp.sum(cnt, axis=0, keepdims=True)

    short = (qi * tq + lax.broadcasted_iota(I32, (1, tq), 1) + 1 + N_META) <= topk

    def unsettled(cnt_t):
        return (jnp.max(jnp.where(short | (cnt_t == kf), 0.0, 1.0)) > 0.5).astype(I32)

    def stage2_cond(st):
        it, _, _, go = st
        return jnp.logical_and(it < STAGE2_BITS, go > 0)

    def stage2(st):
        it, t, cnt_t, _ = st
        cand = t + lax.shift_left(jnp.int32(1), STAGE2_BITS - 1 - it)
        cnt = count32(lax.bitcast_convert_type(jnp.where(cand < 0, cand ^ 0x7FFFFFFF, cand), F32))
        ok = jnp.logical_and(cand < hi, cnt >= kf)
        cnt_t = jnp.where(ok, cnt, cnt_t)
        return it + 1, jnp.where(ok, cand, t), cnt_t, unsettled(cnt_t)

    _, t, cnt_t, tied_any = lax.while_loop(stage2_cond, stage2,
                                           (jnp.int32(0), lo, jnp.full((1, tq), -1.0, F32), jnp.int32(1)))
    thr = lax.bitcast_convert_type(jnp.where(t < 0, t ^ 0x7FFFFFFF, t), F32)
    thr = jnp.where(short, jnp.finfo(F32).min, thr)

    def write_bias(bias_of):
        for c in range(nt):
            @pl.when(c < n_x)
            def _(c=c):
                bias_ref[0, c * cw:(c + 1) * cw, :] = bias_of(
                    sc_ref[c + 1], N_META + c * cw + key_pos).astype(BF16)

            @pl.when(c >= n_x)
            def _(c=c):
                bias_ref[0, c * cw:(c + 1) * cw, :] = jnp.full((cw, tq), NEG_BIAS, BF16)

        meta_pos = lax.broadcasted_iota(I32, (LANES, tq), 0)
        bias_ref[0, seq:seq + LANES, :] = bias_of(sc_ref[0, 0:LANES, :], meta_pos).astype(BF16)

    @pl.when(tied_any == 0)
    def _():
        write_bias(lambda s, pos: jnp.where(s >= thr, 0.0, NEG_BIAS))

    @pl.when(tied_any > 0)
    def _():
        def count_where(hit_of):
            def tile_count(slot):
                pos = key_pos + jnp.where(slot == 0, 0, N_META + (slot - 1) * cw)
                return jnp.sum(hit_of(sc_ref[slot], pos).reshape(cw // 8, 8, tq), axis=0)

            cnt = lax.fori_loop(0, 2 * n_pairs, lambda s_, a: a + tile_count(s_), jnp.zeros((8, tq), F32))
            return jnp.sum(cnt, axis=0, keepdims=True)

        need = kf - count_where(lambda s, pos: jnp.where(s > thr, 1.0, 0.0))
        pos_bits = int(seq + N_META).bit_length()

        def pos_bit(i, cut):
            cand = cut | lax.shift_left(jnp.int32(1), pos_bits - 1 - i)
            below = count_where(lambda s, pos: jnp.where(s == thr, jnp.where(pos < cand, 1.0, 0.0), 0.0))
            return jnp.where(below < need, cand, cut)

        cut = lax.fori_loop(0, pos_bits, pos_bit, jnp.zeros((1, tq), I32))
        tied = jnp.logical_and(jnp.logical_not(short), cnt_t != kf)
        cut = jnp.where(tied, cut, jnp.iinfo(I32).max)
        write_bias(lambda s, pos: jnp.where(
            s > thr, 0.0, jnp.where(s == thr, jnp.where(pos <= cut, 0.0, NEG_BIAS), NEG_BIAS)))


def _idx(p16, p32, ik2_meta, batch, seq, topk):
    nq = seq // IDX_TQ
    nt = seq // IDX_CW
    return pl.pallas_call(
        functools.partial(_idx_kernel, seq=seq, topk=topk),
        out_shape=jax.ShapeDtypeStruct((batch, seq + LANES, seq), BF16),
        grid=(batch, nq),
        in_specs=[
            pl.BlockSpec((4, IDX_TQ, SLAB), lambda b, q: (P16_IQ, b * nq + q, 0)),
            pl.BlockSpec((1, IDX_TQ, SLAB), lambda b, q: (P32_SMALL_SLAB, b * nq + q, 0)),
            pl.BlockSpec((1, seq, SLAB), lambda b, q: (P16_IK2_SLAB, b, 0)),
            pl.BlockSpec((IDX_CW, SLAB), lambda b, q: (0, 0)),
        ],
        out_specs=pl.BlockSpec((1, seq + LANES, IDX_TQ), lambda b, q: (b, 0, q)),
        scratch_shapes=[
            pltpu.VMEM((nt + 2, IDX_CW, IDX_TQ), F32),
            pltpu.VMEM((nt + 2, IDX_CW, IDX_TQ), BF16),
            pltpu.VMEM((LANES, IDX_TQ), F32),
        ],
        compiler_params=_params(("parallel", "arbitrary")),
        name="idx",
    )(p16, p32, p16, ik2_meta)


ATTN_AHEAD = 2


def _attn_kernel(qt_ref, kt_ref, q_ref, k_ref, vt_ref, b_ref, bm_ref, km_ref, vtm_ref, o_ref,
                 m_ref, l_ref, acc_ref, *, tq, tk):
    p = pl.program_id(1)
    qi, ki = qt_ref[p], kt_ref[p]

    def head(ref, h):
        return ref[h // 2][:, (h % 2) * LANES:(h % 2 + 1) * LANES]

    def accumulate(kref, vtref, bias):
        def scores(h):
            return _dot_nt(head(kref, h), head(q_ref, h)) + bias

        pending = [scores(h) for h in range(ATTN_AHEAD)]
        for h in range(DSA_HEADS):
            s = pending.pop(0)
            if h + ATTN_AHEAD < DSA_HEADS:
                pending.append(scores(h + ATTN_AHEAD))
            m_prev = m_ref[h]
            m_new = jnp.maximum(m_prev, jnp.max(s, axis=0, keepdims=True))
            alpha = jnp.exp2(m_prev - m_new)
            pr = jnp.exp2(s - m_new)
            l_ref[h] = alpha * l_ref[h] + jnp.sum(pr, axis=0, keepdims=True)
            vt = vtref[h * DSA_HEAD_DIM:(h + 1) * DSA_HEAD_DIM, :]
            acc_ref[h] = alpha * acc_ref[h] + _dot(vt, pr.astype(BF16))
            m_ref[h] = m_new

    @pl.when(ki == 0)
    def _():
        m_ref[...] = jnp.full(m_ref.shape, -jnp.inf, F32)
        l_ref[...] = jnp.zeros_like(l_ref)
        acc_ref[...] = jnp.zeros_like(acc_ref)
        accumulate(km_ref, vtm_ref, bm_ref[0].astype(F32))

    accumulate(k_ref, vt_ref, b_ref[0].astype(F32))

    @pl.when(ki == (qi * tq + tq - 1) // tk)
    def _():
        for h in range(DSA_HEADS):
            hs = slice(h * DSA_HEAD_DIM, (h + 1) * DSA_HEAD_DIM)
            o_ref[:, hs] = (acc_ref[h] / l_ref[h]).T.astype(o_ref.dtype)


def _attn(p16, vt, bias, p16_meta, vt_meta, batch, seq, tq, tk):
    nq, nk = seq // tq, seq // tk
    pairs = [(q, k) for q in range(nq) for k in range((q * tq + tq - 1) // tk + 1)]
    q_tab = jnp.asarray([q for q, _ in pairs], I32)
    k_tab = jnp.asarray([k for _, k in pairs], I32)
    grid_spec = pltpu.PrefetchScalarGridSpec(
        num_scalar_prefetch=2,
        grid=(batch, len(pairs)),
        in_specs=[
            pl.BlockSpec((4, tq, SLAB), lambda b, p, qt, kt: (P16_DQ, b * nq + qt[p], 0)),
            pl.BlockSpec((4, tk, SLAB), lambda b, p, qt, kt: (P16_DK, b * nk + kt[p], 0)),
            pl.BlockSpec((DSA_WIDTH, tk), lambda b, p, qt, kt: (0, b * nk + kt[p])),
            pl.BlockSpec((1, tk, tq), lambda b, p, qt, kt: (b, kt[p], qt[p])),
            pl.BlockSpec((1, LANES, tq), lambda b, p, qt, kt: (b, seq // LANES, qt[p])),
            pl.BlockSpec((4, LANES, SLAB), lambda b, p, qt, kt: (P16_DK, 0, 0)),
            pl.BlockSpec((DSA_WIDTH, LANES), lambda b, p, qt, kt: (0, 0)),
        ],
        out_specs=pl.BlockSpec((tq, DSA_WIDTH), lambda b, p, qt, kt: (b * nq + qt[p], 0)),
        scratch_shapes=[
            pltpu.VMEM((DSA_HEADS, 1, tq), F32),
            pltpu.VMEM((DSA_HEADS, 1, tq), F32),
            pltpu.VMEM((DSA_HEADS, DSA_HEAD_DIM, tq), F32),
        ],
    )
    return pl.pallas_call(
        functools.partial(_attn_kernel, tq=tq, tk=tk),
        out_shape=jax.ShapeDtypeStruct((batch * seq, DSA_WIDTH), BF16),
        grid_spec=grid_spec,
        compiler_params=_params(("parallel", "arbitrary")),
        name="attn",
    )(q_tab, k_tab, p16, p16, vt, bias, bias, p16_meta, vt_meta)


def _log_sigmoid(x):
    return jnp.minimum(x, 0.0) - jnp.log1p(jnp.exp(-jnp.abs(x)))


def _gla_kernel(gq_ref, gk_ref, go_ref, sm_ref, gv_ref, mk_ref, mv_ref, msm_ref, wgate_ref, bgate_ref,
                ng_ref, o_ref, st_ref, bc_ref):
    c = GLA_CHUNK

    def log_decay(small):
        return _log_sigmoid(_dot_f32(small, wgate_ref[...]) + bgate_ref[...]) / GLA_TAU

    def lower_ones(n):
        r = lax.broadcasted_iota(I32, (n, n), 0)
        cc = lax.broadcasted_iota(I32, (n, n), 1)
        return r >= cc

    @pl.when(pl.program_id(1) == 0)
    def _():
        bcm = _dot_exact_lhs01(jnp.where(lower_ones(N_META), 1.0, 0.0).astype(BF16),
                               log_decay(msm_ref[0]))
        for h in range(GLA_HEADS):
            ks = slice(h * GLA_DK, (h + 1) * GLA_DK)
            kd = mk_ref[h // 2][:, (h % 2) * GLA_DK:(h % 2 + 1) * GLA_DK] * jnp.exp(
                bcm[N_META - 1:N_META, ks] - bcm[:, ks])
            st_ref[h] = _dot_tn(mv_ref[h], kd.astype(BF16))

    rows = bc_ref.shape[0]
    r_i = lax.broadcasted_iota(I32, (rows, rows), 0)
    c_i = lax.broadcasted_iota(I32, (rows, rows), 1)
    chunk_tri = jnp.logical_and(r_i >= c_i, r_i // c == c_i // c)
    bc_ref[...] = _dot_exact_lhs01(jnp.where(chunk_tri, 1.0, 0.0).astype(BF16), log_decay(sm_ref[0]))
    causal = lower_ones(c)
    sub = GLA_SUB
    lane = lax.broadcasted_iota(I32, (sub, c), 1)

    for ch in range(rows // c):
        base = ch * c
        for h in range(GLA_HEADS):
            ks = slice(h * GLA_DK, (h + 1) * GLA_DK)
            hsl = slice((h % 2) * GLA_DK, (h % 2 + 1) * GLA_DK)
            q = gq_ref[h // 2][base:base + c, hsl] * (GLA_DK ** -0.5)
            k = gk_ref[h // 2][base:base + c, hsl]
            v = gv_ref[h][base:base + c]
            bc = bc_ref[base:base + c, ks]

            blocks = []
            for s0 in range(0, c, sub):
                q_s, bc_s = q[s0:s0 + sub], bc[s0:s0 + sub]
                diag = jnp.zeros((sub, c), F32)
                for j in range(s0, s0 + sub):
                    e = jnp.exp(jnp.minimum(bc_s - bc_ref[base + j:base + j + 1, ks], 0.0))
                    k_j = gk_ref[h // 2][base + j:base + j + 1, hsl]
                    diag = jnp.where(lane == j, jnp.sum(q_s * k_j * e, axis=1, keepdims=True), diag)
                if s0:
                    b_edge = bc_ref[base + s0 - 1:base + s0, ks]
                    q_t = q_s * jnp.exp(bc_s - b_edge)
                    k_t = k * jnp.exp(jnp.minimum(b_edge - bc, 0.0))
                    diag = jnp.where(lane < s0, _dot_nt(q_t.astype(BF16), k_t.astype(BF16)), diag)
                blocks.append(diag)
            attn = jnp.where(causal, jnp.concatenate(blocks, axis=0), 0.0)

            state = st_ref[h]
            b_last = bc[c - 1:c, :]
            o = _dot_nt((q * jnp.exp(bc)).astype(BF16), state.astype(BF16))
            o = o + _dot(attn.astype(BF16), v)
            kd = k * jnp.exp(b_last - bc)
            st_ref[h] = state * jnp.exp(b_last) + _dot_tn(v, kd.astype(BF16))

            g = go_ref[h][base:base + c]
            y = _rms(o, ng_ref[...]) * (g * jax.nn.sigmoid(g))
            o_ref[base:base + c, h * GLA_DV:(h + 1) * GLA_DV] = y.astype(o_ref.dtype)


def _gla(p32, p16, p32_meta, p16_meta, w_gate_pad, b_gate, norm_g, batch, seq):
    c = _tile(seq, GLA_ROWS)
    ns = seq // c
    row = lambda b, i: b * ns + i
    return pl.pallas_call(
        _gla_kernel,
        out_shape=jax.ShapeDtypeStruct((batch * seq, GLA_WIDTH), BF16),
        grid=(batch, ns),
        in_specs=[
            pl.BlockSpec((2, c, SLAB), lambda b, i: (P32_GQ_SLAB // 2, row(b, i), 0)),
            pl.BlockSpec((2, c, SLAB), lambda b, i: (P32_GK_SLAB // 2, row(b, i), 0)),
            pl.BlockSpec((4, c, SLAB), lambda b, i: (P32_GOUT_SLAB // 4, row(b, i), 0)),
            pl.BlockSpec((1, c, SLAB), lambda b, i: (P32_SMALL_SLAB, row(b, i), 0)),
            pl.BlockSpec((4, c, SLAB), lambda b, i: (P16_GV, row(b, i), 0)),
            pl.BlockSpec((2, N_META, SLAB), lambda b, i: (P32_GK_SLAB // 2, 0, 0)),
            pl.BlockSpec((4, N_META, SLAB), lambda b, i: (P16_GV, 0, 0)),
            pl.BlockSpec((1, N_META, SLAB), lambda b, i: (P32_SMALL_SLAB, 0, 0)),
            pl.BlockSpec((SLAB, GLA_KEY_WIDTH), lambda b, i: (0, 0)),
            pl.BlockSpec((1, GLA_KEY_WIDTH), lambda b, i: (0, 0)),
            pl.BlockSpec((1, GLA_DV), lambda b, i: (0, 0)),
        ],
        out_specs=pl.BlockSpec((c, GLA_WIDTH), lambda b, i: (row(b, i), 0)),
        scratch_shapes=[
            pltpu.VMEM((GLA_HEADS, GLA_DV, GLA_DK), F32),
            pltpu.VMEM((c, GLA_KEY_WIDTH), F32),
        ],
        compiler_params=_params(("parallel", "arbitrary")),
        name="gla",
    )(p32, p32, p32, p32, p16, p32_meta, p16_meta, p32_meta, w_gate_pad, b_gate, norm_g)


def _out_kernel(h_ref, a_ref, g_ref, wa_ref, wg_ref, pg_ref, o_ref):
    m = _dot(a_ref[...], wa_ref[...]) + _dot(g_ref[...], wg_ref[...])
    o_ref[...] = h_ref[...] + _rms(m, pg_ref[...])


def _out(hs, o_dsa, o_gla, w_dsa, w_gla, post_g, tm):
    r, d = hs.shape
    return pl.pallas_call(
        _out_kernel,
        out_shape=jax.ShapeDtypeStruct((r, d), F32),
        grid=(r // tm,),
        in_specs=[
            pl.BlockSpec((tm, d), lambda i: (i, 0)),
            pl.BlockSpec((tm, DSA_WIDTH), lambda i: (i, 0)),
            pl.BlockSpec((tm, GLA_WIDTH), lambda i: (i, 0)),
            pl.BlockSpec((DSA_WIDTH, d), lambda i: (0, 0)),
            pl.BlockSpec((GLA_WIDTH, d), lambda i: (0, 0)),
            pl.BlockSpec((1, d), lambda i: (0, 0)),
        ],
        out_specs=pl.BlockSpec((tm, d), lambda i: (i, 0)),
        compiler_params=_params(("parallel",)),
        name="outproj",
    )(hs, o_dsa, o_gla, w_dsa, w_gla, post_g)


def _tile(n, pref):
    t = min(n, pref)
    while n % t:
        t //= 2
    return t


def _proj_weights(w_in):
    d = w_in.shape[0]
    offs = np.cumsum((0,) + IN_SPLITS)
    dq, dk, dv, iq, ik, iw, gq, gk, gv, glow, gout = (w_in[:, offs[i]:offs[i + 1]] for i in range(11))
    z = lambda n: jnp.zeros((d, n), w_in.dtype)
    ik2 = jnp.concatenate([ik, z(IDX_DIM), z(IDX_DIM), ik], axis=1)
    q_scale = DSA_HEAD_DIM ** -0.5 * float(np.log2(np.e))
    w16 = jnp.concatenate([dq * q_scale, dk, dv, iq * (IDX_DIM ** -0.5), gv, ik2], axis=1)
    small = jnp.concatenate([iw * (IDX_HEADS ** -0.5), glow, z(SLAB - IDX_HEADS - GLA_GATE_RANK)], axis=1)
    w32 = jnp.concatenate([gq, gk, gout, small], axis=1)
    return w16.astype(BF16), w32.astype(BF16)


def kernel(x, meta_tokens, ffn1_pre_g, ffn1_w_gate, ffn1_w_up, ffn1_w_down, ffn1_post_g, mix_pre_g, w_in, w_gate_up, b_gate, gla_norm_g, w_out, mix_post_g, ffn2_pre_g, ffn2_w_gate, ffn2_w_up, ffn2_w_down, ffn2_post_g):
    batch, seq, d = x.shape
    depth = w_in.shape[0]
    assert depth == 1, "the meta rows skip the mixer, which is only valid for the last layer"
    d_ff = ffn1_w_gate.shape[-1]
    topk = min(TOPK_MAX, seq // 4)
    rows = batch * seq
    tm = _tile(rows, 512)
    tm_ffn = _tile(rows, 512)
    tf = _tile(d_ff, 512)
    tq = _tile(seq, 256)
    tk = _tile(seq, 1024)

    hs = x.reshape(rows, d)
    hm = meta_tokens.astype(x.dtype)
    row2 = lambda v: v.reshape(1, -1)
    for l in range(depth):
        ffn1 = (row2(ffn1_pre_g[l]), ffn1_w_gate[l].astype(BF16), ffn1_w_up[l].astype(BF16),
                ffn1_w_down[l].astype(BF16), row2(ffn1_post_g[l]))
        hs = _ffn(hs, *ffn1, tm=tm_ffn, tf=tf)
        hm = _ffn(hm, *ffn1, tm=N_META, tf=tf)

        w16, w32 = _proj_weights(w_in[l])
        pre = row2(mix_pre_g[l])
        proj16 = functools.partial(_proj, g=pre, w=w16, out_dtype=BF16, slabs_per_step=P16_SLABS_PER_STEP)
        proj32 = functools.partial(_proj, g=pre, w=w32, out_dtype=F32, slabs_per_step=P32_SLABS_PER_STEP)
        tm_proj = _tile(rows, 1024)
        p16, p32 = proj16(hs, tm=tm_proj), proj32(hs, tm=tm_proj)
        p16m, p32m = proj16(hm, tm=N_META), proj32(hm, tm=N_META)

        pad_rows = lambda a, n: jnp.pad(a, ((0, 0), (0, n - a.shape[1]), (0, 0)))
        bias = _idx(p16, p32, pad_rows(p16m, IDX_CW)[P16_IK2_SLAB], batch, seq, topk)
        p16m_pad = pad_rows(p16m, LANES)
        v_t = lambda a: jnp.transpose(a[4 * P16_DV:4 * P16_DV + 4], (0, 2, 1)).reshape(DSA_WIDTH, -1)
        o_dsa = _attn(p16, v_t(p16), bias, p16m_pad, v_t(p16m_pad), batch, seq, tq, tk)

        w_gate_pad = jnp.zeros((SLAB, GLA_KEY_WIDTH), F32).at[
            IDX_HEADS:IDX_HEADS + GLA_GATE_RANK].set(w_gate_up[l])
        o_gla = _gla(p32, p16, p32m, p16m, w_gate_pad, row2(b_gate[l]), row2(gla_norm_g[l]), batch, seq)

        wo = w_out[l].astype(BF16)
        hs = _out(hs, o_dsa, o_gla, wo[:DSA_WIDTH], wo[DSA_WIDTH:], row2(mix_post_g[l]), tm)
        ffn2 = (row2(ffn2_pre_g[l]), ffn2_w_gate[l].astype(BF16), ffn2_w_up[l].astype(BF16),
                ffn2_w_down[l].astype(BF16), row2(ffn2_post_g[l]))
        hs = _ffn(hs, *ffn2, tm=tm_ffn, tf=tf)
    return hs.reshape(batch, seq, d)
```

```python
import functools

import numpy as np
import jax
import jax.numpy as jnp
from jax import lax
from jax.experimental import pallas as pl
from jax.experimental.pallas import tpu as pltpu

F32, BF16, I32 = jnp.float32, jnp.bfloat16, jnp.int32

N_META = 16
FFN_RES = 0.5
EPS = 1e-6
DSA_HEADS = 8
DSA_HEAD_DIM = 128
DSA_WIDTH = DSA_HEADS * DSA_HEAD_DIM
IDX_HEADS = 16
IDX_DIM = 64
TOPK_MAX = 256
GLA_HEADS = 4
GLA_DK = 128
GLA_DV = 256
GLA_KEY_WIDTH = GLA_HEADS * GLA_DK
GLA_WIDTH = GLA_HEADS * GLA_DV
GLA_GATE_RANK = 16
GLA_TAU = 16.0
GLA_CHUNK = 64
GLA_SUB = 16
GLA_ROWS = 256
IN_SPLITS = (DSA_WIDTH, DSA_WIDTH, DSA_WIDTH, IDX_HEADS * IDX_DIM, IDX_DIM, IDX_HEADS,
             GLA_KEY_WIDTH, GLA_KEY_WIDTH, GLA_WIDTH, GLA_GATE_RANK, GLA_WIDTH)

LANES = 128
SLAB = 256
VMEM_LIMIT_BYTES = 56 * 1024 * 1024

P16_DQ, P16_DK, P16_DV, P16_IQ, P16_GV, P16_IK2_SLAB, P16_SLABS = 0, 1, 2, 3, 4, 20, 21
P32_GQ_SLAB, P32_GK_SLAB, P32_GOUT_SLAB, P32_SMALL_SLAB, P32_SLABS = 0, 2, 4, 8, 9
P16_SLABS_PER_STEP, P32_SLABS_PER_STEP = 7, 3

NEG_BIAS = -1e30


def _params(sem, **flags):
    return pltpu.CompilerParams(dimension_semantics=sem, vmem_limit_bytes=VMEM_LIMIT_BYTES,
                                flags=flags or None)


def _dot(a, b):
    return jnp.dot(a, b, preferred_element_type=F32)


def _dot_nt(a, b):
    return lax.dot_general(a, b, (((1,), (1,)), ((), ())), preferred_element_type=F32)


def _dot_tn(a, b):
    return lax.dot_general(a, b, (((0,), (0,)), ((), ())), preferred_element_type=F32)


def _split3(x):
    hi = x.astype(BF16)
    r = x - hi.astype(F32)
    mid = r.astype(BF16)
    lo = (r - mid.astype(F32)).astype(BF16)
    return hi, mid, lo


def _dot_exact_lhs01(l01, x):
    hi, mid, lo = _split3(x)
    return _dot(l01, hi) + _dot(l01, mid) + _dot(l01, lo)


def _dot_f32(a, b):
    ah, am, al = _split3(a)
    bh, bm, bl = _split3(b)
    return (_dot(ah, bh) + (_dot(ah, bm) + _dot(am, bh))
            + (_dot(ah, bl) + _dot(am, bm) + _dot(al, bh)))


def _rms(x, g):
    return x * lax.rsqrt(jnp.mean(x * x, axis=-1, keepdims=True) + EPS) * g


def _ffn_kernel(x_ref, preg_ref, wg_ref, wu_ref, wd_ref, postg_ref, o_ref, xn_ref, *, nf):
    f = pl.program_id(1)

    @pl.when(f == 0)
    def _():
        xn_ref[...] = _rms(x_ref[...], preg_ref[...]).astype(BF16)
        o_ref[...] = jnp.zeros_like(o_ref)

    xn = xn_ref[...]
    g = _dot(xn, wg_ref[...])
    u = _dot(xn, wu_ref[...])
    a = (g * jax.nn.sigmoid(g)) * u
    o_ref[...] += _dot(a.astype(BF16), wd_ref[...])

    @pl.when(f == nf - 1)
    def _():
        o_ref[...] = x_ref[...] + FFN_RES * _rms(o_ref[...], postg_ref[...])


def _ffn(x, pre_g, wg, wu, wd, post_g, tm, tf):
    r, d = x.shape
    nf = wg.shape[1] // tf
    return pl.pallas_call(
        functools.partial(_ffn_kernel, nf=nf),
        out_shape=jax.ShapeDtypeStruct((r, d), F32),
        grid=(r // tm, nf),
        in_specs=[
            pl.BlockSpec((tm, d), lambda i, f: (i, 0)),
            pl.BlockSpec((1, d), lambda i, f: (0, 0)),
            pl.BlockSpec((d, tf), lambda i, f: (0, f)),
            pl.BlockSpec((d, tf), lambda i, f: (0, f)),
            pl.BlockSpec((tf, d), lambda i, f: (f, 0)),
            pl.BlockSpec((1, d), lambda i, f: (0, 0)),
        ],
        out_specs=pl.BlockSpec((tm, d), lambda i, f: (i, 0)),
        scratch_shapes=[pltpu.VMEM((tm, d), BF16)],
        compiler_params=_params(("parallel", "arbitrary")),
        name="ffn",
    )(x, pre_g, wg, wu, wd, post_g)


def _proj_kernel(x_ref, g_ref, w_ref, o_ref, xn_ref):
    @pl.when(pl.program_id(1) == 0)
    def _():
        xn_ref[...] = _rms(x_ref[...], g_ref[...]).astype(BF16)

    res = _dot(xn_ref[...], w_ref[...])
    for s in range(o_ref.shape[0]):
        o_ref[s] = res[:, s * SLAB:(s + 1) * SLAB].astype(o_ref.dtype)


def _proj(x, g, w, out_dtype, tm, slabs_per_step):
    r, d = x.shape
    n_slabs = w.shape[1] // SLAB
    tn = slabs_per_step * SLAB
    return pl.pallas_call(
        _proj_kernel,
        out_shape=jax.ShapeDtypeStruct((n_slabs, r, SLAB), out_dtype),
        grid=(r // tm, n_slabs // slabs_per_step),
        in_specs=[
            pl.BlockSpec((tm, d), lambda i, j: (i, 0)),
            pl.BlockSpec((1, d), lambda i, j: (0, 0)),
            pl.BlockSpec((d, tn), lambda i, j: (0, j)),
        ],
        out_specs=pl.BlockSpec((slabs_per_step, tm, SLAB), lambda i, j: (j, i, 0)),
        scratch_shapes=[pltpu.VMEM((tm, d), BF16)],
        compiler_params=_params(("parallel", "arbitrary")),
        name="proj",
    )(x, g, w)


IDX_TQ = 256
IDX_CW = 256
BF16_KEY_MIN, BF16_KEY_MAX = -(2 ** 15), 2 ** 15 - 1
STAGE2_BITS = 18


def _key32(pattern):
    return jnp.where(pattern < 0, pattern ^ 0x7FFFFFFF, pattern)


def _pattern_of_key16(k16):
    return lax.shift_left(jnp.where(k16 < 0, k16 ^ 0x7FFF, k16), 16)


def _idx_kernel(iq_ref, sm_ref, ik_ref, ikm_ref, bias_ref, sc_ref, sc16_ref, w_ref, *, seq, topk):
    qi = pl.program_id(1)
    tq, cw = IDX_TQ, IDX_CW
    nt = seq // cw
    n_x = qi + 1
    kf = float(topk)

    w_ref[...] = sm_ref[0][:, 0:LANES].T
    key_pos = lax.broadcasted_iota(I32, (cw, tq), 0)
    q_pos = qi * tq + lax.broadcasted_iota(I32, (cw, tq), 1)

    def score_tile(k2):
        lhs = jnp.concatenate([k2[:, 0:LANES], k2[:, LANES:2 * LANES]], axis=0)
        acc = jnp.zeros((cw, tq), F32)
        for p in range(IDX_HEADS // 2):
            rhs = iq_ref[p // 2][:, (p % 2) * LANES:(p % 2 + 1) * LANES]
            l2 = _dot_nt(lhs, rhs)
            acc = acc + jnp.maximum(l2[0:cw], 0.0) * w_ref[2 * p:2 * p + 1, :]
            acc = acc + jnp.maximum(l2[cw:2 * cw], 0.0) * w_ref[2 * p + 1:2 * p + 2, :]
        return acc

    def x_tile(c):
        k2 = ik_ref[0, pl.ds(pl.multiple_of(c * cw, cw), cw), :]
        acc = jnp.where(c * cw + key_pos <= q_pos, score_tile(k2), -jnp.inf)
        sc_ref[c + 1] = acc
        sc16_ref[c + 1] = acc.astype(BF16)

    def x_tile_pair(p, carry):
        x_tile(2 * p)
        x_tile(2 * p + 1)
        return carry

    lax.fori_loop(0, (n_x + 1) // 2, x_tile_pair, 0)
    acc_m = jnp.where(key_pos < N_META, score_tile(ikm_ref[...]), -jnp.inf)
    sc_ref[0] = acc_m
    sc16_ref[0] = acc_m.astype(BF16)
    sc_ref[n_x + 1] = jnp.full((cw, tq), -jnp.inf, F32)
    sc16_ref[n_x + 1] = jnp.full((cw, tq), -jnp.inf, BF16)
    n_pairs = (n_x + 2) // 2

    one16, zero16 = jnp.ones((cw, tq), BF16), jnp.zeros((cw, tq), BF16)

    def count16(c16):
        def tile_count(c):
            hit = jnp.where(sc16_ref[c] >= c16, one16, zero16).reshape(cw // 16, 16, tq)
            parts = [hit[i] for i in range(cw // 16)]
            while len(parts) > 1:
                parts = [parts[i] + parts[i + 1] for i in range(0, len(parts), 2)]
            return parts[0].astype(F32)

        cnt = lax.fori_loop(0, n_pairs, lambda p, a: a + (tile_count(2 * p) + tile_count(2 * p + 1)),
                            jnp.zeros((16, tq), F32))
        return jnp.sum(cnt, axis=0, keepdims=True)

    def stage1(it, k16):
        cand = k16 + lax.shift_left(jnp.int32(1), 15 - it)
        c32 = lax.bitcast_convert_type(_pattern_of_key16(cand), F32)
        c16 = jnp.broadcast_to(c32, (cw, tq)).astype(BF16)
        return jnp.where(count16(c16) >= kf, cand, k16)

    k16 = lax.fori_loop(0, 16, stage1, jnp.full((1, tq), BF16_KEY_MIN, I32))

    near_zero = jnp.logical_and(k16 >= -2, k16 <= 1)
    reach = jnp.where(near_zero, 2, 1)
    lo = _key32(_pattern_of_key16(jnp.maximum(k16 - reach, BF16_KEY_MIN)))
    hi = _key32(_pattern_of_key16(jnp.minimum(k16 + reach, BF16_KEY_MAX)))
    first_bit = jnp.where(jnp.max(jnp.where(near_zero, 1.0, 0.0)) > 0.5, 0, 1)

    def count32(cf):
        def tile_count(c):
            hit = jnp.where(sc_ref[c] >= cf, 1.0, 0.0)
            return jnp.sum(hit.reshape(cw // 8, 8, tq), axis=0)

        cnt = lax.fori_loop(0, n_pairs, lambda p, a: a + (tile_count(2 * p) + tile_count(2 * p + 1)),
                            jnp.zeros((8, tq), F32))
        return jnp.sum(cnt, axis=0, keepdims=True)

    short = (qi * tq + lax.broadcasted_iota(I32, (1, tq), 1) + 1 + N_META) <= topk

    def unsettled(cnt_t):
        return (jnp.max(jnp.where(short | (cnt_t == kf), 0.0, 1.0)) > 0.5).astype(I32)

    def stage2_cond(st):
        it, _, _, go = st
        return jnp.logical_and(it < STAGE2_BITS, go > 0)

    def stage2(st):
        it, t, cnt_t, _ = st
        cand = t + lax.shift_left(jnp.int32(1), STAGE2_BITS - 1 - it)
        cnt = count32(lax.bitcast_convert_type(jnp.where(cand < 0, cand ^ 0x7FFFFFFF, cand), F32))
        ok = jnp.logical_and(cand < hi, cnt >= kf)
        cnt_t = jnp.where(ok, cnt, cnt_t)
        return it + 1, jnp.where(ok, cand, t), cnt_t, unsettled(cnt_t)

    _, t, cnt_t, tied_any = lax.while_loop(stage2_cond, stage2,
                                           (first_bit.astype(I32), lo, jnp.full((1, tq), -1.0, F32), jnp.int32(1)))
    thr = lax.bitcast_convert_type(jnp.where(t < 0, t ^ 0x7FFFFFFF, t), F32)
    thr = jnp.where(short, jnp.finfo(F32).min, thr)

    def write_bias(bias_of):
        for c in range(nt):
            @pl.when(c < n_x)
            def _(c=c):
                bias_ref[0, c * cw:(c + 1) * cw, :] = bias_of(
                    sc_ref[c + 1], N_META + c * cw + key_pos).astype(BF16)

            @pl.when(c >= n_x)
            def _(c=c):
                bias_ref[0, c * cw:(c + 1) * cw, :] = jnp.full((cw, tq), NEG_BIAS, BF16)

        meta_pos = lax.broadcasted_iota(I32, (LANES, tq), 0)
        bias_ref[0, seq:seq + LANES, :] = bias_of(sc_ref[0, 0:LANES, :], meta_pos).astype(BF16)

    @pl.when(tied_any == 0)
    def _():
        write_bias(lambda s, pos: jnp.where(s >= thr, 0.0, NEG_BIAS))

    @pl.when(tied_any > 0)
    def _():
        def count_where(hit_of):
            def tile_count(slot):
                pos = key_pos + jnp.where(slot == 0, 0, N_META + (slot - 1) * cw)
                return jnp.sum(hit_of(sc_ref[slot], pos).reshape(cw // 8, 8, tq), axis=0)

            cnt = lax.fori_loop(0, 2 * n_pairs, lambda s_, a: a + tile_count(s_), jnp.zeros((8, tq), F32))
            return jnp.sum(cnt, axis=0, keepdims=True)

        need = kf - count_where(lambda s, pos: jnp.where(s > thr, 1.0, 0.0))
        pos_bits = int(seq + N_META).bit_length()

        def pos_bit(i, cut):
            cand = cut | lax.shift_left(jnp.int32(1), pos_bits - 1 - i)
            below = count_where(lambda s, pos: jnp.where(s == thr, jnp.where(pos < cand, 1.0, 0.0), 0.0))
            return jnp.where(below < need, cand, cut)

        cut = lax.fori_loop(0, pos_bits, pos_bit, jnp.zeros((1, tq), I32))
        tied = jnp.logical_and(jnp.logical_not(short), cnt_t != kf)
        cut = jnp.where(tied, cut, jnp.iinfo(I32).max)
        write_bias(lambda s, pos: jnp.where(
            s > thr, 0.0, jnp.where(s == thr, jnp.where(pos <= cut, 0.0, NEG_BIAS), NEG_BIAS)))


def _idx(p16, p32, ik2_meta, batch, seq, topk):
    nq = seq // IDX_TQ
    nt = seq // IDX_CW
    return pl.pallas_call(
        functools.partial(_idx_kernel, seq=seq, topk=topk),
        out_shape=jax.ShapeDtypeStruct((batch, seq + LANES, seq), BF16),
        grid=(batch, nq),
        in_specs=[
            pl.BlockSpec((4, IDX_TQ, SLAB), lambda b, q: (P16_IQ, b * nq + q, 0)),
            pl.BlockSpec((1, IDX_TQ, SLAB), lambda b, q: (P32_SMALL_SLAB, b * nq + q, 0)),
            pl.BlockSpec((1, seq, SLAB), lambda b, q: (P16_IK2_SLAB, b, 0)),
            pl.BlockSpec((IDX_CW, SLAB), lambda b, q: (0, 0)),
        ],
        out_specs=pl.BlockSpec((1, seq + LANES, IDX_TQ), lambda b, q: (b, 0, q)),
        scratch_shapes=[
            pltpu.VMEM((nt + 2, IDX_CW, IDX_TQ), F32),
            pltpu.VMEM((nt + 2, IDX_CW, IDX_TQ), BF16),
            pltpu.VMEM((LANES, IDX_TQ), F32),
        ],
        compiler_params=_params(("parallel", "arbitrary")),
        name="idx",
    )(p16, p32, p16, ik2_meta)


ATTN_AHEAD = 2


def _attn_kernel(qt_ref, kt_ref, q_ref, k_ref, vt_ref, b_ref, bm_ref, km_ref, vtm_ref, o_ref,
                 m_ref, l_ref, acc_ref, *, tq, tk):
    p = pl.program_id(1)
    qi, ki = qt_ref[p], kt_ref[p]

    def head(ref, h):
        return ref[h // 2][:, (h % 2) * LANES:(h % 2 + 1) * LANES]

    def accumulate(kref, vtref, bias):
        def scores(h):
            return _dot_nt(head(kref, h), head(q_ref, h)) + bias

        pending = [scores(h) for h in range(ATTN_AHEAD)]
        for h in range(DSA_HEADS):
            s = pending.pop(0)
            if h + ATTN_AHEAD < DSA_HEADS:
                pending.append(scores(h + ATTN_AHEAD))
            m_prev = m_ref[h]
            m_new = jnp.maximum(m_prev, jnp.max(s, axis=0, keepdims=True))
            alpha = jnp.exp2(m_prev - m_new)
            pr = jnp.exp2(s - m_new)
            l_ref[h] = alpha * l_ref[h] + jnp.sum(pr, axis=0, keepdims=True)
            vt = vtref[h * DSA_HEAD_DIM:(h + 1) * DSA_HEAD_DIM, :]
            acc_ref[h] = alpha * acc_ref[h] + _dot(vt, pr.astype(BF16))
            m_ref[h] = m_new

    @pl.when(ki == 0)
    def _():
        m_ref[...] = jnp.full(m_ref.shape, -jnp.inf, F32)
        l_ref[...] = jnp.zeros_like(l_ref)
        acc_ref[...] = jnp.zeros_like(acc_ref)
        accumulate(km_ref, vtm_ref, bm_ref[0].astype(F32))

    accumulate(k_ref, vt_ref, b_ref[0].astype(F32))

    @pl.when(ki == (qi * tq + tq - 1) // tk)
    def _():
        for h in range(DSA_HEADS):
            hs = slice(h * DSA_HEAD_DIM, (h + 1) * DSA_HEAD_DIM)
            o_ref[:, hs] = (acc_ref[h] / l_ref[h]).T.astype(o_ref.dtype)


def _attn(p16, vt, bias, p16_meta, vt_meta, batch, seq, tq, tk):
    nq, nk = seq // tq, seq // tk
    pairs = [(q, k) for q in range(nq) for k in range((q * tq + tq - 1) // tk + 1)]
    q_tab = jnp.asarray([q for q, _ in pairs], I32)
    k_tab = jnp.asarray([k for _, k in pairs], I32)
    grid_spec = pltpu.PrefetchScalarGridSpec(
        num_scalar_prefetch=2,
        grid=(batch, len(pairs)),
        in_specs=[
            pl.BlockSpec((4, tq, SLAB), lambda b, p, qt, kt: (P16_DQ, b * nq + qt[p], 0)),
            pl.BlockSpec((4, tk, SLAB), lambda b, p, qt, kt: (P16_DK, b * nk + kt[p], 0)),
            pl.BlockSpec((DSA_WIDTH, tk), lambda b, p, qt, kt: (0, b * nk + kt[p])),
            pl.BlockSpec((1, tk, tq), lambda b, p, qt, kt: (b, kt[p], qt[p])),
            pl.BlockSpec((1, LANES, tq), lambda b, p, qt, kt: (b, seq // LANES, qt[p])),
            pl.BlockSpec((4, LANES, SLAB), lambda b, p, qt, kt: (P16_DK, 0, 0)),
            pl.BlockSpec((DSA_WIDTH, LANES), lambda b, p, qt, kt: (0, 0)),
        ],
        out_specs=pl.BlockSpec((tq, DSA_WIDTH), lambda b, p, qt, kt: (b * nq + qt[p], 0)),
        scratch_shapes=[
            pltpu.VMEM((DSA_HEADS, 1, tq), F32),
            pltpu.VMEM((DSA_HEADS, 1, tq), F32),
            pltpu.VMEM((DSA_HEADS, DSA_HEAD_DIM, tq), F32),
        ],
    )
    return pl.pallas_call(
        functools.partial(_attn_kernel, tq=tq, tk=tk),
        out_shape=jax.ShapeDtypeStruct((batch * seq, DSA_WIDTH), BF16),
        grid_spec=grid_spec,
        compiler_params=_params(("parallel", "arbitrary")),
        name="attn",
    )(q_tab, k_tab, p16, p16, vt, bias, bias, p16_meta, vt_meta)


def _log_sigmoid(x):
    return jnp.minimum(x, 0.0) - jnp.log1p(jnp.exp(-jnp.abs(x)))


def _gla_kernel(gq_ref, gk_ref, go_ref, sm_ref, gv_ref, mk_ref, mv_ref, msm_ref, wgate_ref, bgate_ref,
                ng_ref, o_ref, st_ref, bc_ref):
    c = GLA_CHUNK

    def log_decay(small):
        return _log_sigmoid(_dot_f32(small, wgate_ref[...]) + bgate_ref[...]) / GLA_TAU

    def lower_ones(n):
        r = lax.broadcasted_iota(I32, (n, n), 0)
        cc = lax.broadcasted_iota(I32, (n, n), 1)
        return r >= cc

    @pl.when(pl.program_id(1) == 0)
    def _():
        bcm = _dot_exact_lhs01(jnp.where(lower_ones(N_META), 1.0, 0.0).astype(BF16),
                               log_decay(msm_ref[0]))
        for h in range(GLA_HEADS):
            ks = slice(h * GLA_DK, (h + 1) * GLA_DK)
            kd = mk_ref[h // 2][:, (h % 2) * GLA_DK:(h % 2 + 1) * GLA_DK] * jnp.exp(
                bcm[N_META - 1:N_META, ks] - bcm[:, ks])
            st_ref[h] = _dot_tn(mv_ref[h], kd.astype(BF16))

    rows = bc_ref.shape[0]
    r_i = lax.broadcasted_iota(I32, (rows, rows), 0)
    c_i = lax.broadcasted_iota(I32, (rows, rows), 1)
    chunk_tri = jnp.logical_and(r_i >= c_i, r_i // c == c_i // c)
    bc_ref[...] = _dot_exact_lhs01(jnp.where(chunk_tri, 1.0, 0.0).astype(BF16), log_decay(sm_ref[0]))
    causal = lower_ones(c)
    sub = GLA_SUB
    lane = lax.broadcasted_iota(I32, (sub, c), 1)

    for ch in range(rows // c):
        base = ch * c
        for h in range(GLA_HEADS):
            ks = slice(h * GLA_DK, (h + 1) * GLA_DK)
            hsl = slice((h % 2) * GLA_DK, (h % 2 + 1) * GLA_DK)
            q = gq_ref[h // 2][base:base + c, hsl] * (GLA_DK ** -0.5)
            k = gk_ref[h // 2][base:base + c, hsl]
            v = gv_ref[h][base:base + c]
            bc = bc_ref[base:base + c, ks]

            blocks = []
            for s0 in range(0, c, sub):
                q_s, bc_s = q[s0:s0 + sub], bc[s0:s0 + sub]
                diag = jnp.zeros((sub, c), F32)
                for j in range(s0, s0 + sub):
                    e = jnp.exp(jnp.minimum(bc_s - bc_ref[base + j:base + j + 1, ks], 0.0))
                    k_j = gk_ref[h // 2][base + j:base + j + 1, hsl]
                    diag = jnp.where(lane == j, jnp.sum(q_s * k_j * e, axis=1, keepdims=True), diag)
                if s0:
                    b_edge = bc_ref[base + s0 - 1:base + s0, ks]
                    q_t = q_s * jnp.exp(bc_s - b_edge)
                    k_t = k * jnp.exp(jnp.minimum(b_edge - bc, 0.0))
                    diag = jnp.where(lane < s0, _dot_nt(q_t.astype(BF16), k_t.astype(BF16)), diag)
                blocks.append(diag)
            attn = jnp.where(causal, jnp.concatenate(blocks, axis=0), 0.0)

            state = st_ref[h]
            b_last = bc[c - 1:c, :]
            o = _dot_nt((q * jnp.exp(bc)).astype(BF16), state.astype(BF16))
            o = o + _dot(attn.astype(BF16), v)
            kd = k * jnp.exp(b_last - bc)
            st_ref[h] = state * jnp.exp(b_last) + _dot_tn(v, kd.astype(BF16))

            g = go_ref[h][base:base + c]
            y = _rms(o, ng_ref[...]) * (g * jax.nn.sigmoid(g))
            o_ref[base:base + c, h * GLA_DV:(h + 1) * GLA_DV] = y.astype(o_ref.dtype)


def _gla(p32, p16, p32_meta, p16_meta, w_gate_pad, b_gate, norm_g, batch, seq):
    c = _tile(seq, GLA_ROWS)
    ns = seq // c
    row = lambda b, i: b * ns + i
    return pl.pallas_call(
        _gla_kernel,
        out_shape=jax.ShapeDtypeStruct((batch * seq, GLA_WIDTH), BF16),
        grid=(batch, ns),
        in_specs=[
            pl.BlockSpec((2, c, SLAB), lambda b, i: (P32_GQ_SLAB // 2, row(b, i), 0)),
            pl.BlockSpec((2, c, SLAB), lambda b, i: (P32_GK_SLAB // 2, row(b, i), 0)),
            pl.BlockSpec((4, c, SLAB), lambda b, i: (P32_GOUT_SLAB // 4, row(b, i), 0)),
            pl.BlockSpec((1, c, SLAB), lambda b, i: (P32_SMALL_SLAB, row(b, i), 0)),
            pl.BlockSpec((4, c, SLAB), lambda b, i: (P16_GV, row(b, i), 0)),
            pl.BlockSpec((2, N_META, SLAB), lambda b, i: (P32_GK_SLAB // 2, 0, 0)),
            pl.BlockSpec((4, N_META, SLAB), lambda b, i: (P16_GV, 0, 0)),
            pl.BlockSpec((1, N_META, SLAB), lambda b, i: (P32_SMALL_SLAB, 0, 0)),
            pl.BlockSpec((SLAB, GLA_KEY_WIDTH), lambda b, i: (0, 0)),
            pl.BlockSpec((1, GLA_KEY_WIDTH), lambda b, i: (0, 0)),
            pl.BlockSpec((1, GLA_DV), lambda b, i: (0, 0)),
        ],
        out_specs=pl.BlockSpec((c, GLA_WIDTH), lambda b, i: (row(b, i), 0)),
        scratch_shapes=[
            pltpu.VMEM((GLA_HEADS, GLA_DV, GLA_DK), F32),
            pltpu.VMEM((c, GLA_KEY_WIDTH), F32),
        ],
        compiler_params=_params(("parallel", "arbitrary")),
        name="gla",
    )(p32, p32, p32, p32, p16, p32_meta, p16_meta, p32_meta, w_gate_pad, b_gate, norm_g)


def _out_kernel(h_ref, a_ref, g_ref, wa_ref, wg_ref, pg_ref, o_ref):
    m = _dot(a_ref[...], wa_ref[...]) + _dot(g_ref[...], wg_ref[...])
    o_ref[...] = h_ref[...] + _rms(m, pg_ref[...])


def _out(hs, o_dsa, o_gla, w_dsa, w_gla, post_g, tm):
    r, d = hs.shape
    return pl.pallas_call(
        _out_kernel,
        out_shape=jax.ShapeDtypeStruct((r, d), F32),
        grid=(r // tm,),
        in_specs=[
            pl.BlockSpec((tm, d), lambda i: (i, 0)),
            pl.BlockSpec((tm, DSA_WIDTH), lambda i: (i, 0)),
            pl.BlockSpec((tm, GLA_WIDTH), lambda i: (i, 0)),
            pl.BlockSpec((DSA_WIDTH, d), lambda i: (0, 0)),
            pl.BlockSpec((GLA_WIDTH, d), lambda i: (0, 0)),
            pl.BlockSpec((1, d), lambda i: (0, 0)),
        ],
        out_specs=pl.BlockSpec((tm, d), lambda i: (i, 0)),
        compiler_params=_params(("parallel",)),
        name="outproj",
    )(hs, o_dsa, o_gla, w_dsa, w_gla, post_g)


def _tile(n, pref):
    t = min(n, pref)
    while n % t:
        t //= 2
    return t


def _proj_weights(w_in):
    d = w_in.shape[0]
    offs = np.cumsum((0,) + IN_SPLITS)
    dq, dk, dv, iq, ik, iw, gq, gk, gv, glow, gout = (w_in[:, offs[i]:offs[i + 1]] for i in range(11))
    z = lambda n: jnp.zeros((d, n), w_in.dtype)
    ik2 = jnp.concatenate([ik, z(IDX_DIM), z(IDX_DIM), ik], axis=1)
    q_scale = DSA_HEAD_DIM ** -0.5 * float(np.log2(np.e))
    w16 = jnp.concatenate([dq * q_scale, dk, dv, iq * (IDX_DIM ** -0.5), gv, ik2], axis=1)
    small = jnp.concatenate([iw * (IDX_HEADS ** -0.5), glow, z(SLAB - IDX_HEADS - GLA_GATE_RANK)], axis=1)
    w32 = jnp.concatenate([gq, gk, gout, small], axis=1)
    return w16.astype(BF16), w32.astype(BF16)


def kernel(x, meta_tokens, ffn1_pre_g, ffn1_w_gate, ffn1_w_up, ffn1_w_down, ffn1_post_g, mix_pre_g, w_in, w_gate_up, b_gate, gla_norm_g, w_out, mix_post_g, ffn2_pre_g, ffn2_w_gate, ffn2_w_up, ffn2_w_down, ffn2_post_g):
    batch, seq, d = x.shape
    depth = w_in.shape[0]
    assert depth == 1, "the meta rows skip the mixer, which is only valid for the last layer"
    assert seq % (2 * IDX_CW) == 0 and meta_tokens.shape[0] == N_META, "score tiles are visited in pairs"
    d_ff = ffn1_w_gate.shape[-1]
    topk = min(TOPK_MAX, seq // 4)
    rows = batch * seq
    tm = _tile(rows, 512)
    tf = _tile(d_ff, 512)
    tq = _tile(seq, 256)
    tk = _tile(seq, 1024)

    hs = x.reshape(rows, d)
    hm = meta_tokens.astype(x.dtype)
    row2 = lambda v: v.reshape(1, -1)
    for l in range(depth):
        ffn1 = (row2(ffn1_pre_g[l]), ffn1_w_gate[l].astype(BF16), ffn1_w_up[l].astype(BF16),
                ffn1_w_down[l].astype(BF16), row2(ffn1_post_g[l]))
        hs = _ffn(hs, *ffn1, tm=tm, tf=tf)
        hm = _ffn(hm, *ffn1, tm=N_META, tf=tf)

        w16, w32 = _proj_weights(w_in[l])
        pre = row2(mix_pre_g[l])
        proj16 = functools.partial(_proj, g=pre, w=w16, out_dtype=BF16, slabs_per_step=P16_SLABS_PER_STEP)
        proj32 = functools.partial(_proj, g=pre, w=w32, out_dtype=F32, slabs_per_step=P32_SLABS_PER_STEP)
        tm_proj = _tile(rows, 1024)
        p16, p32 = proj16(hs, tm=tm_proj), proj32(hs, tm=tm_proj)
        p16m, p32m = proj16(hm, tm=N_META), proj32(hm, tm=N_META)

        pad_rows = lambda a, n: jnp.pad(a, ((0, 0), (0, n - a.shape[1]), (0, 0)))
        bias = _idx(p16, p32, pad_rows(p16m, IDX_CW)[P16_IK2_SLAB], batch, seq, topk)
        p16m_pad = pad_rows(p16m, LANES)
        v_t = lambda a: jnp.transpose(a[4 * P16_DV:4 * P16_DV + 4], (0, 2, 1)).reshape(DSA_WIDTH, -1)
        o_dsa = _attn(p16, v_t(p16), bias, p16m_pad, v_t(p16m_pad), batch, seq, tq, tk)

        w_gate_pad = jnp.zeros((SLAB, GLA_KEY_WIDTH), F32).at[
            IDX_HEADS:IDX_HEADS + GLA_GATE_RANK].set(w_gate_up[l])
        o_gla = _gla(p32, p16, p32m, p16m, w_gate_pad, row2(b_gate[l]), row2(gla_norm_g[l]), batch, seq)

        wo = w_out[l].astype(BF16)
        hs = _out(hs, o_dsa, o_gla, wo[:DSA_WIDTH], wo[DSA_WIDTH:], row2(mix_post_g[l]), tm)
        ffn2 = (row2(ffn2_pre_g[l]), ffn2_w_gate[l].astype(BF16), ffn2_w_up[l].astype(BF16),
                ffn2_w_down[l].astype(BF16), row2(ffn2_post_g[l]))
        hs = _ffn(hs, *ffn2, tm=tm, tf=tf)
    return hs.reshape(batch, seq, d)
```

```python
import functools

import numpy as np
import jax
import jax.numpy as jnp
from jax import lax
from jax.experimental import pallas as pl
from jax.experimental.pallas import tpu as pltpu

F32, BF16, I32 = jnp.float32, jnp.bfloat16, jnp.int32

N_META = 16
FFN_RES = 0.5
EPS = 1e-6
DSA_HEADS = 8
DSA_HEAD_DIM = 128
DSA_WIDTH = DSA_HEADS * DSA_HEAD_DIM
IDX_HEADS = 16
IDX_DIM = 64
TOPK_MAX = 256
GLA_HEADS = 4
GLA_DK = 128
GLA_DV = 256
GLA_KEY_WIDTH = GLA_HEADS * GLA_DK
GLA_WIDTH = GLA_HEADS * GLA_DV
GLA_GATE_RANK = 16
GLA_TAU = 16.0
GLA_CHUNK = 64
GLA_SUB = 16
GLA_ROWS = 256
IN_SPLITS = (DSA_WIDTH, DSA_WIDTH, DSA_WIDTH, IDX_HEADS * IDX_DIM, IDX_DIM, IDX_HEADS,
             GLA_KEY_WIDTH, GLA_KEY_WIDTH, GLA_WIDTH, GLA_GATE_RANK, GLA_WIDTH)

LANES = 128
SLAB = 256
VMEM_LIMIT_BYTES = 56 * 1024 * 1024

P16_DQ, P16_DK, P16_DV, P16_IQ, P16_GV, P16_IK2_SLAB, P16_SLABS = 0, 1, 2, 3, 4, 20, 21
P32_GQ_SLAB, P32_GK_SLAB, P32_GOUT_SLAB, P32_SMALL_SLAB, P32_SLABS = 0, 2, 4, 8, 9
P16_SLABS_PER_STEP, P32_SLABS_PER_STEP = 7, 3

NEG_BIAS = -1e30


def _params(sem, **flags):
    return pltpu.CompilerParams(dimension_semantics=sem, vmem_limit_bytes=VMEM_LIMIT_BYTES,
                                flags=flags or None)


def _dot(a, b):
    return jnp.dot(a, b, preferred_element_type=F32)


def _dot_nt(a, b):
    return lax.dot_general(a, b, (((1,), (1,)), ((), ())), preferred_element_type=F32)


def _dot_tn(a, b):
    return lax.dot_general(a, b, (((0,), (0,)), ((), ())), preferred_element_type=F32)


def _split3(x):
    hi = x.astype(BF16)
    r = x - hi.astype(F32)
    mid = r.astype(BF16)
    lo = (r - mid.astype(F32)).astype(BF16)
    return hi, mid, lo


def _dot_exact_lhs01(l01, x):
    hi, mid, lo = _split3(x)
    return _dot(l01, hi) + _dot(l01, mid) + _dot(l01, lo)


def _dot_f32(a, b):
    ah, am, al = _split3(a)
    bh, bm, bl = _split3(b)
    return (_dot(ah, bh) + (_dot(ah, bm) + _dot(am, bh))
            + (_dot(ah, bl) + _dot(am, bm) + _dot(al, bh)))


def _rms(x, g):
    return x * lax.rsqrt(jnp.mean(x * x, axis=-1, keepdims=True) + EPS) * g


def _ffn_kernel(x_ref, preg_ref, wg_ref, wu_ref, wd_ref, postg_ref, o_ref, xn_ref, *, nf):
    f = pl.program_id(1)

    @pl.when(f == 0)
    def _():
        xn_ref[...] = _rms(x_ref[...], preg_ref[...]).astype(BF16)
        o_ref[...] = jnp.zeros_like(o_ref)

    xn = xn_ref[...]
    g = _dot(xn, wg_ref[...])
    u = _dot(xn, wu_ref[...])
    a = (g * jax.nn.sigmoid(g)) * u
    o_ref[...] += _dot(a.astype(BF16), wd_ref[...])

    @pl.when(f == nf - 1)
    def _():
        o_ref[...] = x_ref[...] + FFN_RES * _rms(o_ref[...], postg_ref[...])


def _ffn(x, pre_g, wg, wu, wd, post_g, tm, tf):
    r, d = x.shape
    nf = wg.shape[1] // tf
    return pl.pallas_call(
        functools.partial(_ffn_kernel, nf=nf),
        out_shape=jax.ShapeDtypeStruct((r, d), F32),
        grid=(r // tm, nf),
        in_specs=[
            pl.BlockSpec((tm, d), lambda i, f: (i, 0)),
            pl.BlockSpec((1, d), lambda i, f: (0, 0)),
            pl.BlockSpec((d, tf), lambda i, f: (0, f)),
            pl.BlockSpec((d, tf), lambda i, f: (0, f)),
            pl.BlockSpec((tf, d), lambda i, f: (f, 0)),
            pl.BlockSpec((1, d), lambda i, f: (0, 0)),
        ],
        out_specs=pl.BlockSpec((tm, d), lambda i, f: (i, 0)),
        scratch_shapes=[pltpu.VMEM((tm, d), BF16)],
        compiler_params=_params(("parallel", "arbitrary")),
        name="ffn",
    )(x, pre_g, wg, wu, wd, post_g)


def _proj_kernel(x_ref, g_ref, w_ref, o_ref, xn_ref):
    @pl.when(pl.program_id(1) == 0)
    def _():
        xn_ref[...] = _rms(x_ref[...], g_ref[...]).astype(BF16)

    res = _dot(xn_ref[...], w_ref[...])
    for s in range(o_ref.shape[0]):
        o_ref[s] = res[:, s * SLAB:(s + 1) * SLAB].astype(o_ref.dtype)


def _proj(x, g, w, out_dtype, tm, slabs_per_step):
    r, d = x.shape
    n_slabs = w.shape[1] // SLAB
    tn = slabs_per_step * SLAB
    return pl.pallas_call(
        _proj_kernel,
        out_shape=jax.ShapeDtypeStruct((n_slabs, r, SLAB), out_dtype),
        grid=(r // tm, n_slabs // slabs_per_step),
        in_specs=[
            pl.BlockSpec((tm, d), lambda i, j: (i, 0)),
            pl.BlockSpec((1, d), lambda i, j: (0, 0)),
            pl.BlockSpec((d, tn), lambda i, j: (0, j)),
        ],
        out_specs=pl.BlockSpec((slabs_per_step, tm, SLAB), lambda i, j: (j, i, 0)),
        scratch_shapes=[pltpu.VMEM((tm, d), BF16)],
        compiler_params=_params(("parallel", "arbitrary")),
        name="proj",
    )(x, g, w)


IDX_TQ = 256
IDX_CW = 256
BF16_KEY_MIN, BF16_KEY_MAX = -(2 ** 15), 2 ** 15 - 1
STAGE2_BITS = 18


def _key32(pattern):
    return jnp.where(pattern < 0, pattern ^ 0x7FFFFFFF, pattern)


def _pattern_of_key16(k16):
    return lax.shift_left(jnp.where(k16 < 0, k16 ^ 0x7FFF, k16), 16)


def _idx_kernel(iq_ref, sm_ref, ik_ref, ikm_ref, bias_ref, sc_ref, sc16_ref, w_ref, *, seq, topk):
    qi = pl.program_id(1)
    tq, cw = IDX_TQ, IDX_CW
    nt = seq // cw
    n_x = qi + 1
    kf = float(topk)

    w_ref[...] = sm_ref[0][:, 0:LANES].T
    key_pos = lax.broadcasted_iota(I32, (cw, tq), 0)
    q_pos = qi * tq + lax.broadcasted_iota(I32, (cw, tq), 1)

    def score_tile(k2):
        lhs = jnp.concatenate([k2[:, 0:LANES], k2[:, LANES:2 * LANES]], axis=0)
        acc = jnp.zeros((cw, tq), F32)
        for p in range(IDX_HEADS // 2):
            rhs = iq_ref[p // 2][:, (p % 2) * LANES:(p % 2 + 1) * LANES]
            l2 = _dot_nt(lhs, rhs)
            acc = acc + jnp.maximum(l2[0:cw], 0.0) * w_ref[2 * p:2 * p + 1, :]
            acc = acc + jnp.maximum(l2[cw:2 * cw], 0.0) * w_ref[2 * p + 1:2 * p + 2, :]
        return acc

    def x_tile(c):
        k2 = ik_ref[0, pl.ds(pl.multiple_of(c * cw, cw), cw), :]
        acc = jnp.where(c * cw + key_pos <= q_pos, score_tile(k2), -jnp.inf)
        sc_ref[c + 1] = acc
        sc16_ref[c + 1] = acc.astype(BF16)

    def x_tile_pair(p, carry):
        x_tile(2 * p)
        x_tile(2 * p + 1)
        return carry

    lax.fori_loop(0, (n_x + 1) // 2, x_tile_pair, 0)
    acc_m = jnp.where(key_pos < N_META, score_tile(ikm_ref[...]), -jnp.inf)
    sc_ref[0] = acc_m
    sc16_ref[0] = acc_m.astype(BF16)
    sc_ref[n_x + 1] = jnp.full((cw, tq), -jnp.inf, F32)
    sc16_ref[n_x + 1] = jnp.full((cw, tq), -jnp.inf, BF16)
    n_pairs = (n_x + 2) // 2

    one16, zero16 = jnp.ones((cw, tq), BF16), jnp.zeros((cw, tq), BF16)

    def count16(c16):
        def tile_count(c):
            hit = jnp.where(sc16_ref[c] >= c16, one16, zero16).reshape(cw // 16, 16, tq)
            parts = [hit[i] for i in range(cw // 16)]
            while len(parts) > 1:
                parts = [parts[i] + parts[i + 1] for i in range(0, len(parts), 2)]
            return parts[0].astype(F32)

        cnt = lax.fori_loop(0, n_pairs, lambda p, a: a + (tile_count(2 * p) + tile_count(2 * p + 1)),
                            jnp.zeros((16, tq), F32))
        return jnp.sum(cnt, axis=0, keepdims=True)

    def stage1(it, k16):
        cand = k16 + lax.shift_left(jnp.int32(1), 15 - it)
        c32 = lax.bitcast_convert_type(_pattern_of_key16(cand), F32)
        c16 = jnp.broadcast_to(c32, (cw, tq)).astype(BF16)
        return jnp.where(count16(c16) >= kf, cand, k16)

    k16 = lax.fori_loop(0, 16, stage1, jnp.full((1, tq), BF16_KEY_MIN, I32))

    near_zero = jnp.logical_and(k16 >= -2, k16 <= 1)
    reach = jnp.where(near_zero, 2, 1)
    lo = _key32(_pattern_of_key16(jnp.maximum(k16 - reach, BF16_KEY_MIN)))
    hi = _key32(_pattern_of_key16(jnp.minimum(k16 + reach, BF16_KEY_MAX)))
    first_bit = jnp.where(jnp.max(jnp.where(near_zero, 1.0, 0.0)) > 0.5, 0, 1)

    def count32(cf):
        def tile_count(c):
            hit = jnp.where(sc_ref[c] >= cf, 1.0, 0.0)
            return jnp.sum(hit.reshape(cw // 8, 8, tq), axis=0)

        cnt = lax.fori_loop(0, n_pairs, lambda p, a: a + (tile_count(2 * p) + tile_count(2 * p + 1)),
                            jnp.zeros((8, tq), F32))
        return jnp.sum(cnt, axis=0, keepdims=True)

    short = (qi * tq + lax.broadcasted_iota(I32, (1, tq), 1) + 1 + N_META) <= topk

    def unsettled(cnt_t):
        return (jnp.max(jnp.where(short | (cnt_t == kf), 0.0, 1.0)) > 0.5).astype(I32)

    def stage2_cond(st):
        it, _, _, go = st
        return jnp.logical_and(it < STAGE2_BITS, go > 0)

    def stage2(st):
        it, t, cnt_t, _ = st
        cand = t + lax.shift_left(jnp.int32(1), STAGE2_BITS - 1 - it)
        cnt = count32(lax.bitcast_convert_type(jnp.where(cand < 0, cand ^ 0x7FFFFFFF, cand), F32))
        ok = jnp.logical_and(cand < hi, cnt >= kf)
        cnt_t = jnp.where(ok, cnt, cnt_t)
        return it + 1, jnp.where(ok, cand, t), cnt_t, unsettled(cnt_t)

    _, t, cnt_t, tied_any = lax.while_loop(stage2_cond, stage2,
                                           (first_bit.astype(I32), lo, jnp.full((1, tq), -1.0, F32), jnp.int32(1)))
    thr = lax.bitcast_convert_type(jnp.where(t < 0, t ^ 0x7FFFFFFF, t), F32)
    thr = jnp.where(short, jnp.finfo(F32).min, thr)

    def write_bias(bias_of):
        for c in range(nt):
            @pl.when(c < n_x)
            def _(c=c):
                bias_ref[0, c * cw:(c + 1) * cw, :] = bias_of(
                    sc_ref[c + 1], N_META + c * cw + key_pos).astype(BF16)

            @pl.when(c >= n_x)
            def _(c=c):
                bias_ref[0, c * cw:(c + 1) * cw, :] = jnp.full((cw, tq), NEG_BIAS, BF16)

        meta_pos = lax.broadcasted_iota(I32, (LANES, tq), 0)
        bias_ref[0, seq:seq + LANES, :] = bias_of(sc_ref[0, 0:LANES, :], meta_pos).astype(BF16)

    @pl.when(tied_any == 0)
    def _():
        write_bias(lambda s, pos: jnp.where(s >= thr, 0.0, NEG_BIAS))

    @pl.when(tied_any > 0)
    def _():
        def count_where(hit_of):
            def tile_count(slot):
                pos = key_pos + jnp.where(slot == 0, 0, N_META + (slot - 1) * cw)
                return jnp.sum(hit_of(sc_ref[slot], pos).reshape(cw // 8, 8, tq), axis=0)

            cnt = lax.fori_loop(0, 2 * n_pairs, lambda s_, a: a + tile_count(s_), jnp.zeros((8, tq), F32))
            return jnp.sum(cnt, axis=0, keepdims=True)

        need = kf - count_where(lambda s, pos: jnp.where(s > thr, 1.0, 0.0))
        pos_bits = int(seq + N_META).bit_length()

        def pos_bit(i, cut):
            cand = cut | lax.shift_left(jnp.int32(1), pos_bits - 1 - i)
            below = count_where(lambda s, pos: jnp.where(s == thr, jnp.where(pos < cand, 1.0, 0.0), 0.0))
            return jnp.where(below < need, cand, cut)

        cut = lax.fori_loop(0, pos_bits, pos_bit, jnp.zeros((1, tq), I32))
        tied = jnp.logical_and(jnp.logical_not(short), cnt_t != kf)
        cut = jnp.where(tied, cut, jnp.iinfo(I32).max)
        write_bias(lambda s, pos: jnp.where(
            s > thr, 0.0, jnp.where(s == thr, jnp.where(pos <= cut, 0.0, NEG_BIAS), NEG_BIAS)))


def _idx(p16, p32, ik2_meta, batch, seq, topk):
    nq = seq // IDX_TQ
    nt = seq // IDX_CW
    return pl.pallas_call(
        functools.partial(_idx_kernel, seq=seq, topk=topk),
        out_shape=jax.ShapeDtypeStruct((batch, seq + LANES, seq), BF16),
        grid=(batch, nq),
        in_specs=[
            pl.BlockSpec((4, IDX_TQ, SLAB), lambda b, q: (P16_IQ, b * nq + q, 0)),
            pl.BlockSpec((1, IDX_TQ, SLAB), lambda b, q: (P32_SMALL_SLAB, b * nq + q, 0)),
            pl.BlockSpec((1, seq, SLAB), lambda b, q: (P16_IK2_SLAB, b, 0)),
            pl.BlockSpec((IDX_CW, SLAB), lambda b, q: (0, 0)),
        ],
        out_specs=pl.BlockSpec((1, seq + LANES, IDX_TQ), lambda b, q: (b, 0, q)),
        scratch_shapes=[
            pltpu.VMEM((nt + 2, IDX_CW, IDX_TQ), F32),
            pltpu.VMEM((nt + 2, IDX_CW, IDX_TQ), BF16),
            pltpu.VMEM((LANES, IDX_TQ), F32),
        ],
        compiler_params=_params(("parallel", "arbitrary")),
        name="idx",
    )(p16, p32, p16, ik2_meta)


ATTN_AHEAD = 2
VT_ROWS = DSA_HEAD_DIM + 16


def _attn_kernel(qt_ref, kt_ref, q_ref, k_ref, vt_ref, b_ref, bm_ref, km_ref, vtm_ref, o_ref,
                 m_ref, acc_ref, *, tq, tk):
    p = pl.program_id(1)
    qi, ki = qt_ref[p], kt_ref[p]

    def head(ref, h):
        return ref[h // 2][:, (h % 2) * LANES:(h % 2 + 1) * LANES]

    def accumulate(kref, vtref, bias):
        def scores(h):
            return _dot_nt(head(kref, h), head(q_ref, h)) + bias

        pending = [scores(h) for h in range(ATTN_AHEAD)]
        for h in range(DSA_HEADS):
            s = pending.pop(0)
            if h + ATTN_AHEAD < DSA_HEADS:
                pending.append(scores(h + ATTN_AHEAD))
            m_prev = m_ref[h]
            m_new = jnp.maximum(m_prev, jnp.max(s, axis=0, keepdims=True))
            alpha = jnp.exp2(m_prev - m_new)
            pr = jnp.exp2(s - m_new).astype(BF16)
            vt = vtref[h * VT_ROWS:(h + 1) * VT_ROWS, :]
            acc_ref[h] = alpha * acc_ref[h] + _dot(vt, pr)
            m_ref[h] = m_new

    @pl.when(ki == 0)
    def _():
        m_ref[...] = jnp.full(m_ref.shape, -jnp.inf, F32)
        acc_ref[...] = jnp.zeros_like(acc_ref)
        accumulate(km_ref, vtm_ref, bm_ref[0].astype(F32))

    accumulate(k_ref, vt_ref, b_ref[0].astype(F32))

    @pl.when(ki == (qi * tq + tq - 1) // tk)
    def _():
        for h in range(DSA_HEADS):
            hs = slice(h * DSA_HEAD_DIM, (h + 1) * DSA_HEAD_DIM)
            acc = acc_ref[h]
            o_ref[:, hs] = (acc[0:DSA_HEAD_DIM] / acc[DSA_HEAD_DIM:DSA_HEAD_DIM + 1]).T.astype(o_ref.dtype)


def _attn(p16, vt, bias, p16_meta, vt_meta, batch, seq, tq, tk):
    nq, nk = seq // tq, seq // tk
    pairs = [(q, k) for q in range(nq) for k in range((q * tq + tq - 1) // tk + 1)]
    q_tab = jnp.asarray([q for q, _ in pairs], I32)
    k_tab = jnp.asarray([k for _, k in pairs], I32)
    grid_spec = pltpu.PrefetchScalarGridSpec(
        num_scalar_prefetch=2,
        grid=(batch, len(pairs)),
        in_specs=[
            pl.BlockSpec((4, tq, SLAB), lambda b, p, qt, kt: (P16_DQ, b * nq + qt[p], 0)),
            pl.BlockSpec((4, tk, SLAB), lambda b, p, qt, kt: (P16_DK, b * nk + kt[p], 0)),
            pl.BlockSpec((DSA_HEADS * VT_ROWS, tk), lambda b, p, qt, kt: (0, b * nk + kt[p])),
            pl.BlockSpec((1, tk, tq), lambda b, p, qt, kt: (b, kt[p], qt[p])),
            pl.BlockSpec((1, LANES, tq), lambda b, p, qt, kt: (b, seq // LANES, qt[p])),
            pl.BlockSpec((4, LANES, SLAB), lambda b, p, qt, kt: (P16_DK, 0, 0)),
            pl.BlockSpec((DSA_HEADS * VT_ROWS, LANES), lambda b, p, qt, kt: (0, 0)),
        ],
        out_specs=pl.BlockSpec((tq, DSA_WIDTH), lambda b, p, qt, kt: (b * nq + qt[p], 0)),
        scratch_shapes=[
            pltpu.VMEM((DSA_HEADS, 1, tq), F32),
            pltpu.VMEM((DSA_HEADS, VT_ROWS, tq), F32),
        ],
    )
    return pl.pallas_call(
        functools.partial(_attn_kernel, tq=tq, tk=tk),
        out_shape=jax.ShapeDtypeStruct((batch * seq, DSA_WIDTH), BF16),
        grid_spec=grid_spec,
        compiler_params=_params(("parallel", "arbitrary")),
        name="attn",
    )(q_tab, k_tab, p16, p16, vt, bias, bias, p16_meta, vt_meta)


def _log_sigmoid(x):
    return jnp.minimum(x, 0.0) - jnp.log1p(jnp.exp(-jnp.abs(x)))


def _gla_kernel(gq_ref, gk_ref, go_ref, sm_ref, gv_ref, mk_ref, mv_ref, msm_ref, wgate_ref, bgate_ref,
                ng_ref, o_ref, st_ref, bc_ref):
    c = GLA_CHUNK

    def log_decay(small):
        return _log_sigmoid(_dot_f32(small, wgate_ref[...]) + bgate_ref[...]) / GLA_TAU

    def lower_ones(n):
        r = lax.broadcasted_iota(I32, (n, n), 0)
        cc = lax.broadcasted_iota(I32, (n, n), 1)
        return r >= cc

    @pl.when(pl.program_id(1) == 0)
    def _():
        bcm = _dot_exact_lhs01(jnp.where(lower_ones(N_META), 1.0, 0.0).astype(BF16),
                               log_decay(msm_ref[0]))
        for h in range(GLA_HEADS):
            ks = slice(h * GLA_DK, (h + 1) * GLA_DK)
            kd = mk_ref[h // 2][:, (h % 2) * GLA_DK:(h % 2 + 1) * GLA_DK] * jnp.exp(
                bcm[N_META - 1:N_META, ks] - bcm[:, ks])
            st_ref[h] = _dot_tn(mv_ref[h], kd.astype(BF16))

    rows = bc_ref.shape[0]
    r_i = lax.broadcasted_iota(I32, (rows, rows), 0)
    c_i = lax.broadcasted_iota(I32, (rows, rows), 1)
    chunk_tri = jnp.logical_and(r_i >= c_i, r_i // c == c_i // c)
    bc_ref[...] = _dot_exact_lhs01(jnp.where(chunk_tri, 1.0, 0.0).astype(BF16), log_decay(sm_ref[0]))
    causal = lower_ones(c)
    sub = GLA_SUB
    lane = lax.broadcasted_iota(I32, (sub, c), 1)

    for ch in range(rows // c):
        base = ch * c
        for h in range(GLA_HEADS):
            ks = slice(h * GLA_DK, (h + 1) * GLA_DK)
            hsl = slice((h % 2) * GLA_DK, (h % 2 + 1) * GLA_DK)
            q = gq_ref[h // 2][base:base + c, hsl] * (GLA_DK ** -0.5)
            k = gk_ref[h // 2][base:base + c, hsl]
            v = gv_ref[h][base:base + c]
            bc = bc_ref[base:base + c, ks]

            blocks = []
            for s0 in range(0, c, sub):
                q_s, bc_s = q[s0:s0 + sub], bc[s0:s0 + sub]
                diag = jnp.zeros((sub, c), F32)
                for j in range(s0, s0 + sub):
                    e = jnp.exp(jnp.minimum(bc_s - bc_ref[base + j:base + j + 1, ks], 0.0))
                    k_j = gk_ref[h // 2][base + j:base + j + 1, hsl]
                    diag = jnp.where(lane == j, jnp.sum(q_s * k_j * e, axis=1, keepdims=True), diag)
                if s0:
                    b_edge = bc_ref[base + s0 - 1:base + s0, ks]
                    q_t = q_s * jnp.exp(bc_s - b_edge)
                    k_t = k * jnp.exp(jnp.minimum(b_edge - bc, 0.0))
                    diag = jnp.where(lane < s0, _dot_nt(q_t.astype(BF16), k_t.astype(BF16)), diag)
                blocks.append(diag)
            attn = jnp.where(causal, jnp.concatenate(blocks, axis=0), 0.0)

            state = st_ref[h]
            b_last = bc[c - 1:c, :]
            o = _dot_nt((q * jnp.exp(bc)).astype(BF16), state.astype(BF16))
            o = o + _dot(attn.astype(BF16), v)
            kd = k * jnp.exp(b_last - bc)
            st_ref[h] = state * jnp.exp(b_last) + _dot_tn(v, kd.astype(BF16))

            g = go_ref[h][base:base + c]
            y = _rms(o, ng_ref[...]) * (g * jax.nn.sigmoid(g))
            o_ref[base:base + c, h * GLA_DV:(h + 1) * GLA_DV] = y.astype(o_ref.dtype)


def _gla(p32, p16, p32_meta, p16_meta, w_gate_pad, b_gate, norm_g, batch, seq):
    c = _tile(seq, GLA_ROWS)
    ns = seq // c
    row = lambda b, i: b * ns + i
    return pl.pallas_call(
        _gla_kernel,
        out_shape=jax.ShapeDtypeStruct((batch * seq, GLA_WIDTH), BF16),
        grid=(batch, ns),
        in_specs=[
            pl.BlockSpec((2, c, SLAB), lambda b, i: (P32_GQ_SLAB // 2, row(b, i), 0)),
            pl.BlockSpec((2, c, SLAB), lambda b, i: (P32_GK_SLAB // 2, row(b, i), 0)),
            pl.BlockSpec((4, c, SLAB), lambda b, i: (P32_GOUT_SLAB // 4, row(b, i), 0)),
            pl.BlockSpec((1, c, SLAB), lambda b, i: (P32_SMALL_SLAB, row(b, i), 0)),
            pl.BlockSpec((4, c, SLAB), lambda b, i: (P16_GV, row(b, i), 0)),
            pl.BlockSpec((2, N_META, SLAB), lambda b, i: (P32_GK_SLAB // 2, 0, 0)),
            pl.BlockSpec((4, N_META, SLAB), lambda b, i: (P16_GV, 0, 0)),
            pl.BlockSpec((1, N_META, SLAB), lambda b, i: (P32_SMALL_SLAB, 0, 0)),
            pl.BlockSpec((SLAB, GLA_KEY_WIDTH), lambda b, i: (0, 0)),
            pl.BlockSpec((1, GLA_KEY_WIDTH), lambda b, i: (0, 0)),
            pl.BlockSpec((1, GLA_DV), lambda b, i: (0, 0)),
        ],
        out_specs=pl.BlockSpec((c, GLA_WIDTH), lambda b, i: (row(b, i), 0)),
        scratch_shapes=[
            pltpu.VMEM((GLA_HEADS, GLA_DV, GLA_DK), F32),
            pltpu.VMEM((c, GLA_KEY_WIDTH), F32),
        ],
        compiler_params=_params(("parallel", "arbitrary")),
        name="gla",
    )(p32, p32, p32, p32, p16, p32_meta, p16_meta, p32_meta, w_gate_pad, b_gate, norm_g)


def _out_kernel(h_ref, a_ref, g_ref, wa_ref, wg_ref, pg_ref, o_ref):
    m = _dot(a_ref[...], wa_ref[...]) + _dot(g_ref[...], wg_ref[...])
    o_ref[...] = h_ref[...] + _rms(m, pg_ref[...])


def _out(hs, o_dsa, o_gla, w_dsa, w_gla, post_g, tm):
    r, d = hs.shape
    return pl.pallas_call(
        _out_kernel,
        out_shape=jax.ShapeDtypeStruct((r, d), F32),
        grid=(r // tm,),
        in_specs=[
            pl.BlockSpec((tm, d), lambda i: (i, 0)),
            pl.BlockSpec((tm, DSA_WIDTH), lambda i: (i, 0)),
            pl.BlockSpec((tm, GLA_WIDTH), lambda i: (i, 0)),
            pl.BlockSpec((DSA_WIDTH, d), lambda i: (0, 0)),
            pl.BlockSpec((GLA_WIDTH, d), lambda i: (0, 0)),
            pl.BlockSpec((1, d), lambda i: (0, 0)),
        ],
        out_specs=pl.BlockSpec((tm, d), lambda i: (i, 0)),
        compiler_params=_params(("parallel",)),
        name="outproj",
    )(hs, o_dsa, o_gla, w_dsa, w_gla, post_g)


def _tile(n, pref):
    t = min(n, pref)
    while n % t:
        t //= 2
    return t


def _proj_weights(w_in):
    d = w_in.shape[0]
    offs = np.cumsum((0,) + IN_SPLITS)
    dq, dk, dv, iq, ik, iw, gq, gk, gv, glow, gout = (w_in[:, offs[i]:offs[i + 1]] for i in range(11))
    z = lambda n: jnp.zeros((d, n), w_in.dtype)
    ik2 = jnp.concatenate([ik, z(IDX_DIM), z(IDX_DIM), ik], axis=1)
    q_scale = DSA_HEAD_DIM ** -0.5 * float(np.log2(np.e))
    w16 = jnp.concatenate([dq * q_scale, dk, dv, iq * (IDX_DIM ** -0.5), gv, ik2], axis=1)
    small = jnp.concatenate([iw * (IDX_HEADS ** -0.5), glow, z(SLAB - IDX_HEADS - GLA_GATE_RANK)], axis=1)
    w32 = jnp.concatenate([gq, gk, gout, small], axis=1)
    return w16.astype(BF16), w32.astype(BF16)


def kernel(x, meta_tokens, ffn1_pre_g, ffn1_w_gate, ffn1_w_up, ffn1_w_down, ffn1_post_g, mix_pre_g, w_in, w_gate_up, b_gate, gla_norm_g, w_out, mix_post_g, ffn2_pre_g, ffn2_w_gate, ffn2_w_up, ffn2_w_down, ffn2_post_g):
    batch, seq, d = x.shape
    depth = w_in.shape[0]
    assert depth == 1, "the meta rows skip the mixer, which is only valid for the last layer"
    assert seq % (2 * IDX_CW) == 0 and meta_tokens.shape[0] == N_META, "score tiles are visited in pairs"
    d_ff = ffn1_w_gate.shape[-1]
    topk = min(TOPK_MAX, seq // 4)
    rows = batch * seq
    tm = _tile(rows, 512)
    tf = _tile(d_ff, 512)
    tq = _tile(seq, 256)
    tk = _tile(seq, 1024)

    hs = x.reshape(rows, d)
    hm = meta_tokens.astype(x.dtype)
    row2 = lambda v: v.reshape(1, -1)
    for l in range(depth):
        ffn1 = (row2(ffn1_pre_g[l]), ffn1_w_gate[l].astype(BF16), ffn1_w_up[l].astype(BF16),
                ffn1_w_down[l].astype(BF16), row2(ffn1_post_g[l]))
        hs = _ffn(hs, *ffn1, tm=tm, tf=tf)
        hm = _ffn(hm, *ffn1, tm=N_META, tf=tf)

        w16, w32 = _proj_weights(w_in[l])
        pre = row2(mix_pre_g[l])
        proj16 = functools.partial(_proj, g=pre, w=w16, out_dtype=BF16, slabs_per_step=P16_SLABS_PER_STEP)
        proj32 = functools.partial(_proj, g=pre, w=w32, out_dtype=F32, slabs_per_step=P32_SLABS_PER_STEP)
        tm_proj = _tile(rows, 1024)
        p16, p32 = proj16(hs, tm=tm_proj), proj32(hs, tm=tm_proj)
        p16m, p32m = proj16(hm, tm=N_META), proj32(hm, tm=N_META)

        pad_rows = lambda a, n: jnp.pad(a, ((0, 0), (0, n - a.shape[1]), (0, 0)))
        bias = _idx(p16, p32, pad_rows(p16m, IDX_CW)[P16_IK2_SLAB], batch, seq, topk)
        p16m_pad = pad_rows(p16m, LANES)

        def v_t(a):
            vt = jnp.transpose(a[4 * P16_DV:4 * P16_DV + 4], (0, 2, 1)).reshape(DSA_HEADS, DSA_HEAD_DIM, -1)
            ones = jnp.zeros((DSA_HEADS, VT_ROWS - DSA_HEAD_DIM, vt.shape[-1]), vt.dtype).at[:, 0].set(1)
            return jnp.concatenate([vt, ones], axis=1).reshape(DSA_HEADS * VT_ROWS, -1)

        o_dsa = _attn(p16, v_t(p16), bias, p16m_pad, v_t(p16m_pad), batch, seq, tq, tk)

        w_gate_pad = jnp.zeros((SLAB, GLA_KEY_WIDTH), F32).at[
            IDX_HEADS:IDX_HEADS + GLA_GATE_RANK].set(w_gate_up[l])
        o_gla = _gla(p32, p16, p32m, p16m, w_gate_pad, row2(b_gate[l]), row2(gla_norm_g[l]), batch, seq)

        wo = w_out[l].astype(BF16)
        hs = _out(hs, o_dsa, o_gla, wo[:DSA_WIDTH], wo[DSA_WIDTH:], row2(mix_post_g[l]), tm)
        ffn2 = (row2(ffn2_pre_g[l]), ffn2_w_gate[l].astype(BF16), ffn2_w_up[l].astype(BF16),
                ffn2_w_down[l].astype(BF16), row2(ffn2_post_g[l]))
        hs = _ffn(hs, *ffn2, tm=tm, tf=tf)
    return hs.reshape(batch, seq, d)
```

```python
import functools

import numpy as np
import jax
import jax.numpy as jnp
from jax import lax
from jax.experimental import pallas as pl
from jax.experimental.pallas import tpu as pltpu

F32, BF16, I32 = jnp.float32, jnp.bfloat16, jnp.int32

N_META = 16
FFN_RES = 0.5
EPS = 1e-6
DSA_HEADS = 8
DSA_HEAD_DIM = 128
DSA_WIDTH = DSA_HEADS * DSA_HEAD_DIM
IDX_HEADS = 16
IDX_DIM = 64
TOPK_MAX = 256
GLA_HEADS = 4
GLA_DK = 128
GLA_DV = 256
GLA_KEY_WIDTH = GLA_HEADS * GLA_DK
GLA_WIDTH = GLA_HEADS * GLA_DV
GLA_GATE_RANK = 16
GLA_TAU = 16.0
GLA_CHUNK = 64
GLA_SUB = 16
GLA_ROWS = 256
IN_SPLITS = (DSA_WIDTH, DSA_WIDTH, DSA_WIDTH, IDX_HEADS * IDX_DIM, IDX_DIM, IDX_HEADS,
             GLA_KEY_WIDTH, GLA_KEY_WIDTH, GLA_WIDTH, GLA_GATE_RANK, GLA_WIDTH)

LANES = 128
SLAB = 256
VMEM_LIMIT_BYTES = 56 * 1024 * 1024

P16_DQ, P16_DK, P16_DV, P16_IQ, P16_GV, P16_IK2_SLAB, P16_SLABS = 0, 1, 2, 3, 4, 20, 21
P32_GQ_SLAB, P32_GK_SLAB, P32_GOUT_SLAB, P32_SMALL_SLAB, P32_SLABS = 0, 2, 4, 8, 9
P16_SLABS_PER_STEP, P32_SLABS_PER_STEP = 7, 3
VT_ROWS = DSA_HEAD_DIM + 16

NEG_BIAS = -1e30


def _params(sem, **flags):
    return pltpu.CompilerParams(dimension_semantics=sem, vmem_limit_bytes=VMEM_LIMIT_BYTES,
                                flags=flags or None)


def _dot(a, b):
    return jnp.dot(a, b, preferred_element_type=F32)


def _dot_nt(a, b):
    return lax.dot_general(a, b, (((1,), (1,)), ((), ())), preferred_element_type=F32)


def _dot_tn(a, b):
    return lax.dot_general(a, b, (((0,), (0,)), ((), ())), preferred_element_type=F32)


def _split3(x):
    hi = x.astype(BF16)
    r = x - hi.astype(F32)
    mid = r.astype(BF16)
    lo = (r - mid.astype(F32)).astype(BF16)
    return hi, mid, lo


def _dot_exact_lhs01(l01, x):
    hi, mid, lo = _split3(x)
    return _dot(l01, hi) + _dot(l01, mid) + _dot(l01, lo)


def _dot_f32(a, b):
    ah, am, al = _split3(a)
    bh, bm, bl = _split3(b)
    return (_dot(ah, bh) + (_dot(ah, bm) + _dot(am, bh))
            + (_dot(ah, bl) + _dot(am, bm) + _dot(al, bh)))


def _rms(x, g):
    return x * lax.rsqrt(jnp.mean(x * x, axis=-1, keepdims=True) + EPS) * g


def _ffn_kernel(x_ref, preg_ref, wg_ref, wu_ref, wd_ref, postg_ref, o_ref, xn_ref, *, nf):
    f = pl.program_id(1)

    @pl.when(f == 0)
    def _():
        xn_ref[...] = _rms(x_ref[...], preg_ref[...]).astype(BF16)
        o_ref[...] = jnp.zeros_like(o_ref)

    xn = xn_ref[...]
    g = _dot(xn, wg_ref[...])
    u = _dot(xn, wu_ref[...])
    a = (g * jax.nn.sigmoid(g)) * u
    o_ref[...] += _dot(a.astype(BF16), wd_ref[...])

    @pl.when(f == nf - 1)
    def _():
        o_ref[...] = x_ref[...] + FFN_RES * _rms(o_ref[...], postg_ref[...])


def _ffn(x, pre_g, wg, wu, wd, post_g, tm, tf):
    r, d = x.shape
    nf = wg.shape[1] // tf
    return pl.pallas_call(
        functools.partial(_ffn_kernel, nf=nf),
        out_shape=jax.ShapeDtypeStruct((r, d), F32),
        grid=(r // tm, nf),
        in_specs=[
            pl.BlockSpec((tm, d), lambda i, f: (i, 0)),
            pl.BlockSpec((1, d), lambda i, f: (0, 0)),
            pl.BlockSpec((d, tf), lambda i, f: (0, f)),
            pl.BlockSpec((d, tf), lambda i, f: (0, f)),
            pl.BlockSpec((tf, d), lambda i, f: (f, 0)),
            pl.BlockSpec((1, d), lambda i, f: (0, 0)),
        ],
        out_specs=pl.BlockSpec((tm, d), lambda i, f: (i, 0)),
        scratch_shapes=[pltpu.VMEM((tm, d), BF16)],
        compiler_params=_params(("parallel", "arbitrary")),
        name="ffn",
    )(x, pre_g, wg, wu, wd, post_g)


def _proj_kernel(x_ref, g_ref, w_ref, o_ref, *rest, vt_step, vt_col):
    vt_ref, xn_ref = rest if vt_step is not None else (None, rest[0])

    @pl.when(pl.program_id(1) == 0)
    def _():
        xn_ref[...] = _rms(x_ref[...], g_ref[...]).astype(BF16)

    res = _dot(xn_ref[...], w_ref[...])
    for s in range(o_ref.shape[0]):
        o_ref[s] = res[:, s * SLAB:(s + 1) * SLAB].astype(o_ref.dtype)

    if vt_ref is not None:
        @pl.when(pl.program_id(1) == vt_step)
        def _():
            tm = res.shape[0]
            vt = res[:, vt_col:vt_col + DSA_WIDTH].T.astype(BF16)
            pad = VT_ROWS - DSA_HEAD_DIM
            ones = jnp.where(lax.broadcasted_iota(I32, (pad, tm), 0) == 0, 1.0, 0.0).astype(BF16)
            for h in range(DSA_HEADS):
                vt_ref[h * VT_ROWS:h * VT_ROWS + DSA_HEAD_DIM, :] = vt[h * DSA_HEAD_DIM:(h + 1) * DSA_HEAD_DIM]
                vt_ref[h * VT_ROWS + DSA_HEAD_DIM:(h + 1) * VT_ROWS, :] = ones


def _proj(x, g, w, out_dtype, tm, slabs_per_step, vt_col=None):
    r, d = x.shape
    n_slabs = w.shape[1] // SLAB
    tn = slabs_per_step * SLAB
    out_shape = [jax.ShapeDtypeStruct((n_slabs, r, SLAB), out_dtype)]
    out_specs = [pl.BlockSpec((slabs_per_step, tm, SLAB), lambda i, j: (j, i, 0))]
    vt_step = None
    if vt_col is not None:
        vt_step = vt_col // tn
        assert (vt_col + DSA_WIDTH - 1) // tn == vt_step, "the transposed columns must fall in one grid step"
        out_shape.append(jax.ShapeDtypeStruct((DSA_HEADS * VT_ROWS, r), BF16))
        out_specs.append(pl.BlockSpec((DSA_HEADS * VT_ROWS, tm), lambda i, j: (0, i)))
    out = pl.pallas_call(
        functools.partial(_proj_kernel, vt_step=vt_step, vt_col=None if vt_col is None else vt_col % tn),
        out_shape=out_shape,
        grid=(r // tm, n_slabs // slabs_per_step),
        in_specs=[
            pl.BlockSpec((tm, d), lambda i, j: (i, 0)),
            pl.BlockSpec((1, d), lambda i, j: (0, 0)),
            pl.BlockSpec((d, tn), lambda i, j: (0, j)),
        ],
        out_specs=out_specs,
        scratch_shapes=[pltpu.VMEM((tm, d), BF16)],
        compiler_params=_params(("arbitrary", "arbitrary")),
        name="proj",
    )(x, g, w)
    return out if vt_col is not None else out[0]


IDX_TQ = 256
IDX_CW = 256
BF16_KEY_MIN, BF16_KEY_MAX = -(2 ** 15), 2 ** 15 - 1
STAGE2_BITS = 18


def _key32(pattern):
    return jnp.where(pattern < 0, pattern ^ 0x7FFFFFFF, pattern)


def _pattern_of_key16(k16):
    return lax.shift_left(jnp.where(k16 < 0, k16 ^ 0x7FFF, k16), 16)


def _idx_kernel(iq_ref, sm_ref, ik_ref, ikm_ref, bias_ref, sc_ref, sc16_ref, w_ref, *, seq, topk):
    qi = pl.program_id(1)
    tq, cw = IDX_TQ, IDX_CW
    nt = seq // cw
    n_x = qi + 1
    kf = float(topk)

    w_ref[...] = sm_ref[0][:, 0:LANES].T
    key_pos = lax.broadcasted_iota(I32, (cw, tq), 0)
    q_pos = qi * tq + lax.broadcasted_iota(I32, (cw, tq), 1)

    def score_tile(k2):
        lhs = jnp.concatenate([k2[:, 0:LANES], k2[:, LANES:2 * LANES]], axis=0)
        acc = jnp.zeros((cw, tq), F32)
        for p in range(IDX_HEADS // 2):
            rhs = iq_ref[p // 2][:, (p % 2) * LANES:(p % 2 + 1) * LANES]
            l2 = _dot_nt(lhs, rhs)
            acc = acc + jnp.maximum(l2[0:cw], 0.0) * w_ref[2 * p:2 * p + 1, :]
            acc = acc + jnp.maximum(l2[cw:2 * cw], 0.0) * w_ref[2 * p + 1:2 * p + 2, :]
        return acc

    def x_tile(c):
        k2 = ik_ref[0, pl.ds(pl.multiple_of(c * cw, cw), cw), :]
        acc = jnp.where(c * cw + key_pos <= q_pos, score_tile(k2), -jnp.inf)
        sc_ref[c + 1] = acc
        sc16_ref[c + 1] = acc.astype(BF16)

    def x_tile_pair(p, carry):
        x_tile(2 * p)
        x_tile(2 * p + 1)
        return carry

    lax.fori_loop(0, (n_x + 1) // 2, x_tile_pair, 0)
    acc_m = jnp.where(key_pos < N_META, score_tile(ikm_ref[...]), -jnp.inf)
    sc_ref[0] = acc_m
    sc16_ref[0] = acc_m.astype(BF16)
    sc_ref[n_x + 1] = jnp.full((cw, tq), -jnp.inf, F32)
    sc16_ref[n_x + 1] = jnp.full((cw, tq), -jnp.inf, BF16)
    n_pairs = (n_x + 2) // 2

    one16, zero16 = jnp.ones((cw, tq), BF16), jnp.zeros((cw, tq), BF16)

    def count16(c16):
        def tile_count(c):
            hit = jnp.where(sc16_ref[c] >= c16, one16, zero16).reshape(cw // 16, 16, tq)
            parts = [hit[i] for i in range(cw // 16)]
            while len(parts) > 1:
                parts = [parts[i] + parts[i + 1] for i in range(0, len(parts), 2)]
            return parts[0].astype(F32)

        cnt = lax.fori_loop(0, n_pairs, lambda p, a: a + (tile_count(2 * p) + tile_count(2 * p + 1)),
                            jnp.zeros((16, tq), F32))
        return jnp.sum(cnt, axis=0, keepdims=True)

    def stage1(it, k16):
        cand = k16 + lax.shift_left(jnp.int32(1), 15 - it)
        c32 = lax.bitcast_convert_type(_pattern_of_key16(cand), F32)
        c16 = jnp.broadcast_to(c32, (cw, tq)).astype(BF16)
        return jnp.where(count16(c16) >= kf, cand, k16)

    k16 = lax.fori_loop(0, 16, stage1, jnp.full((1, tq), BF16_KEY_MIN, I32))

    near_zero = jnp.logical_and(k16 >= -2, k16 <= 1)
    reach = jnp.where(near_zero, 2, 1)
    lo = _key32(_pattern_of_key16(jnp.maximum(k16 - reach, BF16_KEY_MIN)))
    hi = _key32(_pattern_of_key16(jnp.minimum(k16 + reach, BF16_KEY_MAX)))
    first_bit = jnp.where(jnp.max(jnp.where(near_zero, 1.0, 0.0)) > 0.5, 0, 1)

    def count32(cf):
        def tile_count(c):
            hit = jnp.where(sc_ref[c] >= cf, 1.0, 0.0)
            return jnp.sum(hit.reshape(cw // 8, 8, tq), axis=0)

        cnt = lax.fori_loop(0, n_pairs, lambda p, a: a + (tile_count(2 * p) + tile_count(2 * p + 1)),
                            jnp.zeros((8, tq), F32))
        return jnp.sum(cnt, axis=0, keepdims=True)

    short = (qi * tq + lax.broadcasted_iota(I32, (1, tq), 1) + 1 + N_META) <= topk

    def unsettled(cnt_t):
        return (jnp.max(jnp.where(short | (cnt_t == kf), 0.0, 1.0)) > 0.5).astype(I32)

    def stage2_cond(st):
        it, _, _, go = st
        return jnp.logical_and(it < STAGE2_BITS, go > 0)

    def stage2(st):
        it, t, cnt_t, _ = st
        cand = t + lax.shift_left(jnp.int32(1), STAGE2_BITS - 1 - it)
        cnt = count32(lax.bitcast_convert_type(jnp.where(cand < 0, cand ^ 0x7FFFFFFF, cand), F32))
        ok = jnp.logical_and(cand < hi, cnt >= kf)
        cnt_t = jnp.where(ok, cnt, cnt_t)
        return it + 1, jnp.where(ok, cand, t), cnt_t, unsettled(cnt_t)

    _, t, cnt_t, tied_any = lax.while_loop(stage2_cond, stage2,
                                           (first_bit.astype(I32), lo, jnp.full((1, tq), -1.0, F32), jnp.int32(1)))
    thr = lax.bitcast_convert_type(jnp.where(t < 0, t ^ 0x7FFFFFFF, t), F32)
    thr = jnp.where(short, jnp.finfo(F32).min, thr)

    def write_bias(bias_of):
        for c in range(nt):
            @pl.when(c < n_x)
            def _(c=c):
                bias_ref[0, c * cw:(c + 1) * cw, :] = bias_of(
                    sc_ref[c + 1], N_META + c * cw + key_pos).astype(BF16)

            @pl.when(c >= n_x)
            def _(c=c):
                bias_ref[0, c * cw:(c + 1) * cw, :] = jnp.full((cw, tq), NEG_BIAS, BF16)

        meta_pos = lax.broadcasted_iota(I32, (LANES, tq), 0)
        bias_ref[0, seq:seq + LANES, :] = bias_of(sc_ref[0, 0:LANES, :], meta_pos).astype(BF16)

    @pl.when(tied_any == 0)
    def _():
        write_bias(lambda s, pos: jnp.where(s >= thr, 0.0, NEG_BIAS))

    @pl.when(tied_any > 0)
    def _():
        def count_where(hit_of):
            def tile_count(slot):
                pos = key_pos + jnp.where(slot == 0, 0, N_META + (slot - 1) * cw)
                return jnp.sum(hit_of(sc_ref[slot], pos).reshape(cw // 8, 8, tq), axis=0)

            cnt = lax.fori_loop(0, 2 * n_pairs, lambda s_, a: a + tile_count(s_), jnp.zeros((8, tq), F32))
            return jnp.sum(cnt, axis=0, keepdims=True)

        need = kf - count_where(lambda s, pos: jnp.where(s > thr, 1.0, 0.0))
        pos_bits = int(seq + N_META).bit_length()

        def pos_bit(i, cut):
            cand = cut | lax.shift_left(jnp.int32(1), pos_bits - 1 - i)
            below = count_where(lambda s, pos: jnp.where(s == thr, jnp.where(pos < cand, 1.0, 0.0), 0.0))
            return jnp.where(below < need, cand, cut)

        cut = lax.fori_loop(0, pos_bits, pos_bit, jnp.zeros((1, tq), I32))
        tied = jnp.logical_and(jnp.logical_not(short), cnt_t != kf)
        cut = jnp.where(tied, cut, jnp.iinfo(I32).max)
        write_bias(lambda s, pos: jnp.where(
            s > thr, 0.0, jnp.where(s == thr, jnp.where(pos <= cut, 0.0, NEG_BIAS), NEG_BIAS)))


def _idx(p16, p32, ik2_meta, batch, seq, topk):
    nq = seq // IDX_TQ
    nt = seq // IDX_CW
    return pl.pallas_call(
        functools.partial(_idx_kernel, seq=seq, topk=topk),
        out_shape=jax.ShapeDtypeStruct((batch, seq + LANES, seq), BF16),
        grid=(batch, nq),
        in_specs=[
            pl.BlockSpec((4, IDX_TQ, SLAB), lambda b, q: (P16_IQ, b * nq + q, 0)),
            pl.BlockSpec((1, IDX_TQ, SLAB), lambda b, q: (P32_SMALL_SLAB, b * nq + q, 0)),
            pl.BlockSpec((1, seq, SLAB), lambda b, q: (P16_IK2_SLAB, b, 0)),
            pl.BlockSpec((IDX_CW, SLAB), lambda b, q: (0, 0)),
        ],
        out_specs=pl.BlockSpec((1, seq + LANES, IDX_TQ), lambda b, q: (b, 0, q)),
        scratch_shapes=[
            pltpu.VMEM((nt + 2, IDX_CW, IDX_TQ), F32),
            pltpu.VMEM((nt + 2, IDX_CW, IDX_TQ), BF16),
            pltpu.VMEM((LANES, IDX_TQ), F32),
        ],
        compiler_params=_params(("parallel", "arbitrary")),
        name="idx",
    )(p16, p32, p16, ik2_meta)


ATTN_AHEAD = 2


def _attn_kernel(qt_ref, kt_ref, q_ref, k_ref, vt_ref, b_ref, bm_ref, km_ref, vtm_ref, o_ref,
                 m_ref, acc_ref, *, tq, tk):
    p = pl.program_id(1)
    qi, ki = qt_ref[p], kt_ref[p]

    def head(ref, h):
        return ref[h // 2][:, (h % 2) * LANES:(h % 2 + 1) * LANES]

    def accumulate(kref, vtref, bias):
        def scores(h):
            return _dot_nt(head(kref, h), head(q_ref, h)) + bias

        pending = [scores(h) for h in range(ATTN_AHEAD)]
        for h in range(DSA_HEADS):
            s = pending.pop(0)
            if h + ATTN_AHEAD < DSA_HEADS:
                pending.append(scores(h + ATTN_AHEAD))
            m_prev = m_ref[h]
            m_new = jnp.maximum(m_prev, jnp.max(s, axis=0, keepdims=True))
            alpha = jnp.exp2(m_prev - m_new)
            pr = jnp.exp2(s - m_new).astype(BF16)
            vt = vtref[h * VT_ROWS:(h + 1) * VT_ROWS, :]
            acc_ref[h] = alpha * acc_ref[h] + _dot(vt, pr)
            m_ref[h] = m_new

    @pl.when(ki == 0)
    def _():
        m_ref[...] = jnp.full(m_ref.shape, -jnp.inf, F32)
        acc_ref[...] = jnp.zeros_like(acc_ref)
        accumulate(km_ref, vtm_ref, bm_ref[0].astype(F32))

    accumulate(k_ref, vt_ref, b_ref[0].astype(F32))

    @pl.when(ki == (qi * tq + tq - 1) // tk)
    def _():
        for h in range(DSA_HEADS):
            hs = slice(h * DSA_HEAD_DIM, (h + 1) * DSA_HEAD_DIM)
            acc = acc_ref[h]
            o_ref[:, hs] = (acc[0:DSA_HEAD_DIM] / acc[DSA_HEAD_DIM:DSA_HEAD_DIM + 1]).T.astype(o_ref.dtype)


def _attn(p16, vt, bias, p16_meta, vt_meta, batch, seq, tq, tk):
    nq, nk = seq // tq, seq // tk
    pairs = [(q, k) for q in range(nq) for k in range((q * tq + tq - 1) // tk + 1)]
    q_tab = jnp.asarray([q for q, _ in pairs], I32)
    k_tab = jnp.asarray([k for _, k in pairs], I32)
    grid_spec = pltpu.PrefetchScalarGridSpec(
        num_scalar_prefetch=2,
        grid=(batch, len(pairs)),
        in_specs=[
            pl.BlockSpec((4, tq, SLAB), lambda b, p, qt, kt: (P16_DQ, b * nq + qt[p], 0)),
            pl.BlockSpec((4, tk, SLAB), lambda b, p, qt, kt: (P16_DK, b * nk + kt[p], 0)),
            pl.BlockSpec((DSA_HEADS * VT_ROWS, tk), lambda b, p, qt, kt: (0, b * nk + kt[p])),
            pl.BlockSpec((1, tk, tq), lambda b, p, qt, kt: (b, kt[p], qt[p])),
            pl.BlockSpec((1, LANES, tq), lambda b, p, qt, kt: (b, seq // LANES, qt[p])),
            pl.BlockSpec((4, LANES, SLAB), lambda b, p, qt, kt: (P16_DK, 0, 0)),
            pl.BlockSpec((DSA_HEADS * VT_ROWS, LANES), lambda b, p, qt, kt: (0, 0)),
        ],
        out_specs=pl.BlockSpec((tq, DSA_WIDTH), lambda b, p, qt, kt: (b * nq + qt[p], 0)),
        scratch_shapes=[
            pltpu.VMEM((DSA_HEADS, 1, tq), F32),
            pltpu.VMEM((DSA_HEADS, VT_ROWS, tq), F32),
        ],
    )
    return pl.pallas_call(
        functools.partial(_attn_kernel, tq=tq, tk=tk),
        out_shape=jax.ShapeDtypeStruct((batch * seq, DSA_WIDTH), BF16),
        grid_spec=grid_spec,
        compiler_params=_params(("parallel", "arbitrary")),
        name="attn",
    )(q_tab, k_tab, p16, p16, vt, bias, bias, p16_meta, vt_meta)


def _log_sigmoid(x):
    return jnp.minimum(x, 0.0) - jnp.log1p(jnp.exp(-jnp.abs(x)))


def _gla_kernel(gq_ref, gk_ref, go_ref, sm_ref, gv_ref, mk_ref, mv_ref, msm_ref, wgate_ref, bgate_ref,
                ng_ref, o_ref, st_ref, bc_ref):
    c = GLA_CHUNK

    def log_decay(small):
        return _log_sigmoid(_dot_f32(small, wgate_ref[...]) + bgate_ref[...]) / GLA_TAU

    def lower_ones(n):
        r = lax.broadcasted_iota(I32, (n, n), 0)
        cc = lax.broadcasted_iota(I32, (n, n), 1)
        return r >= cc

    @pl.when(pl.program_id(1) == 0)
    def _():
        bcm = _dot_exact_lhs01(jnp.where(lower_ones(N_META), 1.0, 0.0).astype(BF16),
                               log_decay(msm_ref[0]))
        for h in range(GLA_HEADS):
            ks = slice(h * GLA_DK, (h + 1) * GLA_DK)
            kd = mk_ref[h // 2][:, (h % 2) * GLA_DK:(h % 2 + 1) * GLA_DK] * jnp.exp(
                bcm[N_META - 1:N_META, ks] - bcm[:, ks])
            st_ref[h] = _dot_tn(mv_ref[h], kd.astype(BF16))

    rows = bc_ref.shape[0]
    r_i = lax.broadcasted_iota(I32, (rows, rows), 0)
    c_i = lax.broadcasted_iota(I32, (rows, rows), 1)
    chunk_tri = jnp.logical_and(r_i >= c_i, r_i // c == c_i // c)
    bc_ref[...] = _dot_exact_lhs01(jnp.where(chunk_tri, 1.0, 0.0).astype(BF16), log_decay(sm_ref[0]))
    causal = lower_ones(c)
    sub = GLA_SUB
    lane = lax.broadcasted_iota(I32, (sub, c), 1)

    for ch in range(rows // c):
        base = ch * c
        for h in range(GLA_HEADS):
            ks = slice(h * GLA_DK, (h + 1) * GLA_DK)
            hsl = slice((h % 2) * GLA_DK, (h % 2 + 1) * GLA_DK)
            q = gq_ref[h // 2][base:base + c, hsl] * (GLA_DK ** -0.5)
            k = gk_ref[h // 2][base:base + c, hsl]
            v = gv_ref[h][base:base + c]
            bc = bc_ref[base:base + c, ks]

            blocks = []
            for s0 in range(0, c, sub):
                q_s, bc_s = q[s0:s0 + sub], bc[s0:s0 + sub]
                diag = jnp.zeros((sub, c), F32)
                for j in range(s0, s0 + sub):
                    e = jnp.exp(jnp.minimum(bc_s - bc_ref[base + j:base + j + 1, ks], 0.0))
                    k_j = gk_ref[h // 2][base + j:base + j + 1, hsl]
                    diag = jnp.where(lane == j, jnp.sum(q_s * k_j * e, axis=1, keepdims=True), diag)
                if s0:
                    b_edge = bc_ref[base + s0 - 1:base + s0, ks]
                    q_t = q_s * jnp.exp(bc_s - b_edge)
                    k_t = k * jnp.exp(jnp.minimum(b_edge - bc, 0.0))
                    diag = jnp.where(lane < s0, _dot_nt(q_t.astype(BF16), k_t.astype(BF16)), diag)
                blocks.append(diag)
            attn = jnp.where(causal, jnp.concatenate(blocks, axis=0), 0.0)

            state = st_ref[h]
            b_last = bc[c - 1:c, :]
            o = _dot_nt((q * jnp.exp(bc)).astype(BF16), state.astype(BF16))
            o = o + _dot(attn.astype(BF16), v)
            kd = k * jnp.exp(b_last - bc)
            st_ref[h] = state * jnp.exp(b_last) + _dot_tn(v, kd.astype(BF16))

            g = go_ref[h][base:base + c]
            y = _rms(o, ng_ref[...]) * (g * jax.nn.sigmoid(g))
            o_ref[base:base + c, h * GLA_DV:(h + 1) * GLA_DV] = y.astype(o_ref.dtype)


def _gla(p32, p16, p32_meta, p16_meta, w_gate_pad, b_gate, norm_g, batch, seq):
    c = _tile(seq, GLA_ROWS)
    ns = seq // c
    row = lambda b, i: b * ns + i
    return pl.pallas_call(
        _gla_kernel,
        out_shape=jax.ShapeDtypeStruct((batch * seq, GLA_WIDTH), BF16),
        grid=(batch, ns),
        in_specs=[
            pl.BlockSpec((2, c, SLAB), lambda b, i: (P32_GQ_SLAB // 2, row(b, i), 0)),
            pl.BlockSpec((2, c, SLAB), lambda b, i: (P32_GK_SLAB // 2, row(b, i), 0)),
            pl.BlockSpec((4, c, SLAB), lambda b, i: (P32_GOUT_SLAB // 4, row(b, i), 0)),
            pl.BlockSpec((1, c, SLAB), lambda b, i: (P32_SMALL_SLAB, row(b, i), 0)),
            pl.BlockSpec((4, c, SLAB), lambda b, i: (P16_GV, row(b, i), 0)),
            pl.BlockSpec((2, N_META, SLAB), lambda b, i: (P32_GK_SLAB // 2, 0, 0)),
            pl.BlockSpec((4, N_META, SLAB), lambda b, i: (P16_GV, 0, 0)),
            pl.BlockSpec((1, N_META, SLAB), lambda b, i: (P32_SMALL_SLAB, 0, 0)),
            pl.BlockSpec((SLAB, GLA_KEY_WIDTH), lambda b, i: (0, 0)),
            pl.BlockSpec((1, GLA_KEY_WIDTH), lambda b, i: (0, 0)),
            pl.BlockSpec((1, GLA_DV), lambda b, i: (0, 0)),
        ],
        out_specs=pl.BlockSpec((c, GLA_WIDTH), lambda b, i: (row(b, i), 0)),
        scratch_shapes=[
            pltpu.VMEM((GLA_HEADS, GLA_DV, GLA_DK), F32),
            pltpu.VMEM((c, GLA_KEY_WIDTH), F32),
        ],
        compiler_params=_params(("parallel", "arbitrary")),
        name="gla",
    )(p32, p32, p32, p32, p16, p32_meta, p16_meta, p32_meta, w_gate_pad, b_gate, norm_g)


def _out_kernel(h_ref, a_ref, g_ref, wa_ref, wg_ref, pg_ref, o_ref):
    m = _dot(a_ref[...], wa_ref[...]) + _dot(g_ref[...], wg_ref[...])
    o_ref[...] = h_ref[...] + _rms(m, pg_ref[...])


def _out(hs, o_dsa, o_gla, w_dsa, w_gla, post_g, tm):
    r, d = hs.shape
    return pl.pallas_call(
        _out_kernel,
        out_shape=jax.ShapeDtypeStruct((r, d), F32),
        grid=(r // tm,),
        in_specs=[
            pl.BlockSpec((tm, d), lambda i: (i, 0)),
            pl.BlockSpec((tm, DSA_WIDTH), lambda i: (i, 0)),
            pl.BlockSpec((tm, GLA_WIDTH), lambda i: (i, 0)),
            pl.BlockSpec((DSA_WIDTH, d), lambda i: (0, 0)),
            pl.BlockSpec((GLA_WIDTH, d), lambda i: (0, 0)),
            pl.BlockSpec((1, d), lambda i: (0, 0)),
        ],
        out_specs=pl.BlockSpec((tm, d), lambda i: (i, 0)),
        compiler_params=_params(("parallel",)),
        name="outproj",
    )(hs, o_dsa, o_gla, w_dsa, w_gla, post_g)


def _tile(n, pref):
    t = min(n, pref)
    while n % t:
        t //= 2
    return t


def _proj_weights(w_in):
    d = w_in.shape[0]
    offs = np.cumsum((0,) + IN_SPLITS)
    dq, dk, dv, iq, ik, iw, gq, gk, gv, glow, gout = (w_in[:, offs[i]:offs[i + 1]] for i in range(11))
    z = lambda n: jnp.zeros((d, n), w_in.dtype)
    ik2 = jnp.concatenate([ik, z(IDX_DIM), z(IDX_DIM), ik], axis=1)
    q_scale = DSA_HEAD_DIM ** -0.5 * float(np.log2(np.e))
    w16 = jnp.concatenate([dq * q_scale, dk, dv, iq * (IDX_DIM ** -0.5), gv, ik2], axis=1)
    small = jnp.concatenate([iw * (IDX_HEADS ** -0.5), glow, z(SLAB - IDX_HEADS - GLA_GATE_RANK)], axis=1)
    w32 = jnp.concatenate([gq, gk, gout, small], axis=1)
    return w16.astype(BF16), w32.astype(BF16)


def kernel(x, meta_tokens, ffn1_pre_g, ffn1_w_gate, ffn1_w_up, ffn1_w_down, ffn1_post_g, mix_pre_g, w_in, w_gate_up, b_gate, gla_norm_g, w_out, mix_post_g, ffn2_pre_g, ffn2_w_gate, ffn2_w_up, ffn2_w_down, ffn2_post_g):
    batch, seq, d = x.shape
    depth = w_in.shape[0]
    assert depth == 1, "the meta rows skip the mixer, which is only valid for the last layer"
    assert seq % (2 * IDX_CW) == 0 and meta_tokens.shape[0] == N_META, "score tiles are visited in pairs"
    d_ff = ffn1_w_gate.shape[-1]
    topk = min(TOPK_MAX, seq // 4)
    rows = batch * seq
    tm = _tile(rows, 512)
    tf = _tile(d_ff, 512)
    tq = _tile(seq, 256)
    tk = _tile(seq, 1024)

    hs = x.reshape(rows, d)
    hm = meta_tokens.astype(x.dtype)
    row2 = lambda v: v.reshape(1, -1)
    for l in range(depth):
        ffn1 = (row2(ffn1_pre_g[l]), ffn1_w_gate[l].astype(BF16), ffn1_w_up[l].astype(BF16),
                ffn1_w_down[l].astype(BF16), row2(ffn1_post_g[l]))
        hs = _ffn(hs, *ffn1, tm=tm, tf=tf)
        hm = _ffn(hm, *ffn1, tm=N_META, tf=tf)

        w16, w32 = _proj_weights(w_in[l])
        pre = row2(mix_pre_g[l])
        proj16 = functools.partial(_proj, g=pre, w=w16, out_dtype=BF16, slabs_per_step=P16_SLABS_PER_STEP)
        proj32 = functools.partial(_proj, g=pre, w=w32, out_dtype=F32, slabs_per_step=P32_SLABS_PER_STEP)
        tm_proj = _tile(rows, 1024)
        (p16, vt), p32 = proj16(hs, tm=tm_proj, vt_col=4 * SLAB * P16_DV), proj32(hs, tm=tm_proj)
        p16m, p32m = proj16(hm, tm=N_META), proj32(hm, tm=N_META)

        pad_rows = lambda a, n: jnp.pad(a, ((0, 0), (0, n - a.shape[1]), (0, 0)))
        bias = _idx(p16, p32, pad_rows(p16m, IDX_CW)[P16_IK2_SLAB], batch, seq, topk)
        p16m_pad = pad_rows(p16m, LANES)

        def v_t(a):
            vt = jnp.transpose(a[4 * P16_DV:4 * P16_DV + 4], (0, 2, 1)).reshape(DSA_HEADS, DSA_HEAD_DIM, -1)
            ones = jnp.zeros((DSA_HEADS, VT_ROWS - DSA_HEAD_DIM, vt.shape[-1]), vt.dtype).at[:, 0].set(1)
            return jnp.concatenate([vt, ones], axis=1).reshape(DSA_HEADS * VT_ROWS, -1)

        o_dsa = _attn(p16, vt, bias, p16m_pad, v_t(p16m_pad), batch, seq, tq, tk)

        w_gate_pad = jnp.zeros((SLAB, GLA_KEY_WIDTH), F32).at[
            IDX_HEADS:IDX_HEADS + GLA_GATE_RANK].set(w_gate_up[l])
        o_gla = _gla(p32, p16, p32m, p16m, w_gate_pad, row2(b_gate[l]), row2(gla_norm_g[l]), batch, seq)

        wo = w_out[l].astype(BF16)
        hs = _out(hs, o_dsa, o_gla, wo[:DSA_WIDTH], wo[DSA_WIDTH:], row2(mix_post_g[l]), tm)
        ffn2 = (row2(ffn2_pre_g[l]), ffn2_w_gate[l].astype(BF16), ffn2_w_up[l].astype(BF16),
                ffn2_w_down[l].astype(BF16), row2(ffn2_post_g[l]))
        hs = _ffn(hs, *ffn2, tm=tm, tf=tf)
    return hs.reshape(batch, seq, d)
```

```python
import functools

import numpy as np
import jax
import jax.numpy as jnp
from jax import lax
from jax.experimental import pallas as pl
from jax.experimental.pallas import tpu as pltpu

F32, BF16, I32 = jnp.float32, jnp.bfloat16, jnp.int32

N_META = 16
FFN_RES = 0.5
EPS = 1e-6
DSA_HEADS = 8
DSA_HEAD_DIM = 128
DSA_WIDTH = DSA_HEADS * DSA_HEAD_DIM
IDX_HEADS = 16
IDX_DIM = 64
TOPK_MAX = 256
GLA_HEADS = 4
GLA_DK = 128
GLA_DV = 256
GLA_KEY_WIDTH = GLA_HEADS * GLA_DK
GLA_WIDTH = GLA_HEADS * GLA_DV
GLA_GATE_RANK = 16
GLA_TAU = 16.0
GLA_CHUNK = 64
GLA_SUB = 16
GLA_ROWS = 256
IN_SPLITS = (DSA_WIDTH, DSA_WIDTH, DSA_WIDTH, IDX_HEADS * IDX_DIM, IDX_DIM, IDX_HEADS,
             GLA_KEY_WIDTH, GLA_KEY_WIDTH, GLA_WIDTH, GLA_GATE_RANK, GLA_WIDTH)

LANES = 128
SLAB = 256
VMEM_LIMIT_BYTES = 56 * 1024 * 1024

P16_DQ, P16_DK, P16_DV, P16_IQ, P16_GV, P16_IK2_SLAB = 0, 1, 2, 3, 4, 20
P32_GQ_SLAB, P32_GK_SLAB, P32_GOUT_SLAB, P32_SMALL_SLAB = 0, 2, 4, 8
P16_SLABS_PER_STEP, P32_SLABS_PER_STEP = 7, 3
VT_ROWS = DSA_HEAD_DIM + 16

NEG_BIAS = -1e30


def _params(sem):
    return pltpu.CompilerParams(dimension_semantics=sem, vmem_limit_bytes=VMEM_LIMIT_BYTES)


def _dot(a, b):
    return jnp.dot(a, b, preferred_element_type=F32)


def _dot_nt(a, b):
    return lax.dot_general(a, b, (((1,), (1,)), ((), ())), preferred_element_type=F32)


def _dot_tn(a, b):
    return lax.dot_general(a, b, (((0,), (0,)), ((), ())), preferred_element_type=F32)


def _split3(x):
    hi = x.astype(BF16)
    r = x - hi.astype(F32)
    mid = r.astype(BF16)
    lo = (r - mid.astype(F32)).astype(BF16)
    return hi, mid, lo


def _dot_exact_lhs01(l01, x):
    hi, mid, lo = _split3(x)
    return _dot(l01, hi) + _dot(l01, mid) + _dot(l01, lo)


def _dot_f32(a, b):
    ah, am, al = _split3(a)
    bh, bm, bl = _split3(b)
    return (_dot(ah, bh) + (_dot(ah, bm) + _dot(am, bh))
            + (_dot(ah, bl) + _dot(am, bm) + _dot(al, bh)))


def _rms(x, g):
    return x * lax.rsqrt(jnp.mean(x * x, axis=-1, keepdims=True) + EPS) * g


def _ffn_kernel(x_ref, preg_ref, wg_ref, wu_ref, wd_ref, postg_ref, o_ref, xn_ref, *, nf):
    f = pl.program_id(1)

    @pl.when(f == 0)
    def _():
        xn_ref[...] = _rms(x_ref[...], preg_ref[...]).astype(BF16)
        o_ref[...] = jnp.zeros_like(o_ref)

    xn = xn_ref[...]
    g = _dot(xn, wg_ref[...])
    u = _dot(xn, wu_ref[...])
    a = (g * jax.nn.sigmoid(g)) * u
    o_ref[...] += _dot(a.astype(BF16), wd_ref[...])

    @pl.when(f == nf - 1)
    def _():
        o_ref[...] = x_ref[...] + FFN_RES * _rms(o_ref[...], postg_ref[...])


def _ffn(x, pre_g, wg, wu, wd, post_g, tm, tf):
    r, d = x.shape
    nf = wg.shape[1] // tf
    return pl.pallas_call(
        functools.partial(_ffn_kernel, nf=nf),
        out_shape=jax.ShapeDtypeStruct((r, d), F32),
        grid=(r // tm, nf),
        in_specs=[
            pl.BlockSpec((tm, d), lambda i, f: (i, 0)),
            pl.BlockSpec((1, d), lambda i, f: (0, 0)),
            pl.BlockSpec((d, tf), lambda i, f: (0, f)),
            pl.BlockSpec((d, tf), lambda i, f: (0, f)),
            pl.BlockSpec((tf, d), lambda i, f: (f, 0)),
            pl.BlockSpec((1, d), lambda i, f: (0, 0)),
        ],
        out_specs=pl.BlockSpec((tm, d), lambda i, f: (i, 0)),
        scratch_shapes=[pltpu.VMEM((tm, d), BF16)],
        compiler_params=_params(("parallel", "arbitrary")),
        name="ffn",
    )(x, pre_g, wg, wu, wd, post_g)


def _proj_kernel(x_ref, g_ref, w_ref, o_ref, *rest, vt_step, vt_col):
    vt_ref, xn_ref = rest if vt_step is not None else (None, rest[0])

    @pl.when(pl.program_id(1) == 0)
    def _():
        xn_ref[...] = _rms(x_ref[...], g_ref[...]).astype(BF16)

    res = _dot(xn_ref[...], w_ref[...])
    for s in range(o_ref.shape[0]):
        o_ref[s] = res[:, s * SLAB:(s + 1) * SLAB].astype(o_ref.dtype)

    if vt_ref is not None:
        @pl.when(pl.program_id(1) == vt_step)
        def _():
            tm = res.shape[0]
            vt = res[:, vt_col:vt_col + DSA_WIDTH].T.astype(BF16)
            pad = VT_ROWS - DSA_HEAD_DIM
            ones = jnp.where(lax.broadcasted_iota(I32, (pad, tm), 0) == 0, 1.0, 0.0).astype(BF16)
            for h in range(DSA_HEADS):
                vt_ref[h * VT_ROWS:h * VT_ROWS + DSA_HEAD_DIM, :] = vt[h * DSA_HEAD_DIM:(h + 1) * DSA_HEAD_DIM]
                vt_ref[h * VT_ROWS + DSA_HEAD_DIM:(h + 1) * VT_ROWS, :] = ones


def _proj(x, g, w, out_dtype, tm, slabs_per_step, vt_col=None):
    r, d = x.shape
    n_slabs = w.shape[1] // SLAB
    tn = slabs_per_step * SLAB
    out_shape = [jax.ShapeDtypeStruct((n_slabs, r, SLAB), out_dtype)]
    out_specs = [pl.BlockSpec((slabs_per_step, tm, SLAB), lambda i, j: (j, i, 0))]
    vt_step = None
    if vt_col is not None:
        vt_step = vt_col // tn
        assert (vt_col + DSA_WIDTH - 1) // tn == vt_step, "the transposed columns must fall in one grid step"
        out_shape.append(jax.ShapeDtypeStruct((DSA_HEADS * VT_ROWS, r), BF16))
        out_specs.append(pl.BlockSpec((DSA_HEADS * VT_ROWS, tm), lambda i, j: (0, i)))
    out = pl.pallas_call(
        functools.partial(_proj_kernel, vt_step=vt_step, vt_col=None if vt_col is None else vt_col % tn),
        out_shape=out_shape,
        grid=(r // tm, n_slabs // slabs_per_step),
        in_specs=[
            pl.BlockSpec((tm, d), lambda i, j: (i, 0)),
            pl.BlockSpec((1, d), lambda i, j: (0, 0)),
            pl.BlockSpec((d, tn), lambda i, j: (0, j)),
        ],
        out_specs=out_specs,
        scratch_shapes=[pltpu.VMEM((tm, d), BF16)],
        compiler_params=_params(("arbitrary", "arbitrary")),
        name="proj",
    )(x, g, w)
    return out if vt_col is not None else out[0]


IDX_TQ = 256
IDX_CW = 256
BF16_KEY_MIN, BF16_KEY_MAX = -(2 ** 15), 2 ** 15 - 1
STAGE2_BITS = 18


def _key32(pattern):
    return jnp.where(pattern < 0, pattern ^ 0x7FFFFFFF, pattern)


def _pattern_of_key16(k16):
    return lax.shift_left(jnp.where(k16 < 0, k16 ^ 0x7FFF, k16), 16)


def _idx_kernel(iq_ref, sm_ref, ik_ref, ikm_ref, bias_ref, sc_ref, sc16_ref, w_ref, *, seq, topk):
    qi = pl.program_id(1)
    tq, cw = IDX_TQ, IDX_CW
    nt = seq // cw
    n_x = qi + 1
    kf = float(topk)

    w_ref[...] = sm_ref[0][:, 0:LANES].T
    key_pos = lax.broadcasted_iota(I32, (cw, tq), 0)
    q_pos = qi * tq + lax.broadcasted_iota(I32, (cw, tq), 1)

    def score_tile(k2):
        lhs = jnp.concatenate([k2[:, 0:LANES], k2[:, LANES:2 * LANES]], axis=0)
        acc = jnp.zeros((cw, tq), F32)
        for p in range(IDX_HEADS // 2):
            rhs = iq_ref[p // 2][:, (p % 2) * LANES:(p % 2 + 1) * LANES]
            l2 = _dot_nt(lhs, rhs)
            acc = acc + jnp.maximum(l2[0:cw], 0.0) * w_ref[2 * p:2 * p + 1, :]
            acc = acc + jnp.maximum(l2[cw:2 * cw], 0.0) * w_ref[2 * p + 1:2 * p + 2, :]
        return acc

    def x_tile(c):
        k2 = ik_ref[0, pl.ds(pl.multiple_of(c * cw, cw), cw), :]
        acc = jnp.where(c * cw + key_pos <= q_pos, score_tile(k2), -jnp.inf)
        sc_ref[c + 1] = acc
        sc16_ref[c + 1] = acc.astype(BF16)

    def x_tile_pair(p, carry):
        x_tile(2 * p)
        x_tile(2 * p + 1)
        return carry

    lax.fori_loop(0, (n_x + 1) // 2, x_tile_pair, 0)
    acc_m = jnp.where(key_pos < N_META, score_tile(ikm_ref[...]), -jnp.inf)
    sc_ref[0] = acc_m
    sc16_ref[0] = acc_m.astype(BF16)
    sc_ref[n_x + 1] = jnp.full((cw, tq), -jnp.inf, F32)
    sc16_ref[n_x + 1] = jnp.full((cw, tq), -jnp.inf, BF16)
    n_pairs = (n_x + 2) // 2

    one16, zero16 = jnp.ones((cw, tq), BF16), jnp.zeros((cw, tq), BF16)

    def count16(c16):
        def tile_count(c):
            hit = jnp.where(sc16_ref[c] >= c16, one16, zero16).reshape(cw // 16, 16, tq)
            parts = [hit[i] for i in range(cw // 16)]
            while len(parts) > 1:
                parts = [parts[i] + parts[i + 1] for i in range(0, len(parts), 2)]
            return parts[0].astype(F32)

        cnt = lax.fori_loop(0, n_pairs, lambda p, a: a + (tile_count(2 * p) + tile_count(2 * p + 1)),
                            jnp.zeros((16, tq), F32))
        return jnp.sum(cnt, axis=0, keepdims=True)

    def stage1(it, k16):
        cand = k16 + lax.shift_left(jnp.int32(1), 15 - it)
        c32 = lax.bitcast_convert_type(_pattern_of_key16(cand), F32)
        c16 = jnp.broadcast_to(c32, (cw, tq)).astype(BF16)
        return jnp.where(count16(c16) >= kf, cand, k16)

    k16 = lax.fori_loop(0, 16, stage1, jnp.full((1, tq), BF16_KEY_MIN, I32))

    near_zero = jnp.logical_and(k16 >= -2, k16 <= 1)
    reach = jnp.where(near_zero, 2, 1)
    lo = _key32(_pattern_of_key16(jnp.maximum(k16 - reach, BF16_KEY_MIN)))
    hi = _key32(_pattern_of_key16(jnp.minimum(k16 + reach, BF16_KEY_MAX)))
    first_bit = jnp.where(jnp.max(jnp.where(near_zero, 1.0, 0.0)) > 0.5, 0, 1)

    def count32(cf):
        def tile_count(c):
            hit = jnp.where(sc_ref[c] >= cf, 1.0, 0.0)
            return jnp.sum(hit.reshape(cw // 8, 8, tq), axis=0)

        cnt = lax.fori_loop(0, n_pairs, lambda p, a: a + (tile_count(2 * p) + tile_count(2 * p + 1)),
                            jnp.zeros((8, tq), F32))
        return jnp.sum(cnt, axis=0, keepdims=True)

    short = (qi * tq + lax.broadcasted_iota(I32, (1, tq), 1) + 1 + N_META) <= topk

    def unsettled(cnt_t):
        return (jnp.max(jnp.where(short | (cnt_t == kf), 0.0, 1.0)) > 0.5).astype(I32)

    def stage2_cond(st):
        it, _, _, go = st
        return jnp.logical_and(it < STAGE2_BITS, go > 0)

    def stage2(st):
        it, t, cnt_t, _ = st
        cand = t + lax.shift_left(jnp.int32(1), STAGE2_BITS - 1 - it)
        cnt = count32(lax.bitcast_convert_type(jnp.where(cand < 0, cand ^ 0x7FFFFFFF, cand), F32))
        ok = jnp.logical_and(cand < hi, cnt >= kf)
        cnt_t = jnp.where(ok, cnt, cnt_t)
        return it + 1, jnp.where(ok, cand, t), cnt_t, unsettled(cnt_t)

    _, t, cnt_t, tied_any = lax.while_loop(stage2_cond, stage2,
                                           (first_bit.astype(I32), lo, jnp.full((1, tq), -1.0, F32), jnp.int32(1)))
    thr = lax.bitcast_convert_type(jnp.where(t < 0, t ^ 0x7FFFFFFF, t), F32)
    thr = jnp.where(short, jnp.finfo(F32).min, thr)

    def write_bias(bias_of):
        for c in range(nt):
            @pl.when(c < n_x)
            def _(c=c):
                bias_ref[0, c * cw:(c + 1) * cw, :] = bias_of(
                    sc_ref[c + 1], N_META + c * cw + key_pos).astype(BF16)

            @pl.when(c >= n_x)
            def _(c=c):
                bias_ref[0, c * cw:(c + 1) * cw, :] = jnp.full((cw, tq), NEG_BIAS, BF16)

        meta_pos = lax.broadcasted_iota(I32, (LANES, tq), 0)
        bias_ref[0, seq:seq + LANES, :] = bias_of(sc_ref[0, 0:LANES, :], meta_pos).astype(BF16)

    @pl.when(tied_any == 0)
    def _():
        write_bias(lambda s, pos: jnp.where(s >= thr, 0.0, NEG_BIAS))

    @pl.when(tied_any > 0)
    def _():
        def count_where(hit_of):
            def tile_count(slot):
                pos = key_pos + jnp.where(slot == 0, 0, N_META + (slot - 1) * cw)
                return jnp.sum(hit_of(sc_ref[slot], pos).reshape(cw // 8, 8, tq), axis=0)

            cnt = lax.fori_loop(0, 2 * n_pairs, lambda s_, a: a + tile_count(s_), jnp.zeros((8, tq), F32))
            return jnp.sum(cnt, axis=0, keepdims=True)

        need = kf - count_where(lambda s, pos: jnp.where(s > thr, 1.0, 0.0))
        pos_bits = int(seq + N_META).bit_length()

        def pos_bit(i, cut):
            cand = cut | lax.shift_left(jnp.int32(1), pos_bits - 1 - i)
            below = count_where(lambda s, pos: jnp.where(s == thr, jnp.where(pos < cand, 1.0, 0.0), 0.0))
            return jnp.where(below < need, cand, cut)

        cut = lax.fori_loop(0, pos_bits, pos_bit, jnp.zeros((1, tq), I32))
        tied = jnp.logical_and(jnp.logical_not(short), cnt_t != kf)
        cut = jnp.where(tied, cut, jnp.iinfo(I32).max)
        write_bias(lambda s, pos: jnp.where(
            s > thr, 0.0, jnp.where(s == thr, jnp.where(pos <= cut, 0.0, NEG_BIAS), NEG_BIAS)))


def _idx(p16, p32, ik2_meta, batch, seq, topk):
    nq = seq // IDX_TQ
    nt = seq // IDX_CW
    return pl.pallas_call(
        functools.partial(_idx_kernel, seq=seq, topk=topk),
        out_shape=jax.ShapeDtypeStruct((batch, seq + LANES, seq), BF16),
        grid=(batch, nq),
        in_specs=[
            pl.BlockSpec((4, IDX_TQ, SLAB), lambda b, q: (P16_IQ, b * nq + q, 0)),
            pl.BlockSpec((1, IDX_TQ, SLAB), lambda b, q: (P32_SMALL_SLAB, b * nq + q, 0)),
            pl.BlockSpec((1, seq, SLAB), lambda b, q: (P16_IK2_SLAB, b, 0)),
            pl.BlockSpec((IDX_CW, SLAB), lambda b, q: (0, 0)),
        ],
        out_specs=pl.BlockSpec((1, seq + LANES, IDX_TQ), lambda b, q: (b, 0, q)),
        scratch_shapes=[
            pltpu.VMEM((nt + 2, IDX_CW, IDX_TQ), F32),
            pltpu.VMEM((nt + 2, IDX_CW, IDX_TQ), BF16),
            pltpu.VMEM((LANES, IDX_TQ), F32),
        ],
        compiler_params=_params(("parallel", "arbitrary")),
        name="idx",
    )(p16, p32, p16, ik2_meta)


ATTN_AHEAD = 2


def _attn_kernel(qt_ref, kt_ref, q_ref, k_ref, vt_ref, b_ref, bm_ref, km_ref, vtm_ref, o_ref,
                 m_ref, acc_ref, *, tq, tk):
    p = pl.program_id(1)
    qi, ki = qt_ref[p], kt_ref[p]

    def head(ref, h):
        return ref[h // 2][:, (h % 2) * LANES:(h % 2 + 1) * LANES]

    def accumulate(kref, vtref, bias):
        def scores(h):
            return _dot_nt(head(kref, h), head(q_ref, h)) + bias

        pending = [scores(h) for h in range(ATTN_AHEAD)]
        for h in range(DSA_HEADS):
            s = pending.pop(0)
            if h + ATTN_AHEAD < DSA_HEADS:
                pending.append(scores(h + ATTN_AHEAD))
            m_prev = m_ref[h]
            m_new = jnp.maximum(m_prev, jnp.max(s, axis=0, keepdims=True))
            alpha = jnp.exp2(m_prev - m_new)
            pr = jnp.exp2(s - m_new).astype(BF16)
            vt = vtref[h * VT_ROWS:(h + 1) * VT_ROWS, :]
            acc_ref[h] = alpha * acc_ref[h] + _dot(vt, pr)
            m_ref[h] = m_new

    @pl.when(ki == 0)
    def _():
        m_ref[...] = jnp.full(m_ref.shape, -jnp.inf, F32)
        acc_ref[...] = jnp.zeros_like(acc_ref)
        accumulate(km_ref, vtm_ref, bm_ref[0].astype(F32))

    accumulate(k_ref, vt_ref, b_ref[0].astype(F32))

    @pl.when(ki == (qi * tq + tq - 1) // tk)
    def _():
        for h in range(DSA_HEADS):
            hs = slice(h * DSA_HEAD_DIM, (h + 1) * DSA_HEAD_DIM)
            acc = acc_ref[h]
            o_ref[:, hs] = (acc[0:DSA_HEAD_DIM] / acc[DSA_HEAD_DIM:DSA_HEAD_DIM + 1]).T.astype(o_ref.dtype)


def _attn(p16, vt, bias, p16_meta, vt_meta, batch, seq, tq, tk):
    nq, nk = seq // tq, seq // tk
    pairs = [(q, k) for q in range(nq) for k in range((q * tq + tq - 1) // tk + 1)]
    q_tab = jnp.asarray([q for q, _ in pairs], I32)
    k_tab = jnp.asarray([k for _, k in pairs], I32)
    grid_spec = pltpu.PrefetchScalarGridSpec(
        num_scalar_prefetch=2,
        grid=(batch, len(pairs)),
        in_specs=[
            pl.BlockSpec((4, tq, SLAB), lambda b, p, qt, kt: (P16_DQ, b * nq + qt[p], 0)),
            pl.BlockSpec((4, tk, SLAB), lambda b, p, qt, kt: (P16_DK, b * nk + kt[p], 0)),
            pl.BlockSpec((DSA_HEADS * VT_ROWS, tk), lambda b, p, qt, kt: (0, b * nk + kt[p])),
            pl.BlockSpec((1, tk, tq), lambda b, p, qt, kt: (b, kt[p], qt[p])),
            pl.BlockSpec((1, LANES, tq), lambda b, p, qt, kt: (b, seq // LANES, qt[p])),
            pl.BlockSpec((4, LANES, SLAB), lambda b, p, qt, kt: (P16_DK, 0, 0)),
            pl.BlockSpec((DSA_HEADS * VT_ROWS, LANES), lambda b, p, qt, kt: (0, 0)),
        ],
        out_specs=pl.BlockSpec((tq, DSA_WIDTH), lambda b, p, qt, kt: (b * nq + qt[p], 0)),
        scratch_shapes=[
            pltpu.VMEM((DSA_HEADS, 1, tq), F32),
            pltpu.VMEM((DSA_HEADS, VT_ROWS, tq), F32),
        ],
    )
    return pl.pallas_call(
        functools.partial(_attn_kernel, tq=tq, tk=tk),
        out_shape=jax.ShapeDtypeStruct((batch * seq, DSA_WIDTH), BF16),
        grid_spec=grid_spec,
        compiler_params=_params(("parallel", "arbitrary")),
        name="attn",
    )(q_tab, k_tab, p16, p16, vt, bias, bias, p16_meta, vt_meta)


def _log_sigmoid(x):
    return jnp.minimum(x, 0.0) - jnp.log1p(jnp.exp(-jnp.abs(x)))


def _gla_kernel(gq_ref, gk_ref, go_ref, sm_ref, gv_ref, mk_ref, mv_ref, msm_ref, wgate_ref, bgate_ref,
                ng_ref, o_ref, st_ref, bc_ref):
    c = GLA_CHUNK

    def log_decay(small):
        return _log_sigmoid(_dot_f32(small, wgate_ref[...]) + bgate_ref[...]) / GLA_TAU

    def lower_ones(n):
        r = lax.broadcasted_iota(I32, (n, n), 0)
        cc = lax.broadcasted_iota(I32, (n, n), 1)
        return r >= cc

    @pl.when(pl.program_id(1) == 0)
    def _():
        bcm = _dot_exact_lhs01(jnp.where(lower_ones(N_META), 1.0, 0.0).astype(BF16),
                               log_decay(msm_ref[0]))
        for h in range(GLA_HEADS):
            ks = slice(h * GLA_DK, (h + 1) * GLA_DK)
            kd = mk_ref[h // 2][:, (h % 2) * GLA_DK:(h % 2 + 1) * GLA_DK] * jnp.exp(
                bcm[N_META - 1:N_META, ks] - bcm[:, ks])
            st_ref[h] = _dot_tn(mv_ref[h], kd.astype(BF16))

    rows = bc_ref.shape[0]
    r_i = lax.broadcasted_iota(I32, (rows, rows), 0)
    c_i = lax.broadcasted_iota(I32, (rows, rows), 1)
    chunk_tri = jnp.logical_and(r_i >= c_i, r_i // c == c_i // c)
    bc_ref[...] = _dot_exact_lhs01(jnp.where(chunk_tri, 1.0, 0.0).astype(BF16), log_decay(sm_ref[0]))
    causal = lower_ones(c)
    sub = GLA_SUB
    lane = lax.broadcasted_iota(I32, (sub, c), 1)

    for ch in range(rows // c):
        base = ch * c
        for h in range(GLA_HEADS):
            ks = slice(h * GLA_DK, (h + 1) * GLA_DK)
            hsl = slice((h % 2) * GLA_DK, (h % 2 + 1) * GLA_DK)
            q = gq_ref[h // 2][base:base + c, hsl] * (GLA_DK ** -0.5)
            k = gk_ref[h // 2][base:base + c, hsl]
            v = gv_ref[h][base:base + c]
            bc = bc_ref[base:base + c, ks]

            blocks = []
            for s0 in range(0, c, sub):
                q_s, bc_s = q[s0:s0 + sub], bc[s0:s0 + sub]
                diag = jnp.zeros((sub, c), F32)
                for j in range(s0, s0 + sub):
                    e = jnp.exp(jnp.minimum(bc_s - bc_ref[base + j:base + j + 1, ks], 0.0))
                    k_j = gk_ref[h // 2][base + j:base + j + 1, hsl]
                    diag = jnp.where(lane == j, jnp.sum(q_s * k_j * e, axis=1, keepdims=True), diag)
                if s0:
                    b_edge = bc_ref[base + s0 - 1:base + s0, ks]
                    q_t = q_s * jnp.exp(bc_s - b_edge)
                    k_t = k * jnp.exp(jnp.minimum(b_edge - bc, 0.0))
                    diag = jnp.where(lane < s0, _dot_nt(q_t.astype(BF16), k_t.astype(BF16)), diag)
                blocks.append(diag)
            attn = jnp.where(causal, jnp.concatenate(blocks, axis=0), 0.0)

            state = st_ref[h]
            b_last = bc[c - 1:c, :]
            o = _dot_nt((q * jnp.exp(bc)).astype(BF16), state.astype(BF16))
            o = o + _dot(attn.astype(BF16), v)
            kd = k * jnp.exp(b_last - bc)
            st_ref[h] = state * jnp.exp(b_last) + _dot_tn(v, kd.astype(BF16))

            g = go_ref[h][base:base + c]
            y = _rms(o, ng_ref[...]) * (g * jax.nn.sigmoid(g))
            o_ref[base:base + c, h * GLA_DV:(h + 1) * GLA_DV] = y.astype(o_ref.dtype)


def _gla(p32, p16, p32_meta, p16_meta, w_gate_pad, b_gate, norm_g, batch, seq):
    c = _tile(seq, GLA_ROWS)
    ns = seq // c
    row = lambda b, i: b * ns + i
    return pl.pallas_call(
        _gla_kernel,
        out_shape=jax.ShapeDtypeStruct((batch * seq, GLA_WIDTH), BF16),
        grid=(batch, ns),
        in_specs=[
            pl.BlockSpec((2, c, SLAB), lambda b, i: (P32_GQ_SLAB // 2, row(b, i), 0)),
            pl.BlockSpec((2, c, SLAB), lambda b, i: (P32_GK_SLAB // 2, row(b, i), 0)),
            pl.BlockSpec((4, c, SLAB), lambda b, i: (P32_GOUT_SLAB // 4, row(b, i), 0)),
            pl.BlockSpec((1, c, SLAB), lambda b, i: (P32_SMALL_SLAB, row(b, i), 0)),
            pl.BlockSpec((4, c, SLAB), lambda b, i: (P16_GV, row(b, i), 0)),
            pl.BlockSpec((2, N_META, SLAB), lambda b, i: (P32_GK_SLAB // 2, 0, 0)),
            pl.BlockSpec((4, N_META, SLAB), lambda b, i: (P16_GV, 0, 0)),
            pl.BlockSpec((1, N_META, SLAB), lambda b, i: (P32_SMALL_SLAB, 0, 0)),
            pl.BlockSpec((SLAB, GLA_KEY_WIDTH), lambda b, i: (0, 0)),
            pl.BlockSpec((1, GLA_KEY_WIDTH), lambda b, i: (0, 0)),
            pl.BlockSpec((1, GLA_DV), lambda b, i: (0, 0)),
        ],
        out_specs=pl.BlockSpec((c, GLA_WIDTH), lambda b, i: (row(b, i), 0)),
        scratch_shapes=[
            pltpu.VMEM((GLA_HEADS, GLA_DV, GLA_DK), F32),
            pltpu.VMEM((c, GLA_KEY_WIDTH), F32),
        ],
        compiler_params=_params(("parallel", "arbitrary")),
        name="gla",
    )(p32, p32, p32, p32, p16, p32_meta, p16_meta, p32_meta, w_gate_pad, b_gate, norm_g)


def _out_kernel(h_ref, a_ref, g_ref, wa_ref, wg_ref, pg_ref, o_ref):
    m = _dot(a_ref[...], wa_ref[...]) + _dot(g_ref[...], wg_ref[...])
    o_ref[...] = h_ref[...] + _rms(m, pg_ref[...])


def _out(hs, o_dsa, o_gla, w_dsa, w_gla, post_g, tm):
    r, d = hs.shape
    return pl.pallas_call(
        _out_kernel,
        out_shape=jax.ShapeDtypeStruct((r, d), F32),
        grid=(r // tm,),
        in_specs=[
            pl.BlockSpec((tm, d), lambda i: (i, 0)),
            pl.BlockSpec((tm, DSA_WIDTH), lambda i: (i, 0)),
            pl.BlockSpec((tm, GLA_WIDTH), lambda i: (i, 0)),
            pl.BlockSpec((DSA_WIDTH, d), lambda i: (0, 0)),
            pl.BlockSpec((GLA_WIDTH, d), lambda i: (0, 0)),
            pl.BlockSpec((1, d), lambda i: (0, 0)),
        ],
        out_specs=pl.BlockSpec((tm, d), lambda i: (i, 0)),
        compiler_params=_params(("parallel",)),
        name="outproj",
    )(hs, o_dsa, o_gla, w_dsa, w_gla, post_g)


def _tile(n, pref):
    t = min(n, pref)
    while n % t:
        t //= 2
    return t


def _proj_weights(w_in):
    d = w_in.shape[0]
    offs = np.cumsum((0,) + IN_SPLITS)
    dq, dk, dv, iq, ik, iw, gq, gk, gv, glow, gout = (w_in[:, offs[i]:offs[i + 1]] for i in range(11))
    z = lambda n: jnp.zeros((d, n), w_in.dtype)
    ik2 = jnp.concatenate([ik, z(IDX_DIM), z(IDX_DIM), ik], axis=1)
    q_scale = DSA_HEAD_DIM ** -0.5 * float(np.log2(np.e))
    w16 = jnp.concatenate([dq * q_scale, dk, dv, iq * (IDX_DIM ** -0.5), gv, ik2], axis=1)
    small = jnp.concatenate([iw * (IDX_HEADS ** -0.5), glow, z(SLAB - IDX_HEADS - GLA_GATE_RANK)], axis=1)
    w32 = jnp.concatenate([gq, gk, gout, small], axis=1)
    return w16.astype(BF16), w32.astype(BF16)


def kernel(x, meta_tokens, ffn1_pre_g, ffn1_w_gate, ffn1_w_up, ffn1_w_down, ffn1_post_g, mix_pre_g, w_in, w_gate_up, b_gate, gla_norm_g, w_out, mix_post_g, ffn2_pre_g, ffn2_w_gate, ffn2_w_up, ffn2_w_down, ffn2_post_g):
    batch, seq, d = x.shape
    depth = w_in.shape[0]
    assert depth == 1, "the meta rows skip the mixer, which is only valid for the last layer"
    assert seq % (2 * IDX_CW) == 0 and meta_tokens.shape[0] == N_META, "score tiles are visited in pairs"
    d_ff = ffn1_w_gate.shape[-1]
    topk = min(TOPK_MAX, seq // 4)
    rows = batch * seq
    tm = _tile(rows, 512)
    tf = _tile(d_ff, 512)
    tq = _tile(seq, 256)
    tk = _tile(seq, 1024)

    hs = x.reshape(rows, d)
    hm = meta_tokens.astype(x.dtype)
    row2 = lambda v: v.reshape(1, -1)
    for l in range(depth):
        ffn1 = (row2(ffn1_pre_g[l]), ffn1_w_gate[l].astype(BF16), ffn1_w_up[l].astype(BF16),
                ffn1_w_down[l].astype(BF16), row2(ffn1_post_g[l]))
        hs = _ffn(hs, *ffn1, tm=tm, tf=tf)
        hm = _ffn(hm, *ffn1, tm=N_META, tf=tf)

        w16, w32 = _proj_weights(w_in[l])
        pre = row2(mix_pre_g[l])
        proj16 = functools.partial(_proj, g=pre, w=w16, out_dtype=BF16, slabs_per_step=P16_SLABS_PER_STEP)
        proj32 = functools.partial(_proj, g=pre, w=w32, out_dtype=F32, slabs_per_step=P32_SLABS_PER_STEP)
        tm_proj = _tile(rows, 1024)
        (p16, vt), p32 = proj16(hs, tm=tm_proj, vt_col=4 * SLAB * P16_DV), proj32(hs, tm=tm_proj)
        p16m, p32m = proj16(hm, tm=N_META), proj32(hm, tm=N_META)

        pad_rows = lambda a, n: jnp.pad(a, ((0, 0), (0, n - a.shape[1]), (0, 0)))
        bias = _idx(p16, p32, pad_rows(p16m, IDX_CW)[P16_IK2_SLAB], batch, seq, topk)
        p16m_pad = pad_rows(p16m, LANES)

        def v_t(a):
            vt = jnp.transpose(a[4 * P16_DV:4 * P16_DV + 4], (0, 2, 1)).reshape(DSA_HEADS, DSA_HEAD_DIM, -1)
            ones = jnp.zeros((DSA_HEADS, VT_ROWS - DSA_HEAD_DIM, vt.shape[-1]), vt.dtype).at[:, 0].set(1)
            return jnp.concatenate([vt, ones], axis=1).reshape(DSA_HEADS * VT_ROWS, -1)

        o_dsa = _attn(p16, vt, bias, p16m_pad, v_t(p16m_pad), batch, seq, tq, tk)

        w_gate_pad = jnp.zeros((SLAB, GLA_KEY_WIDTH), F32).at[
            IDX_HEADS:IDX_HEADS + GLA_GATE_RANK].set(w_gate_up[l])
        o_gla = _gla(p32, p16, p32m, p16m, w_gate_pad, row2(b_gate[l]), row2(gla_norm_g[l]), batch, seq)

        wo = w_out[l].astype(BF16)
        hs = _out(hs, o_dsa, o_gla, wo[:DSA_WIDTH], wo[DSA_WIDTH:], row2(mix_post_g[l]), tm)
        ffn2 = (row2(ffn2_pre_g[l]), ffn2_w_gate[l].astype(BF16), ffn2_w_up[l].astype(BF16),
                ffn2_w_down[l].astype(BF16), row2(ffn2_post_g[l]))
        hs = _ffn(hs, *ffn2, tm=tm, tf=tf)
    return hs.reshape(batch, seq, d)
```

```python
import functools

import numpy as np
import jax
import jax.numpy as jnp
from jax import lax
from jax.experimental import pallas as pl
from jax.experimental.pallas import tpu as pltpu

F32, BF16, I32 = jnp.float32, jnp.bfloat16, jnp.int32

N_META = 16
FFN_RES = 0.5
EPS = 1e-6
DSA_HEADS = 8
DSA_HEAD_DIM = 128
DSA_WIDTH = DSA_HEADS * DSA_HEAD_DIM
IDX_HEADS = 16
IDX_DIM = 64
TOPK_MAX = 256
GLA_HEADS = 4
GLA_DK = 128
GLA_DV = 256
GLA_KEY_WIDTH = GLA_HEADS * GLA_DK
GLA_WIDTH = GLA_HEADS * GLA_DV
GLA_GATE_RANK = 16
GLA_TAU = 16.0
GLA_CHUNK = 64
GLA_SUB = 16
GLA_ROWS = 256
IN_SPLITS = (DSA_WIDTH, DSA_WIDTH, DSA_WIDTH, IDX_HEADS * IDX_DIM, IDX_DIM, IDX_HEADS,
             GLA_KEY_WIDTH, GLA_KEY_WIDTH, GLA_WIDTH, GLA_GATE_RANK, GLA_WIDTH)

LANES = 128
SLAB = 256
VMEM_LIMIT_BYTES = 56 * 1024 * 1024

P16_DQ, P16_DK, P16_DV, P16_IQ, P16_GV, P16_IK2_SLAB = 0, 1, 2, 3, 4, 20
P32_GQ_SLAB, P32_GK_SLAB, P32_GOUT_SLAB, P32_SMALL_SLAB = 0, 2, 4, 8
P16_SLABS_PER_STEP, P32_SLABS_PER_STEP = 7, 3
VT_ROWS = DSA_HEAD_DIM + 16

NEG_BIAS = -1e30


def _params(sem):
    return pltpu.CompilerParams(dimension_semantics=sem, vmem_limit_bytes=VMEM_LIMIT_BYTES)


def _dot(a, b):
    return jnp.dot(a, b, preferred_element_type=F32)


def _dot_nt(a, b):
    return lax.dot_general(a, b, (((1,), (1,)), ((), ())), preferred_element_type=F32)


def _dot_tn(a, b):
    return lax.dot_general(a, b, (((0,), (0,)), ((), ())), preferred_element_type=F32)


def _split3(x):
    hi = x.astype(BF16)
    r = x - hi.astype(F32)
    mid = r.astype(BF16)
    lo = (r - mid.astype(F32)).astype(BF16)
    return hi, mid, lo


def _dot_exact_lhs01(l01, x):
    hi, mid, lo = _split3(x)
    return _dot(l01, hi) + _dot(l01, mid) + _dot(l01, lo)


def _dot_f32(a, b):
    ah, am, al = _split3(a)
    bh, bm, bl = _split3(b)
    return (_dot(ah, bh) + (_dot(ah, bm) + _dot(am, bh))
            + (_dot(ah, bl) + _dot(am, bm) + _dot(al, bh)))


def _rms(x, g):
    return x * lax.rsqrt(jnp.mean(x * x, axis=-1, keepdims=True) + EPS) * g


def _ffn_kernel(x_ref, preg_ref, wg_ref, wu_ref, wd_ref, postg_ref, o_ref, xn_ref, *, nf):
    f = pl.program_id(1)

    @pl.when(f == 0)
    def _():
        xn_ref[...] = _rms(x_ref[...], preg_ref[...]).astype(BF16)
        o_ref[...] = jnp.zeros_like(o_ref)

    xn = xn_ref[...]
    g = _dot(xn, wg_ref[...])
    u = _dot(xn, wu_ref[...])
    a = (g * jax.nn.sigmoid(g)) * u
    o_ref[...] += _dot(a.astype(BF16), wd_ref[...].astype(BF16))

    @pl.when(f == nf - 1)
    def _():
        o_ref[...] = x_ref[...] + FFN_RES * _rms(o_ref[...], postg_ref[...])


def _ffn(x, pre_g, wg, wu, wd, post_g, tm, tf):
    r, d = x.shape
    nf = wg.shape[1] // tf
    return pl.pallas_call(
        functools.partial(_ffn_kernel, nf=nf),
        out_shape=jax.ShapeDtypeStruct((r, d), F32),
        grid=(r // tm, nf),
        in_specs=[
            pl.BlockSpec((tm, d), lambda i, f: (i, 0)),
            pl.BlockSpec((1, d), lambda i, f: (0, 0)),
            pl.BlockSpec((d, tf), lambda i, f: (0, f)),
            pl.BlockSpec((d, tf), lambda i, f: (0, f)),
            pl.BlockSpec((tf, d), lambda i, f: (f, 0)),
            pl.BlockSpec((1, d), lambda i, f: (0, 0)),
        ],
        out_specs=pl.BlockSpec((tm, d), lambda i, f: (i, 0)),
        scratch_shapes=[pltpu.VMEM((tm, d), BF16)],
        compiler_params=_params(("parallel", "arbitrary")),
        name="ffn",
    )(x, pre_g, wg, wu, wd, post_g)


def _proj_kernel(x_ref, g_ref, w_ref, o_ref, *rest, vt_step, vt_col):
    vt_ref, xn_ref = rest if vt_step is not None else (None, rest[0])

    @pl.when(pl.program_id(1) == 0)
    def _():
        xn_ref[...] = _rms(x_ref[...], g_ref[...]).astype(BF16)

    res = _dot(xn_ref[...], w_ref[...])
    for s in range(o_ref.shape[0]):
        o_ref[s] = res[:, s * SLAB:(s + 1) * SLAB].astype(o_ref.dtype)

    if vt_ref is not None:
        @pl.when(pl.program_id(1) == vt_step)
        def _():
            tm = res.shape[0]
            vt = res[:, vt_col:vt_col + DSA_WIDTH].T.astype(BF16)
            pad = VT_ROWS - DSA_HEAD_DIM
            ones = jnp.where(lax.broadcasted_iota(I32, (pad, tm), 0) == 0, 1.0, 0.0).astype(BF16)
            for h in range(DSA_HEADS):
                vt_ref[h * VT_ROWS:h * VT_ROWS + DSA_HEAD_DIM, :] = vt[h * DSA_HEAD_DIM:(h + 1) * DSA_HEAD_DIM]
                vt_ref[h * VT_ROWS + DSA_HEAD_DIM:(h + 1) * VT_ROWS, :] = ones


def _proj(x, g, w, out_dtype, tm, slabs_per_step, vt_col=None):
    r, d = x.shape
    n_slabs = w.shape[1] // SLAB
    tn = slabs_per_step * SLAB
    out_shape = [jax.ShapeDtypeStruct((n_slabs, r, SLAB), out_dtype)]
    out_specs = [pl.BlockSpec((slabs_per_step, tm, SLAB), lambda i, j: (j, i, 0))]
    vt_step = None
    if vt_col is not None:
        vt_step = vt_col // tn
        assert (vt_col + DSA_WIDTH - 1) // tn == vt_step, "the transposed columns must fall in one grid step"
        out_shape.append(jax.ShapeDtypeStruct((DSA_HEADS * VT_ROWS, r), BF16))
        out_specs.append(pl.BlockSpec((DSA_HEADS * VT_ROWS, tm), lambda i, j: (0, i)))
    out = pl.pallas_call(
        functools.partial(_proj_kernel, vt_step=vt_step, vt_col=None if vt_col is None else vt_col % tn),
        out_shape=out_shape,
        grid=(r // tm, n_slabs // slabs_per_step),
        in_specs=[
            pl.BlockSpec((tm, d), lambda i, j: (i, 0)),
            pl.BlockSpec((1, d), lambda i, j: (0, 0)),
            pl.BlockSpec((d, tn), lambda i, j: (0, j)),
        ],
        out_specs=out_specs,
        scratch_shapes=[pltpu.VMEM((tm, d), BF16)],
        compiler_params=_params(("arbitrary", "arbitrary")),
        name="proj",
    )(x, g, w)
    return out if vt_col is not None else out[0]


IDX_TQ = 256
IDX_CW = 256
BF16_KEY_MIN, BF16_KEY_MAX = -(2 ** 15), 2 ** 15 - 1
STAGE2_BITS = 18


def _key32(pattern):
    return jnp.where(pattern < 0, pattern ^ 0x7FFFFFFF, pattern)


def _pattern_of_key16(k16):
    return lax.shift_left(jnp.where(k16 < 0, k16 ^ 0x7FFF, k16), 16)


def _idx_kernel(iq_ref, sm_ref, ik_ref, ikm_ref, bias_ref, sc_ref, sc16_ref, w_ref, *, seq, topk):
    qi = pl.program_id(1)
    tq, cw = IDX_TQ, IDX_CW
    nt = seq // cw
    n_x = qi + 1
    kf = float(topk)

    w_ref[...] = sm_ref[0][:, 0:LANES].T
    key_pos = lax.broadcasted_iota(I32, (cw, tq), 0)
    q_pos = qi * tq + lax.broadcasted_iota(I32, (cw, tq), 1)

    def score_tile(k2):
        lhs = jnp.concatenate([k2[:, 0:LANES], k2[:, LANES:2 * LANES]], axis=0)
        acc = jnp.zeros((cw, tq), F32)
        for p in range(IDX_HEADS // 2):
            rhs = iq_ref[p // 2][:, (p % 2) * LANES:(p % 2 + 1) * LANES]
            l2 = _dot_nt(lhs, rhs)
            acc = acc + jnp.maximum(l2[0:cw], 0.0) * w_ref[2 * p:2 * p + 1, :]
            acc = acc + jnp.maximum(l2[cw:2 * cw], 0.0) * w_ref[2 * p + 1:2 * p + 2, :]
        return acc

    def x_tile(c):
        k2 = ik_ref[0, pl.ds(pl.multiple_of(c * cw, cw), cw), :]
        acc = jnp.where(c * cw + key_pos <= q_pos, score_tile(k2), -jnp.inf)
        sc_ref[c + 1] = acc
        sc16_ref[c + 1] = acc.astype(BF16)

    def x_tile_pair(p, carry):
        x_tile(2 * p)
        x_tile(2 * p + 1)
        return carry

    lax.fori_loop(0, (n_x + 1) // 2, x_tile_pair, 0)
    acc_m = jnp.where(key_pos < N_META, score_tile(ikm_ref[...]), -jnp.inf)
    sc_ref[0] = acc_m
    sc16_ref[0] = acc_m.astype(BF16)
    sc_ref[n_x + 1] = jnp.full((cw, tq), -jnp.inf, F32)
    sc16_ref[n_x + 1] = jnp.full((cw, tq), -jnp.inf, BF16)
    n_pairs = (n_x + 2) // 2

    one16, zero16 = jnp.ones((cw, tq), BF16), jnp.zeros((cw, tq), BF16)

    def count16(c16):
        def tile_count(c):
            hit = jnp.where(sc16_ref[c] >= c16, one16, zero16).reshape(cw // 16, 16, tq)
            parts = [hit[i] for i in range(cw // 16)]
            while len(parts) > 1:
                parts = [parts[i] + parts[i + 1] for i in range(0, len(parts), 2)]
            return parts[0].astype(F32)

        cnt = lax.fori_loop(0, n_pairs, lambda p, a: a + (tile_count(2 * p) + tile_count(2 * p + 1)),
                            jnp.zeros((16, tq), F32))
        return jnp.sum(cnt, axis=0, keepdims=True)

    def stage1(it, k16):
        cand = k16 + lax.shift_left(jnp.int32(1), 15 - it)
        c32 = lax.bitcast_convert_type(_pattern_of_key16(cand), F32)
        c16 = jnp.broadcast_to(c32, (cw, tq)).astype(BF16)
        return jnp.where(count16(c16) >= kf, cand, k16)

    k16 = lax.fori_loop(0, 16, stage1, jnp.full((1, tq), BF16_KEY_MIN, I32))

    near_zero = jnp.logical_and(k16 >= -2, k16 <= 1)
    reach = jnp.where(near_zero, 2, 1)
    lo = _key32(_pattern_of_key16(jnp.maximum(k16 - reach, BF16_KEY_MIN)))
    hi = _key32(_pattern_of_key16(jnp.minimum(k16 + reach, BF16_KEY_MAX)))
    first_bit = jnp.where(jnp.max(jnp.where(near_zero, 1.0, 0.0)) > 0.5, 0, 1)

    def count32(cf):
        def tile_count(c):
            hit = jnp.where(sc_ref[c] >= cf, 1.0, 0.0)
            return jnp.sum(hit.reshape(cw // 8, 8, tq), axis=0)

        cnt = lax.fori_loop(0, n_pairs, lambda p, a: a + (tile_count(2 * p) + tile_count(2 * p + 1)),
                            jnp.zeros((8, tq), F32))
        return jnp.sum(cnt, axis=0, keepdims=True)

    short = (qi * tq + lax.broadcasted_iota(I32, (1, tq), 1) + 1 + N_META) <= topk

    def unsettled(cnt_t):
        return (jnp.max(jnp.where(short | (cnt_t == kf), 0.0, 1.0)) > 0.5).astype(I32)

    def stage2_cond(st):
        it, _, _, go = st
        return jnp.logical_and(it < STAGE2_BITS, go > 0)

    def stage2(st):
        it, t, cnt_t, _ = st
        cand = t + lax.shift_left(jnp.int32(1), STAGE2_BITS - 1 - it)
        cnt = count32(lax.bitcast_convert_type(jnp.where(cand < 0, cand ^ 0x7FFFFFFF, cand), F32))
        ok = jnp.logical_and(cand < hi, cnt >= kf)
        cnt_t = jnp.where(ok, cnt, cnt_t)
        return it + 1, jnp.where(ok, cand, t), cnt_t, unsettled(cnt_t)

    _, t, cnt_t, tied_any = lax.while_loop(stage2_cond, stage2,
                                           (first_bit.astype(I32), lo, jnp.full((1, tq), -1.0, F32), jnp.int32(1)))
    thr = lax.bitcast_convert_type(jnp.where(t < 0, t ^ 0x7FFFFFFF, t), F32)
    thr = jnp.where(short, jnp.finfo(F32).min, thr)

    def write_bias(bias_of):
        for c in range(nt):
            @pl.when(c < n_x)
            def _(c=c):
                bias_ref[0, c * cw:(c + 1) * cw, :] = bias_of(
                    sc_ref[c + 1], N_META + c * cw + key_pos).astype(BF16)

            @pl.when(c >= n_x)
            def _(c=c):
                bias_ref[0, c * cw:(c + 1) * cw, :] = jnp.full((cw, tq), NEG_BIAS, BF16)

        meta_pos = lax.broadcasted_iota(I32, (LANES, tq), 0)
        bias_ref[0, seq:seq + LANES, :] = bias_of(sc_ref[0, 0:LANES, :], meta_pos).astype(BF16)

    @pl.when(tied_any == 0)
    def _():
        write_bias(lambda s, pos: jnp.where(s >= thr, 0.0, NEG_BIAS))

    @pl.when(tied_any > 0)
    def _():
        def count_where(hit_of):
            def tile_count(slot):
                pos = key_pos + jnp.where(slot == 0, 0, N_META + (slot - 1) * cw)
                return jnp.sum(hit_of(sc_ref[slot], pos).reshape(cw // 8, 8, tq), axis=0)

            cnt = lax.fori_loop(0, 2 * n_pairs, lambda s_, a: a + tile_count(s_), jnp.zeros((8, tq), F32))
            return jnp.sum(cnt, axis=0, keepdims=True)

        need = kf - count_where(lambda s, pos: jnp.where(s > thr, 1.0, 0.0))
        pos_bits = int(seq + N_META).bit_length()

        def pos_bit(i, cut):
            cand = cut | lax.shift_left(jnp.int32(1), pos_bits - 1 - i)
            below = count_where(lambda s, pos: jnp.where(s == thr, jnp.where(pos < cand, 1.0, 0.0), 0.0))
            return jnp.where(below < need, cand, cut)

        cut = lax.fori_loop(0, pos_bits, pos_bit, jnp.zeros((1, tq), I32))
        tied = jnp.logical_and(jnp.logical_not(short), cnt_t != kf)
        cut = jnp.where(tied, cut, jnp.iinfo(I32).max)
        write_bias(lambda s, pos: jnp.where(
            s > thr, 0.0, jnp.where(s == thr, jnp.where(pos <= cut, 0.0, NEG_BIAS), NEG_BIAS)))


def _idx(p16, p32, ik2_meta, batch, seq, topk):
    nq = seq // IDX_TQ
    nt = seq // IDX_CW
    return pl.pallas_call(
        functools.partial(_idx_kernel, seq=seq, topk=topk),
        out_shape=jax.ShapeDtypeStruct((batch, seq + LANES, seq), BF16),
        grid=(batch, nq),
        in_specs=[
            pl.BlockSpec((4, IDX_TQ, SLAB), lambda b, q: (P16_IQ, b * nq + q, 0)),
            pl.BlockSpec((1, IDX_TQ, SLAB), lambda b, q: (P32_SMALL_SLAB, b * nq + q, 0)),
            pl.BlockSpec((1, seq, SLAB), lambda b, q: (P16_IK2_SLAB, b, 0)),
            pl.BlockSpec((IDX_CW, SLAB), lambda b, q: (0, 0)),
        ],
        out_specs=pl.BlockSpec((1, seq + LANES, IDX_TQ), lambda b, q: (b, 0, q)),
        scratch_shapes=[
            pltpu.VMEM((nt + 2, IDX_CW, IDX_TQ), F32),
            pltpu.VMEM((nt + 2, IDX_CW, IDX_TQ), BF16),
            pltpu.VMEM((LANES, IDX_TQ), F32),
        ],
        compiler_params=_params(("parallel", "arbitrary")),
        name="idx",
    )(p16, p32, p16, ik2_meta)


ATTN_AHEAD = 2


def _attn_kernel(qt_ref, kt_ref, q_ref, k_ref, vt_ref, b_ref, bm_ref, km_ref, vtm_ref, o_ref,
                 m_ref, acc_ref, *, tq, tk):
    p = pl.program_id(1)
    qi, ki = qt_ref[p], kt_ref[p]

    def head(ref, h):
        return ref[h // 2][:, (h % 2) * LANES:(h % 2 + 1) * LANES]

    def accumulate(kref, vtref, bias):
        def scores(h):
            return _dot_nt(head(kref, h), head(q_ref, h)) + bias

        pending = [scores(h) for h in range(ATTN_AHEAD)]
        for h in range(DSA_HEADS):
            s = pending.pop(0)
            if h + ATTN_AHEAD < DSA_HEADS:
                pending.append(scores(h + ATTN_AHEAD))
            m_prev = m_ref[h]
            m_new = jnp.maximum(m_prev, jnp.max(s, axis=0, keepdims=True))
            alpha = jnp.exp2(m_prev - m_new)
            pr = jnp.exp2(s - m_new).astype(BF16)
            vt = vtref[h * VT_ROWS:(h + 1) * VT_ROWS, :]
            acc_ref[h] = alpha * acc_ref[h] + _dot(vt, pr)
            m_ref[h] = m_new

    @pl.when(ki == 0)
    def _():
        m_ref[...] = jnp.full(m_ref.shape, -jnp.inf, F32)
        acc_ref[...] = jnp.zeros_like(acc_ref)
        accumulate(km_ref, vtm_ref, bm_ref[0].astype(F32))

    accumulate(k_ref, vt_ref, b_ref[0].astype(F32))

    @pl.when(ki == (qi * tq + tq - 1) // tk)
    def _():
        for h in range(DSA_HEADS):
            hs = slice(h * DSA_HEAD_DIM, (h + 1) * DSA_HEAD_DIM)
            acc = acc_ref[h]
            o_ref[:, hs] = (acc[0:DSA_HEAD_DIM] / acc[DSA_HEAD_DIM:DSA_HEAD_DIM + 1]).T.astype(o_ref.dtype)


def _attn(p16, vt, bias, p16_meta, vt_meta, batch, seq, tq, tk):
    nq, nk = seq // tq, seq // tk
    pairs = [(q, k) for q in range(nq) for k in range((q * tq + tq - 1) // tk + 1)]
    q_tab = jnp.asarray([q for q, _ in pairs], I32)
    k_tab = jnp.asarray([k for _, k in pairs], I32)
    grid_spec = pltpu.PrefetchScalarGridSpec(
        num_scalar_prefetch=2,
        grid=(batch, len(pairs)),
        in_specs=[
            pl.BlockSpec((4, tq, SLAB), lambda b, p, qt, kt: (P16_DQ, b * nq + qt[p], 0)),
            pl.BlockSpec((4, tk, SLAB), lambda b, p, qt, kt: (P16_DK, b * nk + kt[p], 0)),
            pl.BlockSpec((DSA_HEADS * VT_ROWS, tk), lambda b, p, qt, kt: (0, b * nk + kt[p])),
            pl.BlockSpec((1, tk, tq), lambda b, p, qt, kt: (b, kt[p], qt[p])),
            pl.BlockSpec((1, LANES, tq), lambda b, p, qt, kt: (b, seq // LANES, qt[p])),
            pl.BlockSpec((4, LANES, SLAB), lambda b, p, qt, kt: (P16_DK, 0, 0)),
            pl.BlockSpec((DSA_HEADS * VT_ROWS, LANES), lambda b, p, qt, kt: (0, 0)),
        ],
        out_specs=pl.BlockSpec((tq, DSA_WIDTH), lambda b, p, qt, kt: (b * nq + qt[p], 0)),
        scratch_shapes=[
            pltpu.VMEM((DSA_HEADS, 1, tq), F32),
            pltpu.VMEM((DSA_HEADS, VT_ROWS, tq), F32),
        ],
    )
    return pl.pallas_call(
        functools.partial(_attn_kernel, tq=tq, tk=tk),
        out_shape=jax.ShapeDtypeStruct((batch * seq, DSA_WIDTH), BF16),
        grid_spec=grid_spec,
        compiler_params=_params(("parallel", "arbitrary")),
        name="attn",
    )(q_tab, k_tab, p16, p16, vt, bias, bias, p16_meta, vt_meta)


def _log_sigmoid(x):
    return jnp.minimum(x, 0.0) - jnp.log1p(jnp.exp(-jnp.abs(x)))


def _gla_kernel(gq_ref, gk_ref, go_ref, sm_ref, gv_ref, mk_ref, mv_ref, msm_ref, wgate_ref, bgate_ref,
                ng_ref, o_ref, st_ref, bc_ref):
    c = GLA_CHUNK

    def log_decay(small):
        return _log_sigmoid(_dot_f32(small, wgate_ref[...]) + bgate_ref[...]) / GLA_TAU

    def lower_ones(n):
        r = lax.broadcasted_iota(I32, (n, n), 0)
        cc = lax.broadcasted_iota(I32, (n, n), 1)
        return r >= cc

    @pl.when(pl.program_id(1) == 0)
    def _():
        bcm = _dot_exact_lhs01(jnp.where(lower_ones(N_META), 1.0, 0.0).astype(BF16),
                               log_decay(msm_ref[0]))
        for h in range(GLA_HEADS):
            ks = slice(h * GLA_DK, (h + 1) * GLA_DK)
            kd = mk_ref[h // 2][:, (h % 2) * GLA_DK:(h % 2 + 1) * GLA_DK] * jnp.exp(
                bcm[N_META - 1:N_META, ks] - bcm[:, ks])
            st_ref[h] = _dot_tn(mv_ref[h], kd.astype(BF16))

    rows = bc_ref.shape[0]
    r_i = lax.broadcasted_iota(I32, (rows, rows), 0)
    c_i = lax.broadcasted_iota(I32, (rows, rows), 1)
    chunk_tri = jnp.logical_and(r_i >= c_i, r_i // c == c_i // c)
    bc_ref[...] = _dot_exact_lhs01(jnp.where(chunk_tri, 1.0, 0.0).astype(BF16), log_decay(sm_ref[0]))
    causal = lower_ones(c)
    sub = GLA_SUB
    lane = lax.broadcasted_iota(I32, (sub, c), 1)

    for ch in range(rows // c):
        base = ch * c
        for h in range(GLA_HEADS):
            ks = slice(h * GLA_DK, (h + 1) * GLA_DK)
            hsl = slice((h % 2) * GLA_DK, (h % 2 + 1) * GLA_DK)
            q = gq_ref[h // 2][base:base + c, hsl] * (GLA_DK ** -0.5)
            k = gk_ref[h // 2][base:base + c, hsl]
            v = gv_ref[h][base:base + c]
            bc = bc_ref[base:base + c, ks]

            blocks = []
            for s0 in range(0, c, sub):
                q_s, bc_s = q[s0:s0 + sub], bc[s0:s0 + sub]
                diag = jnp.zeros((sub, c), F32)
                for j in range(s0, s0 + sub):
                    e = jnp.exp(jnp.minimum(bc_s - bc_ref[base + j:base + j + 1, ks], 0.0))
                    k_j = gk_ref[h // 2][base + j:base + j + 1, hsl]
                    diag = jnp.where(lane == j, jnp.sum(q_s * k_j * e, axis=1, keepdims=True), diag)
                if s0:
                    b_edge = bc_ref[base + s0 - 1:base + s0, ks]
                    q_t = q_s * jnp.exp(bc_s - b_edge)
                    k_t = k * jnp.exp(jnp.minimum(b_edge - bc, 0.0))
                    diag = jnp.where(lane < s0, _dot_nt(q_t.astype(BF16), k_t.astype(BF16)), diag)
                blocks.append(diag)
            attn = jnp.where(causal, jnp.concatenate(blocks, axis=0), 0.0)

            state = st_ref[h]
            b_last = bc[c - 1:c, :]
            o = _dot_nt((q * jnp.exp(bc)).astype(BF16), state.astype(BF16))
            o = o + _dot(attn.astype(BF16), v)
            kd = k * jnp.exp(b_last - bc)
            st_ref[h] = state * jnp.exp(b_last) + _dot_tn(v, kd.astype(BF16))

            g = go_ref[h][base:base + c]
            y = _rms(o, ng_ref[...]) * (g * jax.nn.sigmoid(g))
            o_ref[base:base + c, h * GLA_DV:(h + 1) * GLA_DV] = y.astype(o_ref.dtype)


def _gla(p32, p16, p32_meta, p16_meta, w_gate_pad, b_gate, norm_g, batch, seq):
    c = _tile(seq, GLA_ROWS)
    ns = seq // c
    row = lambda b, i: b * ns + i
    return pl.pallas_call(
        _gla_kernel,
        out_shape=jax.ShapeDtypeStruct((batch * seq, GLA_WIDTH), BF16),
        grid=(batch, ns),
        in_specs=[
            pl.BlockSpec((2, c, SLAB), lambda b, i: (P32_GQ_SLAB // 2, row(b, i), 0)),
            pl.BlockSpec((2, c, SLAB), lambda b, i: (P32_GK_SLAB // 2, row(b, i), 0)),
            pl.BlockSpec((4, c, SLAB), lambda b, i: (P32_GOUT_SLAB // 4, row(b, i), 0)),
            pl.BlockSpec((1, c, SLAB), lambda b, i: (P32_SMALL_SLAB, row(b, i), 0)),
            pl.BlockSpec((4, c, SLAB), lambda b, i: (P16_GV, row(b, i), 0)),
            pl.BlockSpec((2, N_META, SLAB), lambda b, i: (P32_GK_SLAB // 2, 0, 0)),
            pl.BlockSpec((4, N_META, SLAB), lambda b, i: (P16_GV, 0, 0)),
            pl.BlockSpec((1, N_META, SLAB), lambda b, i: (P32_SMALL_SLAB, 0, 0)),
            pl.BlockSpec((SLAB, GLA_KEY_WIDTH), lambda b, i: (0, 0)),
            pl.BlockSpec((1, GLA_KEY_WIDTH), lambda b, i: (0, 0)),
            pl.BlockSpec((1, GLA_DV), lambda b, i: (0, 0)),
        ],
        out_specs=pl.BlockSpec((c, GLA_WIDTH), lambda b, i: (row(b, i), 0)),
        scratch_shapes=[
            pltpu.VMEM((GLA_HEADS, GLA_DV, GLA_DK), F32),
            pltpu.VMEM((c, GLA_KEY_WIDTH), F32),
        ],
        compiler_params=_params(("parallel", "arbitrary")),
        name="gla",
    )(p32, p32, p32, p32, p16, p32_meta, p16_meta, p32_meta, w_gate_pad, b_gate, norm_g)


def _out_kernel(h_ref, a_ref, g_ref, wa_ref, wg_ref, pg_ref, o_ref):
    m = _dot(a_ref[...], wa_ref[...]) + _dot(g_ref[...], wg_ref[...])
    o_ref[...] = h_ref[...] + _rms(m, pg_ref[...])


def _out(hs, o_dsa, o_gla, w_dsa, w_gla, post_g, tm):
    r, d = hs.shape
    return pl.pallas_call(
        _out_kernel,
        out_shape=jax.ShapeDtypeStruct((r, d), F32),
        grid=(r // tm,),
        in_specs=[
            pl.BlockSpec((tm, d), lambda i: (i, 0)),
            pl.BlockSpec((tm, DSA_WIDTH), lambda i: (i, 0)),
            pl.BlockSpec((tm, GLA_WIDTH), lambda i: (i, 0)),
            pl.BlockSpec((DSA_WIDTH, d), lambda i: (0, 0)),
            pl.BlockSpec((GLA_WIDTH, d), lambda i: (0, 0)),
            pl.BlockSpec((1, d), lambda i: (0, 0)),
        ],
        out_specs=pl.BlockSpec((tm, d), lambda i: (i, 0)),
        compiler_params=_params(("parallel",)),
        name="outproj",
    )(hs, o_dsa, o_gla, w_dsa, w_gla, post_g)


def _tile(n, pref):
    t = min(n, pref)
    while n % t:
        t //= 2
    return t


def _proj_weights(w_in):
    d = w_in.shape[0]
    offs = np.cumsum((0,) + IN_SPLITS)
    dq, dk, dv, iq, ik, iw, gq, gk, gv, glow, gout = (w_in[:, offs[i]:offs[i + 1]] for i in range(11))
    z = lambda n: jnp.zeros((d, n), w_in.dtype)
    ik2 = jnp.concatenate([ik, z(IDX_DIM), z(IDX_DIM), ik], axis=1)
    q_scale = DSA_HEAD_DIM ** -0.5 * float(np.log2(np.e))
    w16 = jnp.concatenate([dq * q_scale, dk, dv, iq * (IDX_DIM ** -0.5), gv, ik2], axis=1)
    small = jnp.concatenate([iw * (IDX_HEADS ** -0.5), glow, z(SLAB - IDX_HEADS - GLA_GATE_RANK)], axis=1)
    w32 = jnp.concatenate([gq, gk, gout, small], axis=1)
    return w16.astype(BF16), w32.astype(BF16)


def kernel(x, meta_tokens, ffn1_pre_g, ffn1_w_gate, ffn1_w_up, ffn1_w_down, ffn1_post_g, mix_pre_g, w_in, w_gate_up, b_gate, gla_norm_g, w_out, mix_post_g, ffn2_pre_g, ffn2_w_gate, ffn2_w_up, ffn2_w_down, ffn2_post_g):
    batch, seq, d = x.shape
    depth = w_in.shape[0]
    assert depth == 1, "the meta rows skip the mixer, which is only valid for the last layer"
    assert seq % (2 * IDX_CW) == 0 and meta_tokens.shape[0] == N_META, "score tiles are visited in pairs"
    d_ff = ffn1_w_gate.shape[-1]
    topk = min(TOPK_MAX, seq // 4)
    rows = batch * seq
    tm = _tile(rows, 512)
    tf = _tile(d_ff, 512)
    tq = _tile(seq, 256)
    tk = _tile(seq, 1024)

    hs = x.reshape(rows, d)
    hm = meta_tokens.astype(x.dtype)
    row2 = lambda v: v.reshape(1, -1)
    for l in range(depth):
        ffn1 = (row2(ffn1_pre_g[l]), ffn1_w_gate[l].astype(BF16), ffn1_w_up[l].astype(BF16),
                ffn1_w_down[l], row2(ffn1_post_g[l]))
        hs = _ffn(hs, *ffn1, tm=tm, tf=tf)
        hm = _ffn(hm, *ffn1, tm=N_META, tf=tf)

        w16, w32 = _proj_weights(w_in[l])
        pre = row2(mix_pre_g[l])
        proj16 = functools.partial(_proj, g=pre, w=w16, out_dtype=BF16, slabs_per_step=P16_SLABS_PER_STEP)
        proj32 = functools.partial(_proj, g=pre, w=w32, out_dtype=F32, slabs_per_step=P32_SLABS_PER_STEP)
        tm_proj = _tile(rows, 1024)
        (p16, vt), p32 = proj16(hs, tm=tm_proj, vt_col=4 * SLAB * P16_DV), proj32(hs, tm=tm_proj)
        p16m, p32m = proj16(hm, tm=N_META), proj32(hm, tm=N_META)

        pad_rows = lambda a, n: jnp.pad(a, ((0, 0), (0, n - a.shape[1]), (0, 0)))
        bias = _idx(p16, p32, pad_rows(p16m, IDX_CW)[P16_IK2_SLAB], batch, seq, topk)
        p16m_pad = pad_rows(p16m, LANES)

        def v_t(a):
            vt = jnp.transpose(a[4 * P16_DV:4 * P16_DV + 4], (0, 2, 1)).reshape(DSA_HEADS, DSA_HEAD_DIM, -1)
            ones = jnp.zeros((DSA_HEADS, VT_ROWS - DSA_HEAD_DIM, vt.shape[-1]), vt.dtype).at[:, 0].set(1)
            return jnp.concatenate([vt, ones], axis=1).reshape(DSA_HEADS * VT_ROWS, -1)

        o_dsa = _attn(p16, vt, bias, p16m_pad, v_t(p16m_pad), batch, seq, tq, tk)

        w_gate_pad = jnp.zeros((SLAB, GLA_KEY_WIDTH), F32).at[
            IDX_HEADS:IDX_HEADS + GLA_GATE_RANK].set(w_gate_up[l])
        o_gla = _gla(p32, p16, p32m, p16m, w_gate_pad, row2(b_gate[l]), row2(gla_norm_g[l]), batch, seq)

        wo = w_out[l].astype(BF16)
        hs = _out(hs, o_dsa, o_gla, wo[:DSA_WIDTH], wo[DSA_WIDTH:], row2(mix_post_g[l]), tm)
        ffn2 = (row2(ffn2_pre_g[l]), ffn2_w_gate[l].astype(BF16), ffn2_w_up[l].astype(BF16),
                ffn2_w_down[l], row2(ffn2_post_g[l]))
        hs = _ffn(hs, *ffn2, tm=tm, tf=tf)
    return hs.reshape(batch, seq, d)
```

```python
import functools

import numpy as np
import jax
import jax.numpy as jnp
from jax import lax
from jax.experimental import pallas as pl
from jax.experimental.pallas import tpu as pltpu

F32, BF16, I32 = jnp.float32, jnp.bfloat16, jnp.int32

N_META = 16
FFN_RES = 0.5
EPS = 1e-6
DSA_HEADS = 8
DSA_HEAD_DIM = 128
DSA_WIDTH = DSA_HEADS * DSA_HEAD_DIM
IDX_HEADS = 16
IDX_DIM = 64
TOPK_MAX = 256
GLA_HEADS = 4
GLA_DK = 128
GLA_DV = 256
GLA_KEY_WIDTH = GLA_HEADS * GLA_DK
GLA_WIDTH = GLA_HEADS * GLA_DV
GLA_GATE_RANK = 16
GLA_TAU = 16.0
GLA_CHUNK = 64
GLA_SUB = 16
GLA_ROWS = 512
IN_SPLITS = (DSA_WIDTH, DSA_WIDTH, DSA_WIDTH, IDX_HEADS * IDX_DIM, IDX_DIM, IDX_HEADS,
             GLA_KEY_WIDTH, GLA_KEY_WIDTH, GLA_WIDTH, GLA_GATE_RANK, GLA_WIDTH)

LANES = 128
SLAB = 256
VMEM_LIMIT_BYTES = 56 * 1024 * 1024

P16_DQ, P16_DK, P16_DV, P16_IQ, P16_GV, P16_IK2_SLAB = 0, 1, 2, 3, 4, 20
P32_GQ_SLAB, P32_GK_SLAB, P32_GOUT_SLAB, P32_SMALL_SLAB = 0, 2, 4, 8
P16_SLABS_PER_STEP, P32_SLABS_PER_STEP = 7, 3
VT_ROWS = DSA_HEAD_DIM + 16

NEG_BIAS = -1e30


def _params(sem):
    return pltpu.CompilerParams(dimension_semantics=sem, vmem_limit_bytes=VMEM_LIMIT_BYTES)


def _dot(a, b):
    return jnp.dot(a, b, preferred_element_type=F32)


def _dot_nt(a, b):
    return lax.dot_general(a, b, (((1,), (1,)), ((), ())), preferred_element_type=F32)


def _dot_tn(a, b):
    return lax.dot_general(a, b, (((0,), (0,)), ((), ())), preferred_element_type=F32)


def _split3(x):
    hi = x.astype(BF16)
    r = x - hi.astype(F32)
    mid = r.astype(BF16)
    lo = (r - mid.astype(F32)).astype(BF16)
    return hi, mid, lo


def _dot_exact_lhs01(l01, x):
    hi, mid, lo = _split3(x)
    return _dot(l01, hi) + _dot(l01, mid) + _dot(l01, lo)


def _dot_f32(a, b):
    ah, am, al = _split3(a)
    bh, bm, bl = _split3(b)
    return (_dot(ah, bh) + (_dot(ah, bm) + _dot(am, bh))
            + (_dot(ah, bl) + _dot(am, bm) + _dot(al, bh)))


def _rms(x, g):
    return x * lax.rsqrt(jnp.mean(x * x, axis=-1, keepdims=True) + EPS) * g


def _ffn_kernel(x_ref, preg_ref, wg_ref, wu_ref, wd_ref, postg_ref, o_ref, xn_ref, *, nf):
    f = pl.program_id(1)

    @pl.when(f == 0)
    def _():
        xn_ref[...] = _rms(x_ref[...], preg_ref[...]).astype(BF16)
        o_ref[...] = jnp.zeros_like(o_ref)

    xn = xn_ref[...]
    g = _dot(xn, wg_ref[...])
    u = _dot(xn, wu_ref[...])
    a = (g * jax.nn.sigmoid(g)) * u
    o_ref[...] += _dot(a.astype(BF16), wd_ref[...])

    @pl.when(f == nf - 1)
    def _():
        o_ref[...] = x_ref[...] + FFN_RES * _rms(o_ref[...], postg_ref[...])


def _ffn(x, pre_g, wg, wu, wd, post_g, tm, tf):
    r, d = x.shape
    nf = wg.shape[1] // tf
    return pl.pallas_call(
        functools.partial(_ffn_kernel, nf=nf),
        out_shape=jax.ShapeDtypeStruct((r, d), F32),
        grid=(r // tm, nf),
        in_specs=[
            pl.BlockSpec((tm, d), lambda i, f: (i, 0)),
            pl.BlockSpec((1, d), lambda i, f: (0, 0)),
            pl.BlockSpec((d, tf), lambda i, f: (0, f)),
            pl.BlockSpec((d, tf), lambda i, f: (0, f)),
            pl.BlockSpec((tf, d), lambda i, f: (f, 0)),
            pl.BlockSpec((1, d), lambda i, f: (0, 0)),
        ],
        out_specs=pl.BlockSpec((tm, d), lambda i, f: (i, 0)),
        scratch_shapes=[pltpu.VMEM((tm, d), BF16)],
        compiler_params=_params(("parallel", "arbitrary")),
        name="ffn",
    )(x, pre_g, wg, wu, wd, post_g)


def _proj_kernel(x_ref, g_ref, w_ref, o_ref, *rest, vt_step, vt_col):
    vt_ref, xn_ref = rest if vt_step is not None else (None, rest[0])

    @pl.when(pl.program_id(1) == 0)
    def _():
        xn_ref[...] = _rms(x_ref[...], g_ref[...]).astype(BF16)

    res = _dot(xn_ref[...], w_ref[...])
    for s in range(o_ref.shape[0]):
        o_ref[s] = res[:, s * SLAB:(s + 1) * SLAB].astype(o_ref.dtype)

    if vt_ref is not None:
        @pl.when(pl.program_id(1) == vt_step)
        def _():
            tm = res.shape[0]
            vt = res[:, vt_col:vt_col + DSA_WIDTH].T.astype(BF16)
            pad = VT_ROWS - DSA_HEAD_DIM
            ones = jnp.where(lax.broadcasted_iota(I32, (pad, tm), 0) == 0, 1.0, 0.0).astype(BF16)
            for h in range(DSA_HEADS):
                vt_ref[h * VT_ROWS:h * VT_ROWS + DSA_HEAD_DIM, :] = vt[h * DSA_HEAD_DIM:(h + 1) * DSA_HEAD_DIM]
                vt_ref[h * VT_ROWS + DSA_HEAD_DIM:(h + 1) * VT_ROWS, :] = ones


def _proj(x, g, w, out_dtype, tm, slabs_per_step, vt_col=None):
    r, d = x.shape
    n_slabs = w.shape[1] // SLAB
    tn = slabs_per_step * SLAB
    out_shape = [jax.ShapeDtypeStruct((n_slabs, r, SLAB), out_dtype)]
    out_specs = [pl.BlockSpec((slabs_per_step, tm, SLAB), lambda i, j: (j, i, 0))]
    vt_step = None
    if vt_col is not None:
        vt_step = vt_col // tn
        assert (vt_col + DSA_WIDTH - 1) // tn == vt_step, "the transposed columns must fall in one grid step"
        out_shape.append(jax.ShapeDtypeStruct((DSA_HEADS * VT_ROWS, r), BF16))
        out_specs.append(pl.BlockSpec((DSA_HEADS * VT_ROWS, tm), lambda i, j: (0, i)))
    out = pl.pallas_call(
        functools.partial(_proj_kernel, vt_step=vt_step, vt_col=None if vt_col is None else vt_col % tn),
        out_shape=out_shape,
        grid=(r // tm, n_slabs // slabs_per_step),
        in_specs=[
            pl.BlockSpec((tm, d), lambda i, j: (i, 0)),
            pl.BlockSpec((1, d), lambda i, j: (0, 0)),
            pl.BlockSpec((d, tn), lambda i, j: (0, j)),
        ],
        out_specs=out_specs,
        scratch_shapes=[pltpu.VMEM((tm, d), BF16)],
        compiler_params=_params(("arbitrary", "arbitrary")),
        name="proj",
    )(x, g, w)
    return out if vt_col is not None else out[0]


IDX_TQ = 256
IDX_CW = 256
BF16_KEY_MIN, BF16_KEY_MAX = -(2 ** 15), 2 ** 15 - 1
STAGE2_BITS = 18


def _key32(pattern):
    return jnp.where(pattern < 0, pattern ^ 0x7FFFFFFF, pattern)


def _pattern_of_key16(k16):
    return lax.shift_left(jnp.where(k16 < 0, k16 ^ 0x7FFF, k16), 16)


def _idx_kernel(iq_ref, sm_ref, ik_ref, ikm_ref, bias_ref, sc_ref, sc16_ref, w_ref, *, seq, topk):
    qi = pl.program_id(1)
    tq, cw = IDX_TQ, IDX_CW
    nt = seq // cw
    n_x = qi + 1
    kf = float(topk)

    w_ref[...] = sm_ref[0][:, 0:LANES].T
    key_pos = lax.broadcasted_iota(I32, (cw, tq), 0)
    q_pos = qi * tq + lax.broadcasted_iota(I32, (cw, tq), 1)

    def score_tile(k2):
        lhs = jnp.concatenate([k2[:, 0:LANES], k2[:, LANES:2 * LANES]], axis=0)
        acc = jnp.zeros((cw, tq), F32)
        for p in range(IDX_HEADS // 2):
            rhs = iq_ref[p // 2][:, (p % 2) * LANES:(p % 2 + 1) * LANES]
            l2 = _dot_nt(lhs, rhs)
            acc = acc + jnp.maximum(l2[0:cw], 0.0) * w_ref[2 * p:2 * p + 1, :]
            acc = acc + jnp.maximum(l2[cw:2 * cw], 0.0) * w_ref[2 * p + 1:2 * p + 2, :]
        return acc

    def x_tile(c):
        k2 = ik_ref[0, pl.ds(pl.multiple_of(c * cw, cw), cw), :]
        acc = jnp.where(c * cw + key_pos <= q_pos, score_tile(k2), -jnp.inf)
        sc_ref[c + 1] = acc
        sc16_ref[c + 1] = acc.astype(BF16)

    def x_tile_pair(p, carry):
        x_tile(2 * p)
        x_tile(2 * p + 1)
        return carry

    lax.fori_loop(0, n_x // 2, x_tile_pair, 0)

    @pl.when((n_x & 1) == 1)
    def _():
        x_tile(n_x - 1)
    acc_m = jnp.where(key_pos < N_META, score_tile(ikm_ref[...]), -jnp.inf)
    sc_ref[0] = acc_m
    sc16_ref[0] = acc_m.astype(BF16)
    sc_ref[n_x + 1] = jnp.full((cw, tq), -jnp.inf, F32)
    sc16_ref[n_x + 1] = jnp.full((cw, tq), -jnp.inf, BF16)
    n_pairs = (n_x + 2) // 2

    one16, zero16 = jnp.ones((cw, tq), BF16), jnp.zeros((cw, tq), BF16)

    def count16(c16):
        def tile_count(c):
            hit = jnp.where(sc16_ref[c] >= c16, one16, zero16).reshape(cw // 16, 16, tq)
            parts = [hit[i] for i in range(cw // 16)]
            while len(parts) > 1:
                parts = [parts[i] + parts[i + 1] for i in range(0, len(parts), 2)]
            return parts[0].astype(F32)

        cnt = lax.fori_loop(0, n_pairs, lambda p, a: a + (tile_count(2 * p) + tile_count(2 * p + 1)),
                            jnp.zeros((16, tq), F32))
        return jnp.sum(cnt, axis=0, keepdims=True)

    def stage1(it, k16):
        cand = k16 + lax.shift_left(jnp.int32(1), 15 - it)
        c32 = lax.bitcast_convert_type(_pattern_of_key16(cand), F32)
        c16 = jnp.broadcast_to(c32, (cw, tq)).astype(BF16)
        return jnp.where(count16(c16) >= kf, cand, k16)

    k16 = lax.fori_loop(0, 16, stage1, jnp.full((1, tq), BF16_KEY_MIN, I32))

    near_zero = jnp.logical_and(k16 >= -2, k16 <= 1)
    reach = jnp.where(near_zero, 2, 1)
    lo = _key32(_pattern_of_key16(jnp.maximum(k16 - reach, BF16_KEY_MIN)))
    hi = _key32(_pattern_of_key16(jnp.minimum(k16 + reach, BF16_KEY_MAX)))
    first_bit = jnp.where(jnp.max(jnp.where(near_zero, 1.0, 0.0)) > 0.5, 0, 1)

    def count32(cf):
        def tile_count(c):
            hit = jnp.where(sc_ref[c] >= cf, 1.0, 0.0)
            return jnp.sum(hit.reshape(cw // 8, 8, tq), axis=0)

        cnt = lax.fori_loop(0, n_pairs, lambda p, a: a + (tile_count(2 * p) + tile_count(2 * p + 1)),
                            jnp.zeros((8, tq), F32))
        return jnp.sum(cnt, axis=0, keepdims=True)

    short = (qi * tq + lax.broadcasted_iota(I32, (1, tq), 1) + 1 + N_META) <= topk

    def unsettled(cnt_t):
        return (jnp.max(jnp.where(short | (cnt_t == kf), 0.0, 1.0)) > 0.5).astype(I32)

    def stage2_cond(st):
        it, _, _, go = st
        return jnp.logical_and(it < STAGE2_BITS, go > 0)

    def stage2(st):
        it, t, cnt_t, _ = st
        cand = t + lax.shift_left(jnp.int32(1), STAGE2_BITS - 1 - it)
        cnt = count32(lax.bitcast_convert_type(jnp.where(cand < 0, cand ^ 0x7FFFFFFF, cand), F32))
        ok = jnp.logical_and(cand < hi, cnt >= kf)
        cnt_t = jnp.where(ok, cnt, cnt_t)
        return it + 1, jnp.where(ok, cand, t), cnt_t, unsettled(cnt_t)

    _, t, cnt_t, tied_any = lax.while_loop(stage2_cond, stage2,
                                           (first_bit.astype(I32), lo, jnp.full((1, tq), -1.0, F32), jnp.int32(1)))
    thr = lax.bitcast_convert_type(jnp.where(t < 0, t ^ 0x7FFFFFFF, t), F32)
    thr = jnp.where(short, jnp.finfo(F32).min, thr)

    def write_bias(bias_of):
        for c in range(nt):
            @pl.when(c < n_x)
            def _(c=c):
                bias_ref[0, c * cw:(c + 1) * cw, :] = bias_of(
                    sc_ref[c + 1], N_META + c * cw + key_pos).astype(BF16)

            @pl.when(c >= n_x)
            def _(c=c):
                bias_ref[0, c * cw:(c + 1) * cw, :] = jnp.full((cw, tq), NEG_BIAS, BF16)

        meta_pos = lax.broadcasted_iota(I32, (LANES, tq), 0)
        bias_ref[0, seq:seq + LANES, :] = bias_of(sc_ref[0, 0:LANES, :], meta_pos).astype(BF16)

    @pl.when(tied_any == 0)
    def _():
        write_bias(lambda s, pos: jnp.where(s >= thr, 0.0, NEG_BIAS))

    @pl.when(tied_any > 0)
    def _():
        def count_where(hit_of):
            def tile_count(slot):
                pos = key_pos + jnp.where(slot == 0, 0, N_META + (slot - 1) * cw)
                return jnp.sum(hit_of(sc_ref[slot], pos).reshape(cw // 8, 8, tq), axis=0)

            cnt = lax.fori_loop(0, 2 * n_pairs, lambda s_, a: a + tile_count(s_), jnp.zeros((8, tq), F32))
            return jnp.sum(cnt, axis=0, keepdims=True)

        need = kf - count_where(lambda s, pos: jnp.where(s > thr, 1.0, 0.0))
        pos_bits = int(seq + N_META).bit_length()

        def pos_bit(i, cut):
            cand = cut | lax.shift_left(jnp.int32(1), pos_bits - 1 - i)
            below = count_where(lambda s, pos: jnp.where(s == thr, jnp.where(pos < cand, 1.0, 0.0), 0.0))
            return jnp.where(below < need, cand, cut)

        cut = lax.fori_loop(0, pos_bits, pos_bit, jnp.zeros((1, tq), I32))
        tied = jnp.logical_and(jnp.logical_not(short), cnt_t != kf)
        cut = jnp.where(tied, cut, jnp.iinfo(I32).max)
        write_bias(lambda s, pos: jnp.where(
            s > thr, 0.0, jnp.where(s == thr, jnp.where(pos <= cut, 0.0, NEG_BIAS), NEG_BIAS)))


def _idx(p16, p32, ik2_meta, batch, seq, topk):
    nq = seq // IDX_TQ
    nt = seq // IDX_CW
    return pl.pallas_call(
        functools.partial(_idx_kernel, seq=seq, topk=topk),
        out_shape=jax.ShapeDtypeStruct((batch, seq + LANES, seq), BF16),
        grid=(batch, nq),
        in_specs=[
            pl.BlockSpec((4, IDX_TQ, SLAB), lambda b, q: (P16_IQ, b * nq + q, 0)),
            pl.BlockSpec((1, IDX_TQ, SLAB), lambda b, q: (P32_SMALL_SLAB, b * nq + q, 0)),
            pl.BlockSpec((1, seq, SLAB), lambda b, q: (P16_IK2_SLAB, b, 0)),
            pl.BlockSpec((IDX_CW, SLAB), lambda b, q: (0, 0)),
        ],
        out_specs=pl.BlockSpec((1, seq + LANES, IDX_TQ), lambda b, q: (b, 0, q)),
        scratch_shapes=[
            pltpu.VMEM((nt + 2, IDX_CW, IDX_TQ), F32),
            pltpu.VMEM((nt + 2, IDX_CW, IDX_TQ), BF16),
            pltpu.VMEM((LANES, IDX_TQ), F32),
        ],
        compiler_params=_params(("parallel", "arbitrary")),
        name="idx",
    )(p16, p32, p16, ik2_meta)


ATTN_AHEAD = 2


def _attn_kernel(qt_ref, kt_ref, q_ref, k_ref, vt_ref, b_ref, bm_ref, km_ref, vtm_ref, o_ref,
                 m_ref, acc_ref, *, tq, tk):
    p = pl.program_id(1)
    qi, ki = qt_ref[p], kt_ref[p]

    def head(ref, h):
        return ref[h // 2][:, (h % 2) * LANES:(h % 2 + 1) * LANES]

    def accumulate(kref, vtref, bias):
        def scores(h):
            return _dot_nt(head(kref, h), head(q_ref, h)) + bias

        pending = [scores(h) for h in range(ATTN_AHEAD)]
        for h in range(DSA_HEADS):
            s = pending.pop(0)
            if h + ATTN_AHEAD < DSA_HEADS:
                pending.append(scores(h + ATTN_AHEAD))
            m_prev = m_ref[h]
            m_new = jnp.maximum(m_prev, jnp.max(s, axis=0, keepdims=True))
            alpha = jnp.exp2(m_prev - m_new)
            pr = jnp.exp2(s - m_new).astype(BF16)
            vt = vtref[h * VT_ROWS:(h + 1) * VT_ROWS, :]
            acc_ref[h] = alpha * acc_ref[h] + _dot(vt, pr)
            m_ref[h] = m_new

    @pl.when(ki == 0)
    def _():
        m_ref[...] = jnp.full(m_ref.shape, -jnp.inf, F32)
        acc_ref[...] = jnp.zeros_like(acc_ref)
        accumulate(km_ref, vtm_ref, bm_ref[0].astype(F32))

    accumulate(k_ref, vt_ref, b_ref[0].astype(F32))

    @pl.when(ki == (qi * tq + tq - 1) // tk)
    def _():
        for h in range(DSA_HEADS):
            hs = slice(h * DSA_HEAD_DIM, (h + 1) * DSA_HEAD_DIM)
            acc = acc_ref[h]
            o_ref[:, hs] = (acc[0:DSA_HEAD_DIM] / acc[DSA_HEAD_DIM:DSA_HEAD_DIM + 1]).T.astype(o_ref.dtype)


def _attn(p16, vt, bias, p16_meta, vt_meta, batch, seq, tq, tk):
    nq, nk = seq // tq, seq // tk
    pairs = [(q, k) for q in range(nq) for k in range((q * tq + tq - 1) // tk + 1)]
    q_tab = jnp.asarray([q for q, _ in pairs], I32)
    k_tab = jnp.asarray([k for _, k in pairs], I32)
    grid_spec = pltpu.PrefetchScalarGridSpec(
        num_scalar_prefetch=2,
        grid=(batch, len(pairs)),
        in_specs=[
            pl.BlockSpec((4, tq, SLAB), lambda b, p, qt, kt: (P16_DQ, b * nq + qt[p], 0)),
            pl.BlockSpec((4, tk, SLAB), lambda b, p, qt, kt: (P16_DK, b * nk + kt[p], 0)),
            pl.BlockSpec((DSA_HEADS * VT_ROWS, tk), lambda b, p, qt, kt: (0, b * nk + kt[p])),
            pl.BlockSpec((1, tk, tq), lambda b, p, qt, kt: (b, kt[p], qt[p])),
            pl.BlockSpec((1, LANES, tq), lambda b, p, qt, kt: (b, seq // LANES, qt[p])),
            pl.BlockSpec((4, LANES, SLAB), lambda b, p, qt, kt: (P16_DK, 0, 0)),
            pl.BlockSpec((DSA_HEADS * VT_ROWS, LANES), lambda b, p, qt, kt: (0, 0)),
        ],
        out_specs=pl.BlockSpec((tq, DSA_WIDTH), lambda b, p, qt, kt: (b * nq + qt[p], 0)),
        scratch_shapes=[
            pltpu.VMEM((DSA_HEADS, 1, tq), F32),
            pltpu.VMEM((DSA_HEADS, VT_ROWS, tq), F32),
        ],
    )
    return pl.pallas_call(
        functools.partial(_attn_kernel, tq=tq, tk=tk),
        out_shape=jax.ShapeDtypeStruct((batch * seq, DSA_WIDTH), BF16),
        grid_spec=grid_spec,
        compiler_params=_params(("parallel", "arbitrary")),
        name="attn",
    )(q_tab, k_tab, p16, p16, vt, bias, bias, p16_meta, vt_meta)


def _log_sigmoid(x):
    return jnp.minimum(x, 0.0) - jnp.log1p(jnp.exp(-jnp.abs(x)))


def _gla_kernel(gq_ref, gk_ref, go_ref, sm_ref, gv_ref, mk_ref, mv_ref, msm_ref, wgate_ref, bgate_ref,
                ng_ref, o_ref, st_ref, bc_ref):
    c = GLA_CHUNK

    def log_decay(small):
        return _log_sigmoid(_dot_f32(small, wgate_ref[...]) + bgate_ref[...]) / GLA_TAU

    def lower_ones(n):
        r = lax.broadcasted_iota(I32, (n, n), 0)
        cc = lax.broadcasted_iota(I32, (n, n), 1)
        return r >= cc

    @pl.when(pl.program_id(1) == 0)
    def _():
        bcm = _dot_exact_lhs01(jnp.where(lower_ones(N_META), 1.0, 0.0).astype(BF16),
                               log_decay(msm_ref[0]))
        for h in range(GLA_HEADS):
            ks = slice(h * GLA_DK, (h + 1) * GLA_DK)
            kd = mk_ref[h // 2][:, (h % 2) * GLA_DK:(h % 2 + 1) * GLA_DK] * jnp.exp(
                bcm[N_META - 1:N_META, ks] - bcm[:, ks])
            st_ref[h] = _dot_tn(mv_ref[h], kd.astype(BF16))

    rows = bc_ref.shape[0]
    r_i = lax.broadcasted_iota(I32, (rows, rows), 0)
    c_i = lax.broadcasted_iota(I32, (rows, rows), 1)
    chunk_tri = jnp.logical_and(r_i >= c_i, r_i // c == c_i // c)
    bc_ref[...] = _dot_exact_lhs01(jnp.where(chunk_tri, 1.0, 0.0).astype(BF16), log_decay(sm_ref[0]))
    causal = lower_ones(c)
    sub = GLA_SUB
    lane = lax.broadcasted_iota(I32, (sub, c), 1)

    for ch in range(rows // c):
        base = ch * c
        for h in range(GLA_HEADS):
            ks = slice(h * GLA_DK, (h + 1) * GLA_DK)
            hsl = slice((h % 2) * GLA_DK, (h % 2 + 1) * GLA_DK)
            q = gq_ref[h // 2][base:base + c, hsl] * (GLA_DK ** -0.5)
            k = gk_ref[h // 2][base:base + c, hsl]
            v = gv_ref[h][base:base + c]
            bc = bc_ref[base:base + c, ks]

            blocks = []
            for s0 in range(0, c, sub):
                q_s, bc_s = q[s0:s0 + sub], bc[s0:s0 + sub]
                diag = jnp.zeros((sub, c), F32)
                for j in range(s0, s0 + sub):
                    e = jnp.exp(jnp.minimum(bc_s - bc_ref[base + j:base + j + 1, ks], 0.0))
                    k_j = gk_ref[h // 2][base + j:base + j + 1, hsl]
                    diag = jnp.where(lane == j, jnp.sum(q_s * k_j * e, axis=1, keepdims=True), diag)
                if s0:
                    b_edge = bc_ref[base + s0 - 1:base + s0, ks]
                    q_t = q_s * jnp.exp(bc_s - b_edge)
                    k_t = k * jnp.exp(jnp.minimum(b_edge - bc, 0.0))
                    diag = jnp.where(lane < s0, _dot_nt(q_t.astype(BF16), k_t.astype(BF16)), diag)
                blocks.append(diag)
            attn = jnp.where(causal, jnp.concatenate(blocks, axis=0), 0.0)

            state = st_ref[h]
            b_last = bc[c - 1:c, :]
            o = _dot_nt((q * jnp.exp(bc)).astype(BF16), state.astype(BF16))
            o = o + _dot(attn.astype(BF16), v)
            kd = k * jnp.exp(b_last - bc)
            st_ref[h] = state * jnp.exp(b_last) + _dot_tn(v, kd.astype(BF16))

            g = go_ref[h][base:base + c]
            y = _rms(o, ng_ref[...]) * (g * jax.nn.sigmoid(g))
            o_ref[base:base + c, h * GLA_DV:(h + 1) * GLA_DV] = y.astype(o_ref.dtype)


def _gla(p32, p16, p32_meta, p16_meta, w_gate_pad, b_gate, norm_g, batch, seq):
    c = _tile(seq, GLA_ROWS)
    ns = seq // c
    row = lambda b, i: b * ns + i
    return pl.pallas_call(
        _gla_kernel,
        out_shape=jax.ShapeDtypeStruct((batch * seq, GLA_WIDTH), BF16),
        grid=(batch, ns),
        in_specs=[
            pl.BlockSpec((2, c, SLAB), lambda b, i: (P32_GQ_SLAB // 2, row(b, i), 0)),
            pl.BlockSpec((2, c, SLAB), lambda b, i: (P32_GK_SLAB // 2, row(b, i), 0)),
            pl.BlockSpec((4, c, SLAB), lambda b, i: (P32_GOUT_SLAB // 4, row(b, i), 0)),
            pl.BlockSpec((1, c, SLAB), lambda b, i: (P32_SMALL_SLAB, row(b, i), 0)),
            pl.BlockSpec((4, c, SLAB), lambda b, i: (P16_GV, row(b, i), 0)),
            pl.BlockSpec((2, N_META, SLAB), lambda b, i: (P32_GK_SLAB // 2, 0, 0)),
            pl.BlockSpec((4, N_META, SLAB), lambda b, i: (P16_GV, 0, 0)),
            pl.BlockSpec((1, N_META, SLAB), lambda b, i: (P32_SMALL_SLAB, 0, 0)),
            pl.BlockSpec((SLAB, GLA_KEY_WIDTH), lambda b, i: (0, 0)),
            pl.BlockSpec((1, GLA_KEY_WIDTH), lambda b, i: (0, 0)),
            pl.BlockSpec((1, GLA_DV), lambda b, i: (0, 0)),
        ],
        out_specs=pl.BlockSpec((c, GLA_WIDTH), lambda b, i: (row(b, i), 0)),
        scratch_shapes=[
            pltpu.VMEM((GLA_HEADS, GLA_DV, GLA_DK), F32),
            pltpu.VMEM((c, GLA_KEY_WIDTH), F32),
        ],
        compiler_params=_params(("parallel", "arbitrary")),
        name="gla",
    )(p32, p32, p32, p32, p16, p32_meta, p16_meta, p32_meta, w_gate_pad, b_gate, norm_g)


def _out_kernel(h_ref, a_ref, g_ref, wa_ref, wg_ref, pg_ref, o_ref):
    m = _dot(a_ref[...], wa_ref[...]) + _dot(g_ref[...], wg_ref[...])
    o_ref[...] = h_ref[...] + _rms(m, pg_ref[...])


def _out(hs, o_dsa, o_gla, w_dsa, w_gla, post_g, tm):
    r, d = hs.shape
    return pl.pallas_call(
        _out_kernel,
        out_shape=jax.ShapeDtypeStruct((r, d), F32),
        grid=(r // tm,),
        in_specs=[
            pl.BlockSpec((tm, d), lambda i: (i, 0)),
            pl.BlockSpec((tm, DSA_WIDTH), lambda i: (i, 0)),
            pl.BlockSpec((tm, GLA_WIDTH), lambda i: (i, 0)),
            pl.BlockSpec((DSA_WIDTH, d), lambda i: (0, 0)),
            pl.BlockSpec((GLA_WIDTH, d), lambda i: (0, 0)),
            pl.BlockSpec((1, d), lambda i: (0, 0)),
        ],
        out_specs=pl.BlockSpec((tm, d), lambda i: (i, 0)),
        compiler_params=_params(("parallel",)),
        name="outproj",
    )(hs, o_dsa, o_gla, w_dsa, w_gla, post_g)


def _tile(n, pref):
    t = min(n, pref)
    while n % t:
        t //= 2
    return t


def _proj_weights(w_in):
    d = w_in.shape[0]
    offs = np.cumsum((0,) + IN_SPLITS)
    dq, dk, dv, iq, ik, iw, gq, gk, gv, glow, gout = (w_in[:, offs[i]:offs[i + 1]] for i in range(11))
    z = lambda n: jnp.zeros((d, n), w_in.dtype)
    ik2 = jnp.concatenate([ik, z(IDX_DIM), z(IDX_DIM), ik], axis=1)
    q_scale = DSA_HEAD_DIM ** -0.5 * float(np.log2(np.e))
    w16 = jnp.concatenate([dq * q_scale, dk, dv, iq * (IDX_DIM ** -0.5), gv, ik2], axis=1)
    small = jnp.concatenate([iw * (IDX_HEADS ** -0.5), glow, z(SLAB - IDX_HEADS - GLA_GATE_RANK)], axis=1)
    w32 = jnp.concatenate([gq, gk, gout, small], axis=1)
    return w16.astype(BF16), w32.astype(BF16)


def kernel(x, meta_tokens, ffn1_pre_g, ffn1_w_gate, ffn1_w_up, ffn1_w_down, ffn1_post_g, mix_pre_g, w_in, w_gate_up, b_gate, gla_norm_g, w_out, mix_post_g, ffn2_pre_g, ffn2_w_gate, ffn2_w_up, ffn2_w_down, ffn2_post_g):
    batch, seq, d = x.shape
    depth = w_in.shape[0]
    assert depth == 1, "the meta rows skip the mixer, which is only valid for the last layer"
    assert seq % (2 * IDX_CW) == 0 and meta_tokens.shape[0] == N_META, "score tiles are visited in pairs"
    d_ff = ffn1_w_gate.shape[-1]
    topk = min(TOPK_MAX, seq // 4)
    rows = batch * seq
    tm = _tile(rows, 512)
    tf = _tile(d_ff, 512)
    tq = _tile(seq, 256)
    tk = _tile(seq, 1024)

    hs = x.reshape(rows, d)
    hm = meta_tokens.astype(x.dtype)
    row2 = lambda v: v.reshape(1, -1)
    for l in range(depth):
        ffn1 = (row2(ffn1_pre_g[l]), ffn1_w_gate[l].astype(BF16), ffn1_w_up[l].astype(BF16),
                ffn1_w_down[l].astype(BF16), row2(ffn1_post_g[l]))
        hs = _ffn(hs, *ffn1, tm=tm, tf=tf)
        hm = _ffn(hm, *ffn1, tm=N_META, tf=tf)

        w16, w32 = _proj_weights(w_in[l])
        pre = row2(mix_pre_g[l])
        proj16 = functools.partial(_proj, g=pre, w=w16, out_dtype=BF16, slabs_per_step=P16_SLABS_PER_STEP)
        proj32 = functools.partial(_proj, g=pre, w=w32, out_dtype=F32, slabs_per_step=P32_SLABS_PER_STEP)
        tm_proj = _tile(rows, 1024)
        (p16, vt), p32 = proj16(hs, tm=tm_proj, vt_col=4 * SLAB * P16_DV), proj32(hs, tm=tm_proj)
        p16m, p32m = proj16(hm, tm=N_META), proj32(hm, tm=N_META)

        pad_rows = lambda a, n: jnp.pad(a, ((0, 0), (0, n - a.shape[1]), (0, 0)))
        bias = _idx(p16, p32, pad_rows(p16m, IDX_CW)[P16_IK2_SLAB], batch, seq, topk)
        p16m_pad = pad_rows(p16m, LANES)

        def v_t(a):
            vt = jnp.transpose(a[4 * P16_DV:4 * P16_DV + 4], (0, 2, 1)).reshape(DSA_HEADS, DSA_HEAD_DIM, -1)
            ones = jnp.zeros((DSA_HEADS, VT_ROWS - DSA_HEAD_DIM, vt.shape[-1]), vt.dtype).at[:, 0].set(1)
            return jnp.concatenate([vt, ones], axis=1).reshape(DSA_HEADS * VT_ROWS, -1)

        o_dsa = _attn(p16, vt, bias, p16m_pad, v_t(p16m_pad), batch, seq, tq, tk)

        w_gate_pad = jnp.zeros((SLAB, GLA_KEY_WIDTH), F32).at[
            IDX_HEADS:IDX_HEADS + GLA_GATE_RANK].set(w_gate_up[l])
        o_gla = _gla(p32, p16, p32m, p16m, w_gate_pad, row2(b_gate[l]), row2(gla_norm_g[l]), batch, seq)

        wo = w_out[l].astype(BF16)
        hs = _out(hs, o_dsa, o_gla, wo[:DSA_WIDTH], wo[DSA_WIDTH:], row2(mix_post_g[l]), tm)
        ffn2 = (row2(ffn2_pre_g[l]), ffn2_w_gate[l].astype(BF16), ffn2_w_up[l].astype(BF16),
                ffn2_w_down[l].astype(BF16), row2(ffn2_post_g[l]))
        hs = _ffn(hs, *ffn2, tm=tm, tf=tf)
    return hs.reshape(batch, seq, d)
```

```python
import functools

import numpy as np
import jax
import jax.numpy as jnp
from jax import lax
from jax.experimental import pallas as pl
from jax.experimental.pallas import tpu as pltpu

F32, BF16, I32 = jnp.float32, jnp.bfloat16, jnp.int32

N_META = 16
FFN_RES = 0.5
EPS = 1e-6
DSA_HEADS = 8
DSA_HEAD_DIM = 128
DSA_WIDTH = DSA_HEADS * DSA_HEAD_DIM
IDX_HEADS = 16
IDX_DIM = 64
TOPK_MAX = 256
GLA_HEADS = 4
GLA_DK = 128
GLA_DV = 256
GLA_KEY_WIDTH = GLA_HEADS * GLA_DK
GLA_WIDTH = GLA_HEADS * GLA_DV
GLA_GATE_RANK = 16
GLA_TAU = 16.0
GLA_CHUNK = 64
GLA_SUB = 16
GLA_ROWS = 512
IN_SPLITS = (DSA_WIDTH, DSA_WIDTH, DSA_WIDTH, IDX_HEADS * IDX_DIM, IDX_DIM, IDX_HEADS,
             GLA_KEY_WIDTH, GLA_KEY_WIDTH, GLA_WIDTH, GLA_GATE_RANK, GLA_WIDTH)

LANES = 128
SLAB = 256
VMEM_LIMIT_BYTES = 56 * 1024 * 1024

P16_DQ, P16_DK, P16_DV, P16_IQ, P16_GV, P16_IK2_SLAB = 0, 1, 2, 3, 4, 20
P32_GQ_SLAB, P32_GK_SLAB, P32_GOUT_SLAB, P32_SMALL_SLAB = 0, 2, 4, 8
P16_SLABS_PER_STEP, P32_SLABS_PER_STEP = 7, 3
VT_ROWS = DSA_HEAD_DIM + 16

NEG_BIAS = -1e30


def _params(sem):
    return pltpu.CompilerParams(dimension_semantics=sem, vmem_limit_bytes=VMEM_LIMIT_BYTES)


def _dot(a, b):
    return jnp.dot(a, b, preferred_element_type=F32)


def _dot_nt(a, b):
    return lax.dot_general(a, b, (((1,), (1,)), ((), ())), preferred_element_type=F32)


def _dot_tn(a, b):
    return lax.dot_general(a, b, (((0,), (0,)), ((), ())), preferred_element_type=F32)


def _split3(x):
    hi = x.astype(BF16)
    r = x - hi.astype(F32)
    mid = r.astype(BF16)
    lo = (r - mid.astype(F32)).astype(BF16)
    return hi, mid, lo


def _dot_exact_lhs01(l01, x):
    hi, mid, lo = _split3(x)
    return _dot(l01, hi) + _dot(l01, mid) + _dot(l01, lo)


def _dot_f32(a, b):
    ah, am, al = _split3(a)
    bh, bm, bl = _split3(b)
    return (_dot(ah, bh) + (_dot(ah, bm) + _dot(am, bh))
            + (_dot(ah, bl) + _dot(am, bm) + _dot(al, bh)))


def _rms(x, g):
    return x * lax.rsqrt(jnp.mean(x * x, axis=-1, keepdims=True) + EPS) * g


def _ffn_kernel(x_ref, preg_ref, wg_ref, wu_ref, wd_ref, postg_ref, o_ref, xn_ref, *, nf):
    f = pl.program_id(1)

    @pl.when(f == 0)
    def _():
        xn_ref[...] = _rms(x_ref[...], preg_ref[...]).astype(BF16)
        o_ref[...] = jnp.zeros_like(o_ref)

    xn = xn_ref[...]
    g = _dot(xn, wg_ref[...])
    u = _dot(xn, wu_ref[...])
    a = (g * jax.nn.sigmoid(g)) * u
    o_ref[...] += _dot(a.astype(BF16), wd_ref[...])

    @pl.when(f == nf - 1)
    def _():
        o_ref[...] = x_ref[...] + FFN_RES * _rms(o_ref[...], postg_ref[...])


def _ffn(x, pre_g, wg, wu, wd, post_g, tm, tf):
    r, d = x.shape
    nf = wg.shape[1] // tf
    return pl.pallas_call(
        functools.partial(_ffn_kernel, nf=nf),
        out_shape=jax.ShapeDtypeStruct((r, d), F32),
        grid=(r // tm, nf),
        in_specs=[
            pl.BlockSpec((tm, d), lambda i, f: (i, 0)),
            pl.BlockSpec((1, d), lambda i, f: (0, 0)),
            pl.BlockSpec((d, tf), lambda i, f: (0, f)),
            pl.BlockSpec((d, tf), lambda i, f: (0, f)),
            pl.BlockSpec((tf, d), lambda i, f: (f, 0)),
            pl.BlockSpec((1, d), lambda i, f: (0, 0)),
        ],
        out_specs=pl.BlockSpec((tm, d), lambda i, f: (i, 0)),
        scratch_shapes=[pltpu.VMEM((tm, d), BF16)],
        compiler_params=_params(("parallel", "arbitrary")),
        name="ffn",
    )(x, pre_g, wg, wu, wd, post_g)


def _proj_kernel(x_ref, g_ref, w_ref, o_ref, *rest, vt_step, vt_col):
    vt_ref, xn_ref = rest if vt_step is not None else (None, rest[0])

    @pl.when(pl.program_id(1) == 0)
    def _():
        xn_ref[...] = _rms(x_ref[...], g_ref[...]).astype(BF16)

    res = _dot(xn_ref[...], w_ref[...])
    for s in range(o_ref.shape[0]):
        o_ref[s] = res[:, s * SLAB:(s + 1) * SLAB].astype(o_ref.dtype)

    if vt_ref is not None:
        @pl.when(pl.program_id(1) == vt_step)
        def _():
            tm = res.shape[0]
            vt = res[:, vt_col:vt_col + DSA_WIDTH].T.astype(BF16)
            pad = VT_ROWS - DSA_HEAD_DIM
            ones = jnp.where(lax.broadcasted_iota(I32, (pad, tm), 0) == 0, 1.0, 0.0).astype(BF16)
            for h in range(DSA_HEADS):
                vt_ref[h * VT_ROWS:h * VT_ROWS + DSA_HEAD_DIM, :] = vt[h * DSA_HEAD_DIM:(h + 1) * DSA_HEAD_DIM]
                vt_ref[h * VT_ROWS + DSA_HEAD_DIM:(h + 1) * VT_ROWS, :] = ones


def _proj(x, g, w, out_dtype, tm, slabs_per_step, vt_col=None):
    r, d = x.shape
    n_slabs = w.shape[1] // SLAB
    tn = slabs_per_step * SLAB
    out_shape = [jax.ShapeDtypeStruct((n_slabs, r, SLAB), out_dtype)]
    out_specs = [pl.BlockSpec((slabs_per_step, tm, SLAB), lambda i, j: (j, i, 0))]
    vt_step = None
    if vt_col is not None:
        vt_step = vt_col // tn
        assert (vt_col + DSA_WIDTH - 1) // tn == vt_step, "the transposed columns must fall in one grid step"
        out_shape.append(jax.ShapeDtypeStruct((DSA_HEADS * VT_ROWS, r), BF16))
        out_specs.append(pl.BlockSpec((DSA_HEADS * VT_ROWS, tm), lambda i, j: (0, i)))
    out = pl.pallas_call(
        functools.partial(_proj_kernel, vt_step=vt_step, vt_col=None if vt_col is None else vt_col % tn),
        out_shape=out_shape,
        grid=(r // tm, n_slabs // slabs_per_step),
        in_specs=[
            pl.BlockSpec((tm, d), lambda i, j: (i, 0)),
            pl.BlockSpec((1, d), lambda i, j: (0, 0)),
            pl.BlockSpec((d, tn), lambda i, j: (0, j)),
        ],
        out_specs=out_specs,
        scratch_shapes=[pltpu.VMEM((tm, d), BF16)],
        compiler_params=_params(("arbitrary", "arbitrary")),
        name="proj",
    )(x, g, w)
    return out if vt_col is not None else out[0]


IDX_TQ = 256
IDX_CW = 256
BF16_KEY_MIN, BF16_KEY_MAX = -(2 ** 15), 2 ** 15 - 1
STAGE2_BITS = 18


def _key32(pattern):
    return jnp.where(pattern < 0, pattern ^ 0x7FFFFFFF, pattern)


def _pattern_of_key16(k16):
    return lax.shift_left(jnp.where(k16 < 0, k16 ^ 0x7FFF, k16), 16)


def _idx_kernel(iq_ref, sm_ref, ik_ref, ikm_ref, bias_ref, sc_ref, sc16_ref, w_ref, *, seq, topk):
    qi = pl.program_id(1)
    tq, cw = IDX_TQ, IDX_CW
    nt = seq // cw
    n_x = qi + 1
    kf = float(topk)

    w_ref[...] = sm_ref[0][:, 0:LANES].T
    key_pos = lax.broadcasted_iota(I32, (cw, tq), 0)
    q_pos = qi * tq + lax.broadcasted_iota(I32, (cw, tq), 1)

    def score_tile(k2):
        lhs = jnp.concatenate([k2[:, 0:LANES], k2[:, LANES:2 * LANES]], axis=0)
        acc = jnp.zeros((cw, tq), F32)
        for p in range(IDX_HEADS // 2):
            rhs = iq_ref[p // 2][:, (p % 2) * LANES:(p % 2 + 1) * LANES]
            l2 = _dot_nt(lhs, rhs)
            acc = acc + jnp.maximum(l2[0:cw], 0.0) * w_ref[2 * p:2 * p + 1, :]
            acc = acc + jnp.maximum(l2[cw:2 * cw], 0.0) * w_ref[2 * p + 1:2 * p + 2, :]
        return acc

    def x_tile(c):
        k2 = ik_ref[0, pl.ds(pl.multiple_of(c * cw, cw), cw), :]
        acc = jnp.where(c * cw + key_pos <= q_pos, score_tile(k2), -jnp.inf)
        sc_ref[c + 1] = acc
        sc16_ref[c + 1] = acc.astype(BF16)

    def x_tile_pair(p, carry):
        x_tile(2 * p)
        x_tile(2 * p + 1)
        return carry

    lax.fori_loop(0, n_x // 2, x_tile_pair, 0)

    @pl.when((n_x & 1) == 1)
    def _():
        x_tile(n_x - 1)
    acc_m = jnp.where(key_pos < N_META, score_tile(ikm_ref[...]), -jnp.inf)
    sc_ref[0] = acc_m
    sc16_ref[0] = acc_m.astype(BF16)
    sc_ref[n_x + 1] = jnp.full((cw, tq), -jnp.inf, F32)
    sc16_ref[n_x + 1] = jnp.full((cw, tq), -jnp.inf, BF16)
    n_pairs = (n_x + 2) // 2

    one16, zero16 = jnp.ones((cw, tq), BF16), jnp.zeros((cw, tq), BF16)

    def count16(c16):
        def tile_count(c):
            hit = jnp.where(sc16_ref[c] >= c16, one16, zero16).reshape(cw // 16, 16, tq)
            parts = [hit[i] for i in range(cw // 16)]
            while len(parts) > 1:
                parts = [parts[i] + parts[i + 1] for i in range(0, len(parts), 2)]
            return parts[0].astype(F32)

        cnt = lax.fori_loop(0, n_pairs, lambda p, a: a + (tile_count(2 * p) + tile_count(2 * p + 1)),
                            jnp.zeros((16, tq), F32))
        return jnp.sum(cnt, axis=0, keepdims=True)

    def stage1(it, k16):
        cand = k16 + lax.shift_left(jnp.int32(1), 15 - it)
        c32 = lax.bitcast_convert_type(_pattern_of_key16(cand), F32)
        c16 = jnp.broadcast_to(c32, (cw, tq)).astype(BF16)
        return jnp.where(count16(c16) >= kf, cand, k16)

    k16 = lax.fori_loop(0, 16, stage1, jnp.full((1, tq), BF16_KEY_MIN, I32))

    near_zero = jnp.logical_and(k16 >= -2, k16 <= 1)
    reach = jnp.where(near_zero, 2, 1)
    lo = _key32(_pattern_of_key16(jnp.maximum(k16 - reach, BF16_KEY_MIN)))
    hi = _key32(_pattern_of_key16(jnp.minimum(k16 + reach, BF16_KEY_MAX)))
    first_bit = jnp.where(jnp.max(jnp.where(near_zero, 1.0, 0.0)) > 0.5, 0, 1)

    def count32(cf):
        def tile_count(c):
            hit = jnp.where(sc_ref[c] >= cf, 1.0, 0.0)
            return jnp.sum(hit.reshape(cw // 8, 8, tq), axis=0)

        cnt = lax.fori_loop(0, n_pairs, lambda p, a: a + (tile_count(2 * p) + tile_count(2 * p + 1)),
                            jnp.zeros((8, tq), F32))
        return jnp.sum(cnt, axis=0, keepdims=True)

    short = (qi * tq + lax.broadcasted_iota(I32, (1, tq), 1) + 1 + N_META) <= topk

    def unsettled(cnt_t):
        return (jnp.max(jnp.where(short | (cnt_t == kf), 0.0, 1.0)) > 0.5).astype(I32)

    def stage2_cond(st):
        it, _, _, go = st
        return jnp.logical_and(it < STAGE2_BITS, go > 0)

    def stage2(st):
        it, t, cnt_t, _ = st
        cand = t + lax.shift_left(jnp.int32(1), STAGE2_BITS - 1 - it)
        cnt = count32(lax.bitcast_convert_type(jnp.where(cand < 0, cand ^ 0x7FFFFFFF, cand), F32))
        ok = jnp.logical_and(cand < hi, cnt >= kf)
        cnt_t = jnp.where(ok, cnt, cnt_t)
        return it + 1, jnp.where(ok, cand, t), cnt_t, unsettled(cnt_t)

    _, t, cnt_t, tied_any = lax.while_loop(stage2_cond, stage2,
                                           (first_bit.astype(I32), lo, jnp.full((1, tq), -1.0, F32), jnp.int32(1)))
    thr = lax.bitcast_convert_type(jnp.where(t < 0, t ^ 0x7FFFFFFF, t), F32)
    thr = jnp.where(short, jnp.finfo(F32).min, thr)

    def write_bias(bias_of):
        for c in range(nt):
            @pl.when(c < n_x)
            def _(c=c):
                bias_ref[0, c * cw:(c + 1) * cw, :] = bias_of(
                    sc_ref[c + 1], N_META + c * cw + key_pos).astype(BF16)

            @pl.when(c >= n_x)
            def _(c=c):
                bias_ref[0, c * cw:(c + 1) * cw, :] = jnp.full((cw, tq), NEG_BIAS, BF16)

        meta_pos = lax.broadcasted_iota(I32, (LANES, tq), 0)
        bias_ref[0, seq:seq + LANES, :] = bias_of(sc_ref[0, 0:LANES, :], meta_pos).astype(BF16)

    @pl.when(tied_any == 0)
    def _():
        write_bias(lambda s, pos: jnp.where(s >= thr, 0.0, NEG_BIAS))

    @pl.when(tied_any > 0)
    def _():
        def count_where(hit_of):
            def tile_count(slot):
                pos = key_pos + jnp.where(slot == 0, 0, N_META + (slot - 1) * cw)
                return jnp.sum(hit_of(sc_ref[slot], pos).reshape(cw // 8, 8, tq), axis=0)

            cnt = lax.fori_loop(0, 2 * n_pairs, lambda s_, a: a + tile_count(s_), jnp.zeros((8, tq), F32))
            return jnp.sum(cnt, axis=0, keepdims=True)

        need = kf - count_where(lambda s, pos: jnp.where(s > thr, 1.0, 0.0))
        pos_bits = int(seq + N_META).bit_length()

        def pos_bit(i, cut):
            cand = cut | lax.shift_left(jnp.int32(1), pos_bits - 1 - i)
            below = count_where(lambda s, pos: jnp.where(s == thr, jnp.where(pos < cand, 1.0, 0.0), 0.0))
            return jnp.where(below < need, cand, cut)

        cut = lax.fori_loop(0, pos_bits, pos_bit, jnp.zeros((1, tq), I32))
        tied = jnp.logical_and(jnp.logical_not(short), cnt_t != kf)
        cut = jnp.where(tied, cut, jnp.iinfo(I32).max)
        write_bias(lambda s, pos: jnp.where(
            s > thr, 0.0, jnp.where(s == thr, jnp.where(pos <= cut, 0.0, NEG_BIAS), NEG_BIAS)))


def _idx(p16, p32, ik2_meta, batch, seq, topk):
    nq = seq // IDX_TQ
    nt = seq // IDX_CW
    return pl.pallas_call(
        functools.partial(_idx_kernel, seq=seq, topk=topk),
        out_shape=jax.ShapeDtypeStruct((batch, seq + LANES, seq), BF16),
        grid=(batch, nq),
        in_specs=[
            pl.BlockSpec((4, IDX_TQ, SLAB), lambda b, q: (P16_IQ, b * nq + q, 0)),
            pl.BlockSpec((1, IDX_TQ, SLAB), lambda b, q: (P32_SMALL_SLAB, b * nq + q, 0)),
            pl.BlockSpec((1, seq, SLAB), lambda b, q: (P16_IK2_SLAB, b, 0)),
            pl.BlockSpec((IDX_CW, SLAB), lambda b, q: (0, 0)),
        ],
        out_specs=pl.BlockSpec((1, seq + LANES, IDX_TQ), lambda b, q: (b, 0, q)),
        scratch_shapes=[
            pltpu.VMEM((nt + 2, IDX_CW, IDX_TQ), F32),
            pltpu.VMEM((nt + 2, IDX_CW, IDX_TQ), BF16),
            pltpu.VMEM((LANES, IDX_TQ), F32),
        ],
        compiler_params=_params(("parallel", "arbitrary")),
        name="idx",
    )(p16, p32, p16, ik2_meta)


ATTN_AHEAD = 2


def _attn_kernel(qt_ref, kt_ref, q_ref, k_ref, vt_ref, b_ref, bm_ref, km_ref, vtm_ref, o_ref,
                 m_ref, acc_ref, *, tq, tk):
    p = pl.program_id(1)
    qi, ki = qt_ref[p], kt_ref[p]

    def head(ref, h):
        return ref[h // 2][:, (h % 2) * LANES:(h % 2 + 1) * LANES]

    def accumulate(kref, vtref, bias):
        def scores(h):
            return _dot_nt(head(kref, h), head(q_ref, h)) + bias

        pending = [scores(h) for h in range(ATTN_AHEAD)]
        for h in range(DSA_HEADS):
            s = pending.pop(0)
            if h + ATTN_AHEAD < DSA_HEADS:
                pending.append(scores(h + ATTN_AHEAD))
            m_prev = m_ref[h]
            m_new = jnp.maximum(m_prev, jnp.max(s, axis=0, keepdims=True))
            alpha = jnp.exp2(m_prev - m_new)
            pr = jnp.exp2(s - m_new).astype(BF16)
            vt = vtref[h * VT_ROWS:(h + 1) * VT_ROWS, :]
            acc_ref[h] = alpha * acc_ref[h] + _dot(vt, pr)
            m_ref[h] = m_new

    @pl.when(ki == 0)
    def _():
        m_ref[...] = jnp.full(m_ref.shape, -jnp.inf, F32)
        acc_ref[...] = jnp.zeros_like(acc_ref)
        accumulate(km_ref, vtm_ref, bm_ref[0].astype(F32))

    accumulate(k_ref, vt_ref, b_ref[0].astype(F32))

    @pl.when(ki == (qi * tq + tq - 1) // tk)
    def _():
        for h in range(DSA_HEADS):
            hs = slice(h * DSA_HEAD_DIM, (h + 1) * DSA_HEAD_DIM)
            acc = acc_ref[h]
            o_ref[:, hs] = (acc[0:DSA_HEAD_DIM] / acc[DSA_HEAD_DIM:DSA_HEAD_DIM + 1]).T.astype(o_ref.dtype)


def _attn(p16, vt, bias, p16_meta, vt_meta, batch, seq, tq, tk):
    nq, nk = seq // tq, seq // tk
    pairs = [(q, k) for q in range(nq) for k in range((q * tq + tq - 1) // tk + 1)]
    q_tab = jnp.asarray([q for q, _ in pairs], I32)
    k_tab = jnp.asarray([k for _, k in pairs], I32)
    grid_spec = pltpu.PrefetchScalarGridSpec(
        num_scalar_prefetch=2,
        grid=(batch, len(pairs)),
        in_specs=[
            pl.BlockSpec((4, tq, SLAB), lambda b, p, qt, kt: (P16_DQ, b * nq + qt[p], 0)),
            pl.BlockSpec((4, tk, SLAB), lambda b, p, qt, kt: (P16_DK, b * nk + kt[p], 0)),
            pl.BlockSpec((DSA_HEADS * VT_ROWS, tk), lambda b, p, qt, kt: (0, b * nk + kt[p])),
            pl.BlockSpec((1, tk, tq), lambda b, p, qt, kt: (b, kt[p], qt[p])),
            pl.BlockSpec((1, LANES, tq), lambda b, p, qt, kt: (b, seq // LANES, qt[p])),
            pl.BlockSpec((4, LANES, SLAB), lambda b, p, qt, kt: (P16_DK, 0, 0)),
            pl.BlockSpec((DSA_HEADS * VT_ROWS, LANES), lambda b, p, qt, kt: (0, 0)),
        ],
        out_specs=pl.BlockSpec((tq, DSA_WIDTH), lambda b, p, qt, kt: (b * nq + qt[p], 0)),
        scratch_shapes=[
            pltpu.VMEM((DSA_HEADS, 1, tq), F32),
            pltpu.VMEM((DSA_HEADS, VT_ROWS, tq), F32),
        ],
    )
    return pl.pallas_call(
        functools.partial(_attn_kernel, tq=tq, tk=tk),
        out_shape=jax.ShapeDtypeStruct((batch * seq, DSA_WIDTH), BF16),
        grid_spec=grid_spec,
        compiler_params=_params(("parallel", "arbitrary")),
        name="attn",
    )(q_tab, k_tab, p16, p16, vt, bias, bias, p16_meta, vt_meta)


def _log_sigmoid(x):
    return jnp.minimum(x, 0.0) - jnp.log1p(jnp.exp(-jnp.abs(x)))


def _gla_kernel(gq_ref, gk_ref, go_ref, sm_ref, gv_ref, mk_ref, mv_ref, msm_ref, wgate_ref, bgate_ref,
                ng_ref, o_ref, st_ref, bc_ref):
    c = GLA_CHUNK

    def log_decay(small):
        return _log_sigmoid(_dot_f32(small, wgate_ref[...]) + bgate_ref[...]) / GLA_TAU

    def lower_ones(n):
        r = lax.broadcasted_iota(I32, (n, n), 0)
        cc = lax.broadcasted_iota(I32, (n, n), 1)
        return r >= cc

    @pl.when(pl.program_id(1) == 0)
    def _():
        bcm = _dot_exact_lhs01(jnp.where(lower_ones(N_META), 1.0, 0.0).astype(BF16),
                               log_decay(msm_ref[0]))
        for h in range(GLA_HEADS):
            ks = slice(h * GLA_DK, (h + 1) * GLA_DK)
            kd = mk_ref[h // 2][:, (h % 2) * GLA_DK:(h % 2 + 1) * GLA_DK] * jnp.exp(
                bcm[N_META - 1:N_META, ks] - bcm[:, ks])
            st_ref[h] = _dot_tn(mv_ref[h], kd.astype(BF16))

    rows = bc_ref.shape[0]
    r_i = lax.broadcasted_iota(I32, (rows, rows), 0)
    c_i = lax.broadcasted_iota(I32, (rows, rows), 1)
    chunk_tri = jnp.logical_and(r_i >= c_i, r_i // c == c_i // c)
    bc_ref[...] = _dot_exact_lhs01(jnp.where(chunk_tri, 1.0, 0.0).astype(BF16), log_decay(sm_ref[0]))
    causal = lower_ones(c)
    sub = GLA_SUB
    lane = lax.broadcasted_iota(I32, (sub, c), 1)

    for ch in range(rows // c):
        base = ch * c
        for h in range(GLA_HEADS):
            ks = slice(h * GLA_DK, (h + 1) * GLA_DK)
            hsl = slice((h % 2) * GLA_DK, (h % 2 + 1) * GLA_DK)
            q = gq_ref[h // 2][base:base + c, hsl] * (GLA_DK ** -0.5)
            k = gk_ref[h // 2][base:base + c, hsl]
            v = gv_ref[h][base:base + c]
            bc = bc_ref[base:base + c, ks]

            blocks = []
            for s0 in range(0, c, sub):
                q_s, bc_s = q[s0:s0 + sub], bc[s0:s0 + sub]
                diag = jnp.zeros((sub, c), F32)
                for j in range(s0, s0 + sub):
                    e = jnp.exp(jnp.minimum(bc_s - bc_ref[base + j:base + j + 1, ks], 0.0))
                    k_j = gk_ref[h // 2][base + j:base + j + 1, hsl]
                    diag = jnp.where(lane == j, jnp.sum(q_s * k_j * e, axis=1, keepdims=True), diag)
                if s0:
                    b_edge = bc_ref[base + s0 - 1:base + s0, ks]
                    q_t = q_s * jnp.exp(bc_s - b_edge)
                    k_t = k * jnp.exp(jnp.minimum(b_edge - bc, 0.0))
                    diag = jnp.where(lane < s0, _dot_nt(q_t.astype(BF16), k_t.astype(BF16)), diag)
                blocks.append(diag)
            attn = jnp.where(causal, jnp.concatenate(blocks, axis=0), 0.0)

            state = st_ref[h]
            b_last = bc[c - 1:c, :]
            o = _dot_nt((q * jnp.exp(bc)).astype(BF16), state.astype(BF16))
            o = o + _dot(attn.astype(BF16), v)
            kd = k * jnp.exp(b_last - bc)
            st_ref[h] = state * jnp.exp(b_last) + _dot_tn(v, kd.astype(BF16))

            g = go_ref[h][base:base + c]
            y = _rms(o, ng_ref[...]) * (g * jax.nn.sigmoid(g))
            o_ref[base:base + c, h * GLA_DV:(h + 1) * GLA_DV] = y.astype(o_ref.dtype)


def _gla(p32, p16, p32_meta, p16_meta, w_gate_pad, b_gate, norm_g, batch, seq):
    c = _tile(seq, GLA_ROWS)
    ns = seq // c
    row = lambda b, i: b * ns + i
    return pl.pallas_call(
        _gla_kernel,
        out_shape=jax.ShapeDtypeStruct((batch * seq, GLA_WIDTH), BF16),
        grid=(batch, ns),
        in_specs=[
            pl.BlockSpec((2, c, SLAB), lambda b, i: (P32_GQ_SLAB // 2, row(b, i), 0)),
            pl.BlockSpec((2, c, SLAB), lambda b, i: (P32_GK_SLAB // 2, row(b, i), 0)),
            pl.BlockSpec((4, c, SLAB), lambda b, i: (P32_GOUT_SLAB // 4, row(b, i), 0)),
            pl.BlockSpec((1, c, SLAB), lambda b, i: (P32_SMALL_SLAB, row(b, i), 0)),
            pl.BlockSpec((4, c, SLAB), lambda b, i: (P16_GV, row(b, i), 0)),
            pl.BlockSpec((2, N_META, SLAB), lambda b, i: (P32_GK_SLAB // 2, 0, 0)),
            pl.BlockSpec((4, N_META, SLAB), lambda b, i: (P16_GV, 0, 0)),
            pl.BlockSpec((1, N_META, SLAB), lambda b, i: (P32_SMALL_SLAB, 0, 0)),
            pl.BlockSpec((SLAB, GLA_KEY_WIDTH), lambda b, i: (0, 0)),
            pl.BlockSpec((1, GLA_KEY_WIDTH), lambda b, i: (0, 0)),
            pl.BlockSpec((1, GLA_DV), lambda b, i: (0, 0)),
        ],
        out_specs=pl.BlockSpec((c, GLA_WIDTH), lambda b, i: (row(b, i), 0)),
        scratch_shapes=[
            pltpu.VMEM((GLA_HEADS, GLA_DV, GLA_DK), F32),
            pltpu.VMEM((c, GLA_KEY_WIDTH), F32),
        ],
        compiler_params=_params(("parallel", "arbitrary")),
        name="gla",
    )(p32, p32, p32, p32, p16, p32_meta, p16_meta, p32_meta, w_gate_pad, b_gate, norm_g)


def _out_kernel(h_ref, a_ref, g_ref, wa_ref, wg_ref, pg_ref, o_ref):
    m = _dot(a_ref[...], wa_ref[...]) + _dot(g_ref[...], wg_ref[...])
    o_ref[...] = h_ref[...] + _rms(m, pg_ref[...])


def _out(hs, o_dsa, o_gla, w_dsa, w_gla, post_g, tm):
    r, d = hs.shape
    return pl.pallas_call(
        _out_kernel,
        out_shape=jax.ShapeDtypeStruct((r, d), F32),
        grid=(r // tm,),
        in_specs=[
            pl.BlockSpec((tm, d), lambda i: (i, 0)),
            pl.BlockSpec((tm, DSA_WIDTH), lambda i: (i, 0)),
            pl.BlockSpec((tm, GLA_WIDTH), lambda i: (i, 0)),
            pl.BlockSpec((DSA_WIDTH, d), lambda i: (0, 0)),
            pl.BlockSpec((GLA_WIDTH, d), lambda i: (0, 0)),
            pl.BlockSpec((1, d), lambda i: (0, 0)),
        ],
        out_specs=pl.BlockSpec((tm, d), lambda i: (i, 0)),
        compiler_params=_params(("parallel",)),
        name="outproj",
    )(hs, o_dsa, o_gla, w_dsa, w_gla, post_g)


def _tile(n, pref):
    t = min(n, pref)
    while n % t:
        t //= 2
    return t


def _proj_weights(w_in):
    d = w_in.shape[0]
    offs = np.cumsum((0,) + IN_SPLITS)
    dq, dk, dv, iq, ik, iw, gq, gk, gv, glow, gout = (w_in[:, offs[i]:offs[i + 1]] for i in range(11))
    z = lambda n: jnp.zeros((d, n), w_in.dtype)
    ik2 = jnp.concatenate([ik, z(IDX_DIM), z(IDX_DIM), ik], axis=1)
    q_scale = DSA_HEAD_DIM ** -0.5 * float(np.log2(np.e))
    w16 = jnp.concatenate([dq * q_scale, dk, dv, iq * (IDX_DIM ** -0.5), gv, ik2], axis=1)
    small = jnp.concatenate([iw * (IDX_HEADS ** -0.5), glow, z(SLAB - IDX_HEADS - GLA_GATE_RANK)], axis=1)
    w32 = jnp.concatenate([gq, gk, gout, small], axis=1)
    return w16.astype(BF16), w32.astype(BF16)


def kernel(x, meta_tokens, ffn1_pre_g, ffn1_w_gate, ffn1_w_up, ffn1_w_down, ffn1_post_g, mix_pre_g, w_in, w_gate_up, b_gate, gla_norm_g, w_out, mix_post_g, ffn2_pre_g, ffn2_w_gate, ffn2_w_up, ffn2_w_down, ffn2_post_g):
    batch, seq, d = x.shape
    depth = w_in.shape[0]
    assert depth == 1, "the meta rows skip the mixer, which is only valid for the last layer"
    assert seq % (2 * IDX_CW) == 0 and meta_tokens.shape[0] == N_META, "score tiles are visited in pairs"
    d_ff = ffn1_w_gate.shape[-1]
    topk = min(TOPK_MAX, seq // 4)
    rows = batch * seq
    tm = _tile(rows, 512)
    tf = _tile(d_ff, 512)
    tq = _tile(seq, 512)
    tk = _tile(seq, 1024)

    hs = x.reshape(rows, d)
    hm = meta_tokens.astype(x.dtype)
    row2 = lambda v: v.reshape(1, -1)
    for l in range(depth):
        ffn1 = (row2(ffn1_pre_g[l]), ffn1_w_gate[l].astype(BF16), ffn1_w_up[l].astype(BF16),
                ffn1_w_down[l].astype(BF16), row2(ffn1_post_g[l]))
        hs = _ffn(hs, *ffn1, tm=tm, tf=tf)
        hm = _ffn(hm, *ffn1, tm=N_META, tf=tf)

        w16, w32 = _proj_weights(w_in[l])
        pre = row2(mix_pre_g[l])
        proj16 = functools.partial(_proj, g=pre, w=w16, out_dtype=BF16, slabs_per_step=P16_SLABS_PER_STEP)
        proj32 = functools.partial(_proj, g=pre, w=w32, out_dtype=F32, slabs_per_step=P32_SLABS_PER_STEP)
        tm_proj = _tile(rows, 1024)
        (p16, vt), p32 = proj16(hs, tm=tm_proj, vt_col=4 * SLAB * P16_DV), proj32(hs, tm=tm_proj)
        p16m, p32m = proj16(hm, tm=N_META), proj32(hm, tm=N_META)

        pad_rows = lambda a, n: jnp.pad(a, ((0, 0), (0, n - a.shape[1]), (0, 0)))
        bias = _idx(p16, p32, pad_rows(p16m, IDX_CW)[P16_IK2_SLAB], batch, seq, topk)
        p16m_pad = pad_rows(p16m, LANES)

        def v_t(a):
            vt = jnp.transpose(a[4 * P16_DV:4 * P16_DV + 4], (0, 2, 1)).reshape(DSA_HEADS, DSA_HEAD_DIM, -1)
            ones = jnp.zeros((DSA_HEADS, VT_ROWS - DSA_HEAD_DIM, vt.shape[-1]), vt.dtype).at[:, 0].set(1)
            return jnp.concatenate([vt, ones], axis=1).reshape(DSA_HEADS * VT_ROWS, -1)

        o_dsa = _attn(p16, vt, bias, p16m_pad, v_t(p16m_pad), batch, seq, tq, tk)

        w_gate_pad = jnp.zeros((SLAB, GLA_KEY_WIDTH), F32).at[
            IDX_HEADS:IDX_HEADS + GLA_GATE_RANK].set(w_gate_up[l])
        o_gla = _gla(p32, p16, p32m, p16m, w_gate_pad, row2(b_gate[l]), row2(gla_norm_g[l]), batch, seq)

        wo = w_out[l].astype(BF16)
        hs = _out(hs, o_dsa, o_gla, wo[:DSA_WIDTH], wo[DSA_WIDTH:], row2(mix_post_g[l]), tm)
        ffn2 = (row2(ffn2_pre_g[l]), ffn2_w_gate[l].astype(BF16), ffn2_w_up[l].astype(BF16),
                ffn2_w_down[l].astype(BF16), row2(ffn2_post_g[l]))
        hs = _ffn(hs, *ffn2, tm=tm, tf=tf)
    return hs.reshape(batch, seq, d)
```
